```python
import jax
import jax.numpy as jnp
from jax import lax
import numpy as np

D_MODEL = 1024
BATCH = 2
SEQ = 8192
DEPTH = 2

GRID_W = 64
CTX_LEN = 256
N_MOD = 9
D_FF = 2816
N_BRANCH = 4
BRANCH_WIDTH = D_MODEL // 2
GLA_HEADS = 4
GLA_DV = BRANCH_WIDTH // GLA_HEADS
GLA_DK = GLA_DV // 2
GLA_RANK = 16
GLA_NORMALIZER = 16.0
GLA_CHUNK = 64
SGU_GROUPS = 4
SGU_GC = BRANCH_WIDTH // SGU_GROUPS
SGU_CHUNK = 128
FNET_GROUPS = 4
FNET_GC = BRANCH_WIDTH // FNET_GROUPS
CONV_TAPS = 3
EPS = 1e-6
IN_SIZES = (GLA_HEADS * GLA_DK, GLA_HEADS * GLA_DK, BRANCH_WIDTH, BRANCH_WIDTH, 2 * GLA_RANK,
            BRANCH_WIDTH, BRANCH_WIDTH, BRANCH_WIDTH, BRANCH_WIDTH, BRANCH_WIDTH, BRANCH_WIDTH)
COL_Q = GLA_HEADS * GLA_DK
COL_K = 2 * GLA_HEADS * GLA_DK
COL_V = COL_K + BRANCH_WIDTH
COL_R = COL_V + BRANCH_WIDTH
COL_A = COL_R + 2 * GLA_RANK
IN_WIDTH = COL_A + 6 * BRANCH_WIDTH

kernel_name = 'hybrid_gla_sgu_fnet_conv_dit_block'


def rmsnorm(x, g):
    x32 = x.astype(jnp.float32)
    y = x32 * lax.rsqrt(jnp.mean(x32 * x32, axis=-1, keepdims=True) + EPS)
    return (y * g.astype(jnp.float32)).astype(x.dtype)


def layernorm(x):
    x32 = x.astype(jnp.float32)
    xc = x32 - jnp.mean(x32, axis=-1, keepdims=True)
    var = jnp.mean(xc * xc, axis=-1, keepdims=True)
    return (xc * lax.rsqrt(var + EPS)).astype(x.dtype)


def adaln(cvec, w, b):
    m = jax.nn.silu(cvec) @ w + b
    return jnp.transpose(m.reshape(cvec.shape[0], N_MOD, D_MODEL), (1, 0, 2))[:, :, None, :]


def modulate(h, shift, scale):
    return h * (1.0 + scale) + shift


def swiglu(h, w1, w3, w2):
    return (jax.nn.silu(h @ w1) * (h @ w3)) @ w2


def ffn_half_step(s, g, mods, j, w1, w3, w2):
    h = modulate(rmsnorm(s, g), mods[j], mods[j + 1])
    return s + 0.5 * mods[j + 2] * swiglu(h, w1, w3, w2)


def split_columns(p, sizes):
    return jnp.split(p, np.cumsum(sizes)[:-1].tolist(), axis=-1)


def gla_scan(q, k, v, g, s0):
    bsz, L, H, _ = k.shape
    dv = v.shape[-1]
    n = L // GLA_CHUNK

    def chunks(t):
        return jnp.transpose(t.reshape(bsz, n, GLA_CHUNK, H, t.shape[-1]), (1, 0, 3, 2, 4))

    mask = jnp.tril(jnp.ones((GLA_CHUNK, GLA_CHUNK), dtype=bool))[:, :, None]
    xs = (chunks(k), chunks(v), chunks(g)) if q is None else (chunks(k), chunks(v), chunks(g), chunks(q))

    def step(s, inp):
        kc, vc, gc = (t.astype(jnp.float32) for t in inp[:3])
        b = jnp.cumsum(gc, axis=2)
        b_end = b[:, :, -1:, :]
        s_new = (jnp.exp(b_end)[:, :, 0, :, None] * s
                 + jnp.einsum('bhjd,bhje->bhde', kc * jnp.exp(b_end - b), vc))
        if q is None:
            return s_new, None
        qc = inp[3].astype(jnp.float32)
        o = jnp.einsum('bhid,bhde->bhie', qc * jnp.exp(b), s)
        decay = jnp.exp(jnp.where(mask, b[:, :, :, None, :] - b[:, :, None, :, :], -jnp.inf))
        a = jnp.einsum('bhid,bhjd,bhijd->bhij', qc, kc, decay)
        return s_new, o + jnp.einsum('bhij,bhje->bhie', a, vc)

    s_fin, o = lax.scan(step, s0, xs)
    if q is None:
        return None, s_fin
    o = jnp.transpose(o, (1, 0, 3, 2, 4)).reshape(bsz, L, H, dv).astype(v.dtype)
    return o, s_fin


def gla_mixer(q, k, v, r, a, w_a2, b_a2, g_o, s_f0, s_b0):
    bsz, L = k.shape[:2]
    kh = k.reshape(bsz, L, GLA_HEADS, GLA_DK)
    vh = v.reshape(bsz, L, GLA_HEADS, GLA_DV)

    def log_decay(a_dir, w, bias):
        logits = (a_dir @ w + bias).astype(jnp.float32)
        return (jax.nn.log_sigmoid(logits) / GLA_NORMALIZER).reshape(bsz, L, GLA_HEADS, GLA_DK)

    def flip(t):
        return jnp.flip(t, axis=1)

    g_f = log_decay(a[..., :GLA_RANK], w_a2[0], b_a2[0])
    g_b = log_decay(a[..., GLA_RANK:], w_a2[1], b_a2[1])
    if q is None:
        q_f, q_b = None, None
    else:
        qh = q.reshape(bsz, L, GLA_HEADS, GLA_DK) * (GLA_DK ** -0.5)
        q_f, q_b = qh, flip(qh)
    o_f, s_f = gla_scan(q_f, kh, vh, g_f, s_f0)
    o_b, s_b = gla_scan(q_b, flip(kh), flip(vh), flip(g_b), s_b0)
    if q is None:
        return None, s_f, s_b
    o = rmsnorm(o_f + flip(o_b), g_o) * jax.nn.silu(r.reshape(bsz, L, GLA_HEADS, GLA_DV))
    return o.reshape(bsz, L, BRANCH_WIDTH), s_f, s_b


def spatial_gating(u, v, w_s, b_s):
    bsz, L, _ = u.shape
    shape = (bsz, L // SGU_CHUNK, SGU_CHUNK, SGU_GROUPS, SGU_GC)
    z = layernorm(v.reshape(shape))
    s = jnp.einsum('gij,bnjgc->bnigc', w_s, z) + jnp.transpose(b_s)[None, None, :, :, None]
    return (u.reshape(shape) * s).reshape(bsz, L, BRANCH_WIDTH)


def fourier_mix(f):
    bsz, L, _ = f.shape
    f32 = f.astype(jnp.float32).reshape(bsz, L, FNET_GROUPS, FNET_GC)
    y = jnp.fft.fftn(f32, axes=(1, 3), norm='ortho').real
    return y.reshape(bsz, L, BRANCH_WIDTH).astype(f.dtype)


def conv3(z, w, axis):
    pad = [(0, 0)] * z.ndim
    pad[axis] = (1, 1)
    zp = jnp.pad(z, pad)
    n = z.shape[axis]
    tap = lambda s: lax.slice_in_dim(zp, s, s + n, axis=axis)
    return tap(0) * w[0] + tap(1) * w[1] + tap(2) * w[2]


def short_conv(cb, cc, cx, w, rows):
    z = cc * cx
    if rows is None:
        y = conv3(z, w, axis=1)
    else:
        bsz, L, ch = z.shape
        y = conv3(z.reshape(bsz, rows, GRID_W, ch), w, axis=2).reshape(bsz, L, ch)
    return cb * y


def mixer_branches(p, o_gla, w_sgu, b_sgu, w_conv, rows):
    su, sv, fn, cb, cc, cx = p[5:]
    return (o_gla, spatial_gating(su, sv, w_sgu, b_sgu), fourier_mix(fn),
            short_conv(cb, cc, cx, w_conv, rows))


def merge(h, branches, w_branch, w_gate, b_gate, w_out):
    m = None
    for k in range(N_BRANCH):
        term = jax.nn.sigmoid(h @ w_gate[k] + b_gate[k]) * (branches[k] @ w_branch[k])
        m = term if m is None else m + term
    return m @ w_out


def setup_inputs(seed: int = 0) -> dict:
    key = jax.random.key(seed)
    ks = jax.random.split(key, 22)

    def nrm(k, shape, scale):
        return scale * jax.random.normal(k, shape, jnp.float32)

    return {
        'x': nrm(ks[0], (BATCH, SEQ, D_MODEL), 1.0),
        'c': nrm(ks[1], (BATCH, D_MODEL), 1.0),
        'ctx': nrm(ks[2], (BATCH, CTX_LEN, D_MODEL), 1.0),
        'c_ctx': nrm(ks[3], (D_MODEL,), 1.0),
        'w_ada': nrm(ks[4], (DEPTH, D_MODEL, N_MOD * D_MODEL), 0.5 * D_MODEL ** -0.5),
        'b_ada': nrm(ks[5], (DEPTH, N_MOD * D_MODEL), 0.02),
        'g_norm': 1.0 + nrm(ks[6], (DEPTH, 3, D_MODEL), 0.05),
        'w_ff1': nrm(ks[7], (DEPTH, 2, D_MODEL, D_FF), D_MODEL ** -0.5),
        'w_ff3': nrm(ks[8], (DEPTH, 2, D_MODEL, D_FF), D_MODEL ** -0.5),
        'w_ff2': nrm(ks[9], (DEPTH, 2, D_FF, D_MODEL), D_FF ** -0.5),
        'w_in': nrm(ks[10], (DEPTH, D_MODEL, IN_WIDTH), D_MODEL ** -0.5),
        'w_gla_a2': nrm(ks[11], (DEPTH, 2, GLA_RANK, GLA_HEADS * GLA_DK), GLA_RANK ** -0.5),
        'b_gla_a2': 2.0 + nrm(ks[12], (DEPTH, 2, GLA_HEADS * GLA_DK), 0.1),
        'g_gla_norm': 1.0 + nrm(ks[13], (DEPTH, GLA_HEADS, GLA_DV), 0.05),
        'w_sgu': nrm(ks[14], (DEPTH, SGU_GROUPS, SGU_CHUNK, SGU_CHUNK), 0.5 * SGU_CHUNK ** -0.5),
        'b_sgu': 1.0 + nrm(ks[15], (DEPTH, SGU_GROUPS, SGU_CHUNK), 0.1),
        'w_conv': nrm(ks[16], (DEPTH, CONV_TAPS, BRANCH_WIDTH), CONV_TAPS ** -0.5),
        'w_branch': nrm(ks[17], (DEPTH, N_BRANCH, BRANCH_WIDTH, D_MODEL), BRANCH_WIDTH ** -0.5),
        'w_gate': nrm(ks[18], (DEPTH, N_BRANCH, D_MODEL, D_MODEL), D_MODEL ** -0.5),
        'b_gate': nrm(ks[19], (DEPTH, N_BRANCH, D_MODEL), 0.02),
        'w_out': nrm(ks[20], (DEPTH, D_MODEL, D_MODEL), D_MODEL ** -0.5),
        'g_final': 1.0 + nrm(ks[21], (D_MODEL,), 0.05),
    }


def reference(x, c, ctx, c_ctx, w_ada, b_ada, g_norm, w_ff1, w_ff3, w_ff2, w_in, w_gla_a2,
              b_gla_a2, g_gla_norm, w_sgu, b_sgu, w_conv, w_branch, w_gate, b_gate, w_out, g_final):
    rows = x.shape[1] // GRID_W
    s_zero = jnp.zeros((ctx.shape[0], GLA_HEADS, GLA_DK, GLA_DV), jnp.float32)
    for i in range(DEPTH):
        last = i == DEPTH - 1
        m_lat = adaln(c, w_ada[i], b_ada[i])
        m_ctx = adaln(c_ctx[None, :], w_ada[i], b_ada[i])
        gla_w = (w_gla_a2[i], b_gla_a2[i], g_gla_norm[i])

        x = ffn_half_step(x, g_norm[i, 0], m_lat, 0, w_ff1[i, 0], w_ff3[i, 0], w_ff2[i, 0])
        ctx = ffn_half_step(ctx, g_norm[i, 0], m_ctx, 0, w_ff1[i, 0], w_ff3[i, 0], w_ff2[i, 0])

        h_c = modulate(rmsnorm(ctx, g_norm[i, 1]), m_ctx[3], m_ctx[4])
        if last:
            wi = w_in[i]
            _, s_f, s_b = gla_mixer(None, h_c @ wi[:, COL_Q:COL_K], h_c @ wi[:, COL_K:COL_V], None,
                                    h_c @ wi[:, COL_R:COL_A], *gla_w, s_zero, s_zero)
        else:
            p_c = split_columns(h_c @ w_in[i], IN_SIZES)
            o_c, s_f, s_b = gla_mixer(*p_c[:5], *gla_w, s_zero, s_zero)

        h = modulate(rmsnorm(x, g_norm[i, 1]), m_lat[3], m_lat[4])
        p_l = split_columns(h @ w_in[i], IN_SIZES)
        o_l, _, _ = gla_mixer(*p_l[:5], *gla_w, s_f, s_b)
        y = merge(h, mixer_branches(p_l, o_l, w_sgu[i], b_sgu[i], w_conv[i], rows),
                  w_branch[i], w_gate[i], b_gate[i], w_out[i])
        x = x + m_lat[5] * y

        x = ffn_half_step(x, g_norm[i, 2], m_lat, 6, w_ff1[i, 1], w_ff3[i, 1], w_ff2[i, 1])

        if not last:
            y_c = merge(h_c, mixer_branches(p_c, o_c, w_sgu[i], b_sgu[i], w_conv[i], None),
                        w_branch[i], w_gate[i], b_gate[i], w_out[i])
            ctx = ctx + m_ctx[5] * y_c
            ctx = ffn_half_step(ctx, g_norm[i, 2], m_ctx, 6, w_ff1[i, 1], w_ff3[i, 1], w_ff2[i, 1])
    return rmsnorm(x, g_final)
```

```python
import functools
import math

import numpy as np
import jax
import jax.numpy as jnp
from jax import lax
from jax.experimental import pallas as pl
from jax.experimental.pallas import tpu as pltpu

D = 1024
DEPTH = 2
GRID_W = 64
N_MOD = 9
D_FF = 2816
BW = 512
HEADS = 4
DV = 128
DK = 64
RANK = 16
GLA_NORMALIZER = 16.0
SGU_CHUNK = 128
GC = 128
NGROUP = 4
EPS = 1e-6

TM_LAT = 512
FF_CHUNK = 256
GLA_C = 128
GLA_SUB = 16
GLA_NSUB = GLA_C // GLA_SUB
FFT_N1 = 64
FFT_N2 = 128
FFT_COLS = 4096
ADA_TN = 1152
VMEM_LIMIT = 56 * 1024 * 1024

BF = jnp.bfloat16
F32 = jnp.float32

_NT = (((1,), (1,)), ((), ()))


def _dot(a, b):
    return jnp.dot(a, b, preferred_element_type=F32)


def _dot_nt(a, b):
    return lax.dot_general(a, b, _NT, preferred_element_type=F32)


def _sigmoid(x):
    return 1.0 / (1.0 + jnp.exp(-x))


def _silu(x):
    return x * _sigmoid(x)


def _log_sigmoid(x):
    return jnp.minimum(x, 0.0) - jnp.log(1.0 + jnp.exp(-jnp.abs(x)))


def _split_bf16(x):
    hi = x.astype(BF)
    lo = (x - hi.astype(F32)).astype(BF)
    return hi, lo


def _bf16_const(a):
    return jnp.asarray(a, F32).astype(BF)


def _const_spec(shape):
    nd = len(shape)
    return pl.BlockSpec(shape, lambda *_: (0,) * nd, pipeline_mode=pl.Buffered(1))


def _params(n_grid):
    return pltpu.CompilerParams(dimension_semantics=("arbitrary",) * n_grid,
                                vmem_limit_bytes=VMEM_LIMIT)


def _norm_mod(x, g, shift, scale):
    hn = x * lax.rsqrt(jnp.mean(x * x, axis=-1, keepdims=True) + EPS) * g
    return hn * (1.0 + scale) + shift


def _adaln_kernel(cb_ref, w_ref, b_ref, o_ref):
    for r in range(3):
        cv = cb_ref[r]
        s = _silu(cv)
        for j in range(ADA_TN // 128):
            w = w_ref[:, j * 128:(j + 1) * 128]
            o_ref[r:r + 1, j * 128:(j + 1) * 128] = (
                jnp.sum(w * s, axis=0, keepdims=True) + b_ref[:, j * 128:(j + 1) * 128])


def _adaln(cb, w_ada, b_ada):
    nmod = N_MOD * D
    return pl.pallas_call(
        _adaln_kernel,
        grid=(DEPTH, nmod // ADA_TN),
        in_specs=[
            pl.BlockSpec((3, D, 128), lambda l, j: (0, 0, 0)),
            pl.BlockSpec((None, D, ADA_TN), lambda l, j: (l, 0, j)),
            pl.BlockSpec((None, 1, ADA_TN), lambda l, j: (l, 0, j)),
        ],
        out_specs=pl.BlockSpec((None, 3, ADA_TN), lambda l, j: (l, 0, j)),
        out_shape=jax.ShapeDtypeStruct((DEPTH, 3, nmod), F32),
        compiler_params=_params(2),
        name="adaln",
    )(cb, w_ada, b_ada.reshape(DEPTH, 1, nmod))


def _ffn_kernel(*refs, final):
    if final:
        s_ref, g_ref, sh_ref, sc_ref, gt_ref, w1_ref, w3_ref, w2_ref, gf_ref, o_ref = refs
    else:
        s_ref, g_ref, sh_ref, sc_ref, gt_ref, w1_ref, w3_ref, w2_ref, o_ref = refs
    s = s_ref[...]
    h = _norm_mod(s, g_ref[...], sh_ref[...], sc_ref[...]).astype(BF)
    acc = jnp.zeros(s.shape, F32)
    for j in range(D_FF // FF_CHUNK):
        cols = slice(j * FF_CHUNK, (j + 1) * FF_CHUNK)
        a = _dot(h, w1_ref[:, cols])
        b = _dot(h, w3_ref[:, cols])
        u = (_silu(a) * b).astype(BF)
        acc = acc + _dot(u, w2_ref[cols, :])
    out = s + 0.5 * gt_ref[...] * acc
    if final:
        out = out * lax.rsqrt(jnp.mean(out * out, axis=-1, keepdims=True) + EPS) * gf_ref[...]
    o_ref[...] = out


def _mod_spec(tiles_per_batch, per_batch):
    if per_batch:
        return pl.BlockSpec((None, 1, D), lambda i: (i // tiles_per_batch, 0, 0))
    return pl.BlockSpec((None, 1, D), lambda i: (0, 0, 0))


def _ffn(s, g, shift, scale, gate, w1, w3, w2, *, tm, tiles_per_batch, per_batch, g_final=None):
    n = s.shape[0]
    final = g_final is not None
    ms = _mod_spec(tiles_per_batch, per_batch)
    in_specs = [
        pl.BlockSpec((tm, D), lambda i: (i, 0)),
        _const_spec((1, D)), ms, ms, ms,
        _const_spec((D, D_FF)), _const_spec((D, D_FF)), _const_spec((D_FF, D)),
    ]
    args = [s, g, shift, scale, gate, w1, w3, w2]
    if final:
        in_specs.append(_const_spec((1, D)))
        args.append(g_final)
    return pl.pallas_call(
        functools.partial(_ffn_kernel, final=final),
        grid=(n // tm,),
        in_specs=in_specs,
        out_specs=pl.BlockSpec((tm, D), lambda i: (i, 0)),
        out_shape=jax.ShapeDtypeStruct((n, D), F32),
        compiler_params=_params(1),
        name="ffn_final" if final else "ffn",
    )(*args)


def _bd_mask():
    r = lax.broadcasted_iota(jnp.int32, (2 * DK, 2 * DV), 0) // DK
    c = lax.broadcasted_iota(jnp.int32, (2 * DK, 2 * DV), 1) // DV
    return r == c


def _prep_kernel(x_ref, g_ref, sh_ref, sc_ref, wfn_ref, ccs_ref, wkT_ref, wv_ref, waT_ref,
                 wa2T_ref, ba2T_ref, mend_ref, ones_ref,
                 h_ref, wr_ref, wi_ref, u_ref, d_ref):
    h = _norm_mod(x_ref[...], g_ref[...], sh_ref[...], sc_ref[...]).astype(BF)
    h_ref[...] = h
    fn = _dot(h, wfn_ref[...]).astype(BF)
    for g in range(NGROUP):
        pq = _dot(fn[:, g * GC:(g + 1) * GC], ccs_ref[...])
        wr_ref[:, g * GC:(g + 1) * GC] = pq[:, :GC].astype(BF)
        wi_ref[:, g * GC:(g + 1) * GC] = pq[:, GC:].astype(BF)
    kT = _dot_nt(wkT_ref[...], h)
    v = _dot(h, wv_ref[...]).astype(BF)
    aT = _dot_nt(waT_ref[...], h).astype(BF)
    bd = _bd_mask()
    for dr in range(2):
        gT = _log_sigmoid(_dot(wa2T_ref[dr], aT) + ba2T_ref[dr]) * (1.0 / GLA_NORMALIZER)
        ghi, glo = _split_bf16(gT)
        ex = _dot(ghi, mend_ref[dr]) + _dot(glo, mend_ref[dr])
        tot = _dot(ghi, ones_ref[...]) + _dot(glo, ones_ref[...])
        kend = (kT * jnp.exp(ex)).astype(BF)
        for p in range(2):
            rows = slice(p * 2 * DK, (p + 1) * 2 * DK)
            upd = _dot(kend[rows], v[:, p * 2 * DV:(p + 1) * 2 * DV])
            u_ref[dr, p] = jnp.where(bd, upd, 0.0)
            d_ref[dr, p] = jnp.exp(tot[rows])


def _prep(x, g, shift, scale, w, *, tm, tiles_per_batch, per_batch):
    n = x.shape[0]
    nt = n // tm
    ms = _mod_spec(tiles_per_batch, per_batch)
    t = np.arange(tm)
    mend = np.stack([t[:, None] > t[None, :], t[:, None] < t[None, :]]).astype(np.float32)
    in_specs = [
        pl.BlockSpec((tm, D), lambda i: (i, 0)),
        _const_spec((1, D)), ms, ms,
        _const_spec((D, BW)), _const_spec((GC, 2 * GC)), _const_spec((HEADS * DK, D)),
        _const_spec((D, BW)), _const_spec((2 * RANK, D)),
        _const_spec((2, HEADS * DK, 2 * RANK)), _const_spec((2, HEADS * DK, 1)),
        _const_spec((2, tm, tm)), _const_spec((tm, 128)),
    ]
    out_specs = [
        pl.BlockSpec((tm, D), lambda i: (i, 0)),
        pl.BlockSpec((tm, BW), lambda i: (i, 0)),
        pl.BlockSpec((tm, BW), lambda i: (i, 0)),
        pl.BlockSpec((None, 2, 2, 2 * DK, 2 * DV), lambda i: (i, 0, 0, 0, 0)),
        pl.BlockSpec((None, 2, 2, 2 * DK, 128), lambda i: (i, 0, 0, 0, 0)),
    ]
    out_shape = [
        jax.ShapeDtypeStruct((n, D), BF),
        jax.ShapeDtypeStruct((n, BW), BF),
        jax.ShapeDtypeStruct((n, BW), BF),
        jax.ShapeDtypeStruct((nt, 2, 2, 2 * DK, 2 * DV), F32),
        jax.ShapeDtypeStruct((nt, 2, 2, 2 * DK, 128), F32),
    ]
    return pl.pallas_call(
        _prep_kernel,
        grid=(nt,),
        in_specs=in_specs,
        out_specs=out_specs,
        out_shape=out_shape,
        compiler_params=_params(1),
        name="prep",
    )(x, g, shift, scale, w["fn"], w["ccs"], w["kT"], w["v"], w["aT"], w["a2T"], w["ba2T"],
      jnp.asarray(mend, BF), jnp.ones((tm, 128), BF))


def _combine_kernel(u_ref, d_ref, s0_ref, o_ref, *, nt):
    dr = pl.program_id(1)

    def run(order):
        s = s0_ref[...]
        for t in order:
            o_ref[t] = s
            dd = d_ref[t]
            s = jnp.concatenate([dd, dd], axis=1) * s + u_ref[t]

    @pl.when(dr == 0)
    def _():
        run(range(nt))

    @pl.when(dr == 1)
    def _():
        run(range(nt - 1, -1, -1))


def _combine(u, d, s0, *, batch, nt):
    u6 = u.reshape(batch, nt, 2, 2, 2 * DK, 2 * DV)
    d6 = d.reshape(batch, nt, 2, 2, 2 * DK, 128)
    out = pl.pallas_call(
        functools.partial(_combine_kernel, nt=nt),
        grid=(batch, 2, 2),
        in_specs=[
            pl.BlockSpec((None, nt, None, None, 2 * DK, 2 * DV), lambda b, r, p: (b, 0, r, p, 0, 0)),
            pl.BlockSpec((None, nt, None, None, 2 * DK, 128), lambda b, r, p: (b, 0, r, p, 0, 0)),
            pl.BlockSpec((None, None, None, 2 * DK, 2 * DV), lambda b, r, p: (b, r, p, 0, 0)),
        ],
        out_specs=pl.BlockSpec((None, nt, None, None, 2 * DK, 2 * DV),
                               lambda b, r, p: (b, 0, r, p, 0, 0)),
        out_shape=jax.ShapeDtypeStruct((batch, nt, 2, 2, 2 * DK, 2 * DV), F32),
        compiler_params=_params(3),
        name="combine",
    )(u6, d6, s0)
    return out.reshape(batch * nt, 2, 2, 2 * DK, 2 * DV)


def _gla_consts():
    c = GLA_C
    t = np.arange(c)
    sub = t // GLA_SUB
    lc = np.stack([t[None, :] <= t[:, None], t[None, :] >= t[:, None]]).astype(np.float32)
    tt, jj = t[:, None], t[None, :]
    sub_end = (sub * GLA_SUB + GLA_SUB - 1)[None, :]
    sub_start = (sub * GLA_SUB)[None, :]
    mk_f = (tt > jj) & (tt <= sub_end)
    mk_b = (tt < jj) & (tt >= sub_start)
    me_f = tt > jj
    me_b = tt < jj
    ones = np.ones((c, c), bool)
    mcat = np.stack([np.concatenate([mk_f, me_f, ones], axis=1),
                     np.concatenate([mk_b, me_b, ones], axis=1)]).astype(np.float32)
    return jnp.asarray(lc, BF), jnp.asarray(mcat, BF)


def _gla_kernel(h_ref, sin_ref, wq_ref, wkT_ref, wv_ref, wr_ref, wa_ref, waT_ref,
                wa2_ref, wa2T_ref, ba2_ref, ba2T_ref, go_ref, lc_ref, mcat_ref,
                o_ref, acc_ref, *, tm):
    c = GLA_C
    nchunk = tm // c
    h = h_ref[...]
    q = _dot(h, wq_ref[...]) * (DK ** -0.5)
    kT = _dot_nt(wkT_ref[...], h)
    v = _dot(h, wv_ref[...]).astype(BF)
    a = _dot(h, wa_ref[...]).astype(BF)
    aT = _dot_nt(waT_ref[...], h).astype(BF)

    row = lax.broadcasted_iota(jnp.int32, (c, 128), 0)
    col_sub = lax.broadcasted_iota(jnp.int32, (128, c), 1) // GLA_SUB
    ii = lax.broadcasted_iota(jnp.int32, (c, 2 * c), 0)
    jj = lax.broadcasted_iota(jnp.int32, (c, 2 * c), 1) % c
    bd = _bd_mask()
    zero_kd = jnp.zeros((DK, c), BF)

    for dr in range(2):
        scale = 1.0 / GLA_NORMALIZER
        g = _log_sigmoid(_dot(a, wa2_ref[dr]) + ba2_ref[dr]) * scale
        gT = _log_sigmoid(_dot(wa2T_ref[dr], aT) + ba2T_ref[dr]) * scale
        ghi, glo = _split_bf16(g)
        gThi, gTlo = _split_bf16(gT)
        tri = (jj <= ii) if dr == 0 else (jj >= ii)
        state = [sin_ref[dr, 0], sin_ref[dr, 1]]
        order = range(nchunk) if dr == 0 else range(nchunk - 1, -1, -1)
        for ci in order:
            rows = slice(ci * c, (ci + 1) * c)
            b = _dot(lc_ref[dr], ghi[rows]) + _dot(lc_ref[dr], glo[rows])
            fm = _dot(gThi[:, rows], mcat_ref[dr]) + _dot(gTlo[:, rows], mcat_ref[dr])
            kc = kT[:, rows]
            ksub = kc * jnp.exp(fm[:, 0:c])
            kend = (kc * jnp.exp(fm[:, c:2 * c])).astype(BF)
            tot = fm[:, 2 * c:3 * c]
            qc = q[rows]
            qb = (qc * jnp.exp(b)).astype(BF)
            for p in range(2):
                lanes = slice(p * 128, (p + 1) * 128)
                qp = qc[:, lanes]
                bp = b[:, lanes]
                qblocks = []
                kblocks = []
                kp = ksub[p * 128:(p + 1) * 128]
                for j in range(GLA_NSUB):
                    if dr == 0:
                        ref_row = j * GLA_SUB + GLA_SUB - 1
                        valid = row >= j * GLA_SUB
                    else:
                        ref_row = j * GLA_SUB
                        valid = row < (j + 1) * GLA_SUB
                    e = jnp.where(valid, bp - bp[ref_row:ref_row + 1, :], -1e30)
                    qblocks.append((qp * jnp.exp(e)).astype(BF))
                    kj = jnp.where(col_sub == j, kp, 0.0).astype(BF)
                    kblocks.append(jnp.concatenate([
                        jnp.concatenate([kj[:DK], zero_kd], axis=1),
                        jnp.concatenate([zero_kd, kj[DK:]], axis=1)], axis=0))
                qhat = jnp.concatenate(qblocks, axis=1)
                khat = jnp.concatenate(kblocks, axis=0)
                att = jnp.where(tri, _dot(qhat, khat), 0.0).astype(BF)
                vp = v[rows, p * 2 * DV:(p + 1) * 2 * DV]
                o_intra = jnp.concatenate([_dot(att[:, :c], vp[:, :DV]),
                                           _dot(att[:, c:], vp[:, DV:])], axis=1)
                o_inter = _dot(qb[:, lanes], state[p].astype(BF))
                o_p = o_inter + o_intra
                if dr == 0:
                    acc_ref[rows, p * 2 * DV:(p + 1) * 2 * DV] = o_p
                else:
                    acc_ref[rows, p * 2 * DV:(p + 1) * 2 * DV] += o_p
                upd = _dot(kend[p * 128:(p + 1) * 128], vp)
                dec = jnp.exp(tot[p * 128:(p + 1) * 128])
                state[p] = jnp.concatenate([dec, dec], axis=1) * state[p] + jnp.where(bd, upd, 0.0)

    r = _dot(h, wr_ref[...])
    for hd in range(HEADS):
        lanes = slice(hd * DV, (hd + 1) * DV)
        oh = acc_ref[:, lanes]
        on = oh * lax.rsqrt(jnp.mean(oh * oh, axis=-1, keepdims=True) + EPS) * go_ref[:, lanes]
        o_ref[:, lanes] = (on * _silu(r[:, lanes])).astype(BF)


def _gla(h, sin, w, *, tm):
    n = h.shape[0]
    lc, mcat = _gla_consts()
    hk = HEADS * DK
    in_specs = [
        pl.BlockSpec((tm, D), lambda i: (i, 0)),
        pl.BlockSpec((None, 2, 2, 2 * DK, 2 * DV), lambda i: (i, 0, 0, 0, 0)),
        _const_spec((D, hk)), _const_spec((hk, D)), _const_spec((D, BW)), _const_spec((D, BW)),
        _const_spec((D, 2 * RANK)), _const_spec((2 * RANK, D)),
        _const_spec((2, 2 * RANK, hk)), _const_spec((2, hk, 2 * RANK)),
        _const_spec((2, 1, hk)), _const_spec((2, hk, 1)), _const_spec((1, BW)),
        _const_spec((2, GLA_C, GLA_C)), _const_spec((2, GLA_C, 3 * GLA_C)),
    ]
    return pl.pallas_call(
        functools.partial(_gla_kernel, tm=tm),
        grid=(n // tm,),
        in_specs=in_specs,
        out_specs=pl.BlockSpec((tm, BW), lambda i: (i, 0)),
        out_shape=jax.ShapeDtypeStruct((n, BW), BF),
        scratch_shapes=[pltpu.VMEM((tm, BW), F32)],
        compiler_params=_params(1),
        name="gla",
    )(h, sin, w["q"], w["kT"], w["v"], w["r"], w["a"], w["aT"], w["a2"], w["a2T"],
      w["ba2"], w["ba2T"], w["go"], lc, mcat)


def _fft1_kernel(wr_ref, wi_ref, f_ref, tc_ref, ts_ref, zr_ref, zi_ref):
    x = jnp.concatenate([wr_ref[...], wi_ref[...]], axis=0)
    y = _dot(f_ref[...], x)
    yr, yi = y[:FFT_N1], y[FFT_N1:]
    for j in range(FFT_COLS // BW):
        cols = slice(j * BW, (j + 1) * BW)
        tc = jnp.concatenate([tc_ref[j]] * (BW // 128), axis=1)
        ts = jnp.concatenate([ts_ref[j]] * (BW // 128), axis=1)
        a, b = yr[:, cols], yi[:, cols]
        zr_ref[:, cols] = (a * tc + b * ts).astype(BF)
        zi_ref[:, cols] = (b * tc - a * ts).astype(BF)


def _fft2_kernel(zr_ref, zi_ref, f_ref, o_ref):
    z = jnp.concatenate([zr_ref[...], zi_ref[...]], axis=0)
    o_ref[...] = _dot(f_ref[...], z).astype(BF)


def _dft_cos_sin(n):
    k = np.arange(n)
    ang = 2.0 * np.pi * ((k[:, None] * k[None, :]) % n) / n
    return np.cos(ang), np.sin(ang)


def _fft_latent(wr, wi, batch, seq):
    n1, n2 = FFT_N1, FFT_N2
    ncol = n2 * BW
    c1, s1 = _dft_cos_sin(n1)
    f1 = np.block([[c1, s1], [-s1, c1]]) / math.sqrt(n1)
    k1 = np.arange(n1)[None, :, None]
    m2 = np.arange(n2)[:, None, None]
    ang = 2.0 * np.pi * (k1 * m2) / (n1 * n2)
    tc = np.broadcast_to(np.cos(ang), (n2, n1, 128)).astype(np.float32)
    ts = np.broadcast_to(np.sin(ang), (n2, n1, 128)).astype(np.float32)
    nj = FFT_COLS // BW
    blk = pl.BlockSpec((None, n1, FFT_COLS), lambda b, j: (b, 0, j))
    tw = pl.BlockSpec((nj, n1, 128), lambda b, j: (j, 0, 0))
    zr, zi = pl.pallas_call(
        _fft1_kernel,
        grid=(batch, ncol // FFT_COLS),
        in_specs=[blk, blk, _const_spec((2 * n1, 2 * n1)), tw, tw],
        out_specs=[blk, blk],
        out_shape=[jax.ShapeDtypeStruct((batch, n1, ncol), BF)] * 2,
        compiler_params=_params(2),
        name="fft1",
    )(wr.reshape(batch, n1, ncol), wi.reshape(batch, n1, ncol), _bf16_const(f1),
      jnp.asarray(tc), jnp.asarray(ts))
    c2, s2 = _dft_cos_sin(n2)
    f2 = np.concatenate([c2, s2], axis=1) / math.sqrt(n2 * GC)
    zblk = pl.BlockSpec((None, None, n2, BW), lambda b, k: (b, k, 0, 0))
    y = pl.pallas_call(
        _fft2_kernel,
        grid=(batch, n1),
        in_specs=[zblk, zblk, _const_spec((n2, 2 * n2))],
        out_specs=pl.BlockSpec((None, n2, BW), lambda b, k: (b, 0, k)),
        out_shape=jax.ShapeDtypeStruct((batch, n2, n1 * BW), BF),
        compiler_params=_params(2),
        name="fft2",
    )(zr.reshape(batch, n1, n2, BW), zi.reshape(batch, n1, n2, BW), _bf16_const(f2))
    return y.reshape(batch * seq, BW)


def _fft_direct(wr, wi, batch, seq):
    c2, s2 = _dft_cos_sin(seq)
    f2 = np.concatenate([c2, s2], axis=1) / math.sqrt(seq * GC)
    zblk = pl.BlockSpec((None, seq, BW), lambda b: (b, 0, 0))
    y = pl.pallas_call(
        _fft2_kernel,
        grid=(batch,),
        in_specs=[zblk, zblk, _const_spec((seq, 2 * seq))],
        out_specs=zblk,
        out_shape=jax.ShapeDtypeStruct((batch, seq, BW), BF),
        compiler_params=_params(1),
        name="fft_direct",
    )(wr.reshape(batch, seq, BW), wi.reshape(batch, seq, BW), _bf16_const(f2))
    return y.reshape(batch * seq, BW)


def _mix_kernel(x_ref, h_ref, og_ref, yf_ref, gt_ref, wsu_ref, wsv_ref, wcb_ref, wcc_ref, wcx_ref,
                wsgu_ref, bsgu_ref, wconv_ref, wbr_ref, wgate_ref, bgate_ref, wout_ref,
                o_ref, sgu_ref, *, tm, row_len):
    h = h_ref[...]

    def gated(k, branch):
        gate = _sigmoid(_dot(h, wgate_ref[k]) + bgate_ref[k])
        return gate * _dot(branch, wbr_ref[k])

    m = gated(0, og_ref[...])
    su = _dot(h, wsu_ref[...])
    sv = _dot(h, wsv_ref[...])
    for g in range(NGROUP):
        lanes = slice(g * GC, (g + 1) * GC)
        vg = sv[:, lanes]
        vc = vg - jnp.mean(vg, axis=-1, keepdims=True)
        z = (vc * lax.rsqrt(jnp.mean(vc * vc, axis=-1, keepdims=True) + EPS)).astype(BF)
        for n in range(tm // SGU_CHUNK):
            rows = slice(n * SGU_CHUNK, (n + 1) * SGU_CHUNK)
            s = _dot(wsgu_ref[g], z[rows]) + bsgu_ref[:, lanes]
            sgu_ref[rows, lanes] = (su[rows, lanes] * s).astype(BF)
    m = m + gated(1, sgu_ref[...])
    m = m + gated(2, yf_ref[...])
    zc = _dot(h, wcc_ref[...]) * _dot(h, wcx_ref[...])
    pos = lax.broadcasted_iota(jnp.int32, (tm, BW), 0) % row_len
    left = jnp.where(pos == 0, 0.0, pltpu.roll(zc, 1, 0))
    right = jnp.where(pos == row_len - 1, 0.0, pltpu.roll(zc, tm - 1, 0))
    y = left * wconv_ref[0:1, :] + zc * wconv_ref[1:2, :] + right * wconv_ref[2:3, :]
    conv = (_dot(h, wcb_ref[...]) * y).astype(BF)
    m = m + gated(3, conv)
    o_ref[...] = x_ref[...] + gt_ref[...] * _dot(m.astype(BF), wout_ref[...])


def _mix(x, h, og, yf, gate, w, *, tm, tiles_per_batch, per_batch, row_len):
    n = x.shape[0]
    ms = _mod_spec(tiles_per_batch, per_batch)
    tile = lambda width: pl.BlockSpec((tm, width), lambda i: (i, 0))
    in_specs = [
        tile(D), tile(D), tile(BW), tile(BW), ms,
        _const_spec((D, BW)), _const_spec((D, BW)), _const_spec((D, BW)), _const_spec((D, BW)),
        _const_spec((D, BW)),
        _const_spec((NGROUP, SGU_CHUNK, SGU_CHUNK)), _const_spec((SGU_CHUNK, BW)),
        _const_spec((3, BW)), _const_spec((4, BW, D)), _const_spec((4, D, D)),
        _const_spec((4, 1, D)), _const_spec((D, D)),
    ]
    return pl.pallas_call(
        functools.partial(_mix_kernel, tm=tm, row_len=row_len),
        grid=(n // tm,),
        in_specs=in_specs,
        out_specs=tile(D),
        out_shape=jax.ShapeDtypeStruct((n, D), F32),
        scratch_shapes=[pltpu.VMEM((tm, BW), BF)],
        compiler_params=_params(1),
        name="mix",
    )(x, h, og, yf, gate, w["su"], w["sv"], w["cb"], w["cc"], w["cx"], w["sgu"], w["bsgu"],
      w["conv"], w["branch"], w["gate"], w["bgate"], w["out"])


def _layer_weights(i, w_in, w_gla_a2, b_gla_a2, g_gla_norm, w_sgu, b_sgu, w_conv, w_branch,
                   w_gate, b_gate, w_out):
    hk = HEADS * DK
    wi = w_in[i].astype(BF)
    edges = np.cumsum([0, hk, hk, BW, BW, 2 * RANK, BW, BW, BW, BW, BW, BW])
    q, k, v, r, a, su, sv, fn, cb, cc, cx = [wi[:, edges[j]:edges[j + 1]] for j in range(11)]
    a2 = w_gla_a2[i]
    zeros = jnp.zeros_like(a2[0])
    a2p = jnp.stack([jnp.concatenate([a2[0], zeros], axis=0),
                     jnp.concatenate([zeros, a2[1]], axis=0)]).astype(BF)
    cc_, sc_ = _dft_cos_sin(GC)
    return {
        "q": q, "kT": k.T, "v": v, "r": r, "a": a, "aT": a.T,
        "a2": a2p, "a2T": jnp.transpose(a2p, (0, 2, 1)),
        "ba2": b_gla_a2[i][:, None, :], "ba2T": b_gla_a2[i][:, :, None],
        "go": g_gla_norm[i].reshape(1, BW),
        "su": su, "sv": sv, "fn": fn, "cb": cb, "cc": cc, "cx": cx,
        "ccs": _bf16_const(np.concatenate([cc_, -sc_], axis=1)),
        "sgu": w_sgu[i].astype(BF),
        "bsgu": jnp.repeat(jnp.transpose(b_sgu[i]), GC, axis=1),
        "conv": w_conv[i],
        "branch": w_branch[i].astype(BF), "gate": w_gate[i].astype(BF),
        "bgate": b_gate[i][:, None, :], "out": w_out[i].astype(BF),
    }


def kernel(x, c, ctx, c_ctx, w_ada, b_ada, g_norm, w_ff1, w_ff3, w_ff2, w_in, w_gla_a2, b_gla_a2,
           g_gla_norm, w_sgu, b_sgu, w_conv, w_branch, w_gate, b_gate, w_out, g_final):
    batch, seq, _ = x.shape
    ctx_len = ctx.shape[1]
    assert seq % TM_LAT == 0 and TM_LAT % GRID_W == 0 and seq == FFT_N1 * FFT_N2
    assert ctx_len % GLA_C == 0 and batch == 2
    tpb = seq // TM_LAT

    cvec = jnp.concatenate([c, c_ctx[None, :]], axis=0)
    cb = jnp.broadcast_to(cvec[:, :, None], (3, D, 128))
    mods = _adaln(cb, w_ada, b_ada).reshape(DEPTH, 3, N_MOD, 1, D)

    xs = x.reshape(batch * seq, D)
    cs = ctx.reshape(batch * ctx_len, D)
    lat = dict(tm=TM_LAT, tiles_per_batch=tpb, per_batch=True)
    cx_ = dict(tm=ctx_len, tiles_per_batch=1, per_batch=False)

    for i in range(DEPTH):
        last = i == DEPTH - 1
        ml = [mods[i, 0:2, j] for j in range(N_MOD)]
        mc = [mods[i, 2:3, j] for j in range(N_MOD)]
        gn = g_norm[i][:, None, :]
        ff = [(w_ff1[i, j].astype(BF), w_ff3[i, j].astype(BF), w_ff2[i, j].astype(BF))
              for j in range(2)]
        w = _layer_weights(i, w_in, w_gla_a2, b_gla_a2, g_gla_norm, w_sgu, b_sgu, w_conv,
                           w_branch, w_gate, b_gate, w_out)

        xs = _ffn(xs, gn[0], ml[0], ml[1], ml[2], *ff[0], **lat)
        cs = _ffn(cs, gn[0], mc[0], mc[1], mc[2], *ff[0], **cx_)

        hc, wrc, wic, uc, _ = _prep(cs, gn[1], mc[3], mc[4], w, **cx_)
        s0 = uc.reshape(batch, 2, 2, 2 * DK, 2 * DV)

        hl, wrl, wil, ul, dl = _prep(xs, gn[1], ml[3], ml[4], w, **lat)
        sin = _combine(ul, dl, s0, batch=batch, nt=tpb)
        og = _gla(hl, sin, w, tm=TM_LAT)
        yf = _fft_latent(wrl, wil, batch, seq)
        xs = _mix(xs, hl, og, yf, ml[5], w, row_len=GRID_W, **lat)
        xs = _ffn(xs, gn[2], ml[6], ml[7], ml[8], *ff[1], g_final=g_final[None, :] if last else None,
                  **lat)

        if not last:
            ogc = _gla(hc, jnp.zeros_like(uc), w, tm=ctx_len)
            yfc = _fft_direct(wrc, wic, batch, ctx_len)
            cs = _mix(cs, hc, ogc, yfc, mc[5], w, row_len=ctx_len, **cx_)
            cs = _ffn(cs, gn[2], mc[6], mc[7], mc[8], *ff[1], **cx_)
    return xs.reshape(batch, seq, D)
```

```python
import functools
import math

import numpy as np
import jax
import jax.numpy as jnp
from jax import lax
from jax.experimental import pallas as pl
from jax.experimental.pallas import tpu as pltpu

D = 1024
DEPTH = 2
GRID_W = 64
N_MOD = 9
D_FF = 2816
BW = 512
HEADS = 4
DV = 128
DK = 64
HK = HEADS * DK
RANK = 16
GLA_NORMALIZER = 16.0
SGU_CHUNK = 128
GC = 128
NGROUP = 4
EPS = 1e-6
LANES = 128

TM_LAT = 512
FF_CHUNK = 256
GLA_C = 128
GLA_SUB = 32
GLA_NSUB = GLA_C // GLA_SUB
FFT_N1 = 64
FFT_N2 = 128
FFT_COLS = 4096
ADA_TN = 1152
VMEM_LIMIT = 56 * 1024 * 1024

BF = jnp.bfloat16
F32 = jnp.float32

_NT = (((1,), (1,)), ((), ()))


def _dot(a, b):
    return jnp.dot(a, b, preferred_element_type=F32)


def _dot_nt(a, b):
    return lax.dot_general(a, b, _NT, preferred_element_type=F32)


def _sigmoid(x):
    return 1.0 / (1.0 + jnp.exp(-x))


def _silu(x):
    return x * _sigmoid(x)


def _log_sigmoid(x):
    return jnp.minimum(x, 0.0) - jnp.log(1.0 + jnp.exp(-jnp.abs(x)))


def _bf16_const(a):
    return jnp.asarray(a, F32).astype(BF)


def _const_spec(shape):
    nd = len(shape)
    return pl.BlockSpec(shape, lambda *_: (0,) * nd, pipeline_mode=pl.Buffered(1))


def _params(n_grid):
    return pltpu.CompilerParams(dimension_semantics=("arbitrary",) * n_grid,
                                vmem_limit_bytes=VMEM_LIMIT)


def _norm_mod(x, g, shift, scale):
    hn = x * lax.rsqrt(jnp.mean(x * x, axis=-1, keepdims=True) + EPS) * g
    return hn * (1.0 + scale) + shift


def _adaln_kernel(cb_ref, w_ref, b_ref, o_ref):
    for r in range(3):
        cv = cb_ref[r]
        s = _silu(cv)
        for j in range(ADA_TN // LANES):
            w = w_ref[:, j * LANES:(j + 1) * LANES]
            o_ref[r:r + 1, j * LANES:(j + 1) * LANES] = (
                jnp.sum(w * s, axis=0, keepdims=True) + b_ref[:, j * LANES:(j + 1) * LANES])


def _adaln(cb, w_ada, b_ada):
    nmod = N_MOD * D
    return pl.pallas_call(
        _adaln_kernel,
        grid=(DEPTH, nmod // ADA_TN),
        in_specs=[
            pl.BlockSpec((3, D, LANES), lambda l, j: (0, 0, 0)),
            pl.BlockSpec((None, D, ADA_TN), lambda l, j: (l, 0, j)),
            pl.BlockSpec((None, 1, ADA_TN), lambda l, j: (l, 0, j)),
        ],
        out_specs=pl.BlockSpec((None, 3, ADA_TN), lambda l, j: (l, 0, j)),
        out_shape=jax.ShapeDtypeStruct((DEPTH, 3, nmod), F32),
        compiler_params=_params(2),
        name="adaln",
    )(cb, w_ada, b_ada.reshape(DEPTH, 1, nmod))


def _ffn_kernel(*refs, final):
    if final:
        s_ref, g_ref, sh_ref, sc_ref, gt_ref, w1_ref, w3_ref, w2_ref, gf_ref, o_ref = refs
    else:
        s_ref, g_ref, sh_ref, sc_ref, gt_ref, w1_ref, w3_ref, w2_ref, o_ref = refs
    s = s_ref[...]
    h = _norm_mod(s, g_ref[...], sh_ref[...], sc_ref[...]).astype(BF)
    acc = jnp.zeros(s.shape, F32)
    for j in range(D_FF // FF_CHUNK):
        cols = slice(j * FF_CHUNK, (j + 1) * FF_CHUNK)
        a = _dot(h, w1_ref[:, cols])
        b = _dot(h, w3_ref[:, cols])
        u = (_silu(a) * b).astype(BF)
        acc = acc + _dot(u, w2_ref[cols, :])
    out = s + 0.5 * gt_ref[...] * acc
    if final:
        out = out * lax.rsqrt(jnp.mean(out * out, axis=-1, keepdims=True) + EPS) * gf_ref[...]
    o_ref[...] = out


def _mod_spec(tiles_per_batch, per_batch):
    if per_batch:
        return pl.BlockSpec((None, 1, D), lambda i: (i // tiles_per_batch, 0, 0))
    return pl.BlockSpec((None, 1, D), lambda i: (0, 0, 0))


def _ffn(s, g, shift, scale, gate, w1, w3, w2, *, tm, tiles_per_batch, per_batch, g_final=None):
    n = s.shape[0]
    final = g_final is not None
    ms = _mod_spec(tiles_per_batch, per_batch)
    in_specs = [
        pl.BlockSpec((tm, D), lambda i: (i, 0)),
        _const_spec((1, D)), ms, ms, ms,
        _const_spec((D, D_FF)), _const_spec((D, D_FF)), _const_spec((D_FF, D)),
    ]
    args = [s, g, shift, scale, gate, w1, w3, w2]
    if final:
        in_specs.append(_const_spec((1, D)))
        args.append(g_final)
    return pl.pallas_call(
        functools.partial(_ffn_kernel, final=final),
        grid=(n // tm,),
        in_specs=in_specs,
        out_specs=pl.BlockSpec((tm, D), lambda i: (i, 0)),
        out_shape=jax.ShapeDtypeStruct((n, D), F32),
        compiler_params=_params(1),
        name="ffn_final" if final else "ffn",
    )(*args)


def _prep_kernel(x_ref, g_ref, sh_ref, sc_ref, wfn_ref, ccs_ref, wq_ref, wkT_ref, wv_ref, wa_ref,
                 waT_ref, h_ref, wr_ref, wi_ref, q_ref, kT_ref, v_ref, a_ref, aT_ref):
    h = _norm_mod(x_ref[...], g_ref[...], sh_ref[...], sc_ref[...]).astype(BF)
    h_ref[...] = h
    fn = _dot(h, wfn_ref[...]).astype(BF)
    for g in range(NGROUP):
        pq = _dot(fn[:, g * GC:(g + 1) * GC], ccs_ref[...])
        wr_ref[:, g * GC:(g + 1) * GC] = pq[:, :GC].astype(BF)
        wi_ref[:, g * GC:(g + 1) * GC] = pq[:, GC:].astype(BF)
    q_ref[...] = (_dot(h, wq_ref[...]) * (DK ** -0.5)).astype(BF)
    kT_ref[...] = _dot_nt(wkT_ref[...], h).astype(BF)
    v_ref[...] = _dot(h, wv_ref[...]).astype(BF)
    a_ref[...] = _dot(h, wa_ref[...]).astype(BF)
    aT_ref[...] = _dot_nt(waT_ref[...], h).astype(BF)


def _prep(x, g, shift, scale, w, *, tm, tiles_per_batch, per_batch):
    n = x.shape[0]
    ms = _mod_spec(tiles_per_batch, per_batch)
    rows = lambda width: pl.BlockSpec((tm, width), lambda i: (i, 0))
    cols = lambda height: pl.BlockSpec((height, tm), lambda i: (0, i))
    in_specs = [
        rows(D), _const_spec((1, D)), ms, ms,
        _const_spec((D, BW)), _const_spec((GC, 2 * GC)), _const_spec((D, HK)), _const_spec((HK, D)),
        _const_spec((D, BW)), _const_spec((D, LANES)), _const_spec((2 * RANK, D)),
    ]
    out_specs = [rows(D), rows(BW), rows(BW), rows(HK), cols(HK), rows(BW), rows(LANES),
                 cols(2 * RANK)]
    out_shape = [
        jax.ShapeDtypeStruct((n, D), BF), jax.ShapeDtypeStruct((n, BW), BF),
        jax.ShapeDtypeStruct((n, BW), BF), jax.ShapeDtypeStruct((n, HK), BF),
        jax.ShapeDtypeStruct((HK, n), BF), jax.ShapeDtypeStruct((n, BW), BF),
        jax.ShapeDtypeStruct((n, LANES), BF), jax.ShapeDtypeStruct((2 * RANK, n), BF),
    ]
    return pl.pallas_call(
        _prep_kernel,
        grid=(n // tm,),
        in_specs=in_specs,
        out_specs=out_specs,
        out_shape=out_shape,
        compiler_params=_params(1),
        name="prep",
    )(x, g, shift, scale, w["fn"], w["ccs"], w["q"], w["kT"], w["v"], w["a"], w["aT"])


def _bd_mask():
    r = lax.broadcasted_iota(jnp.int32, (2 * DK, 2 * DV), 0) // DK
    c = lax.broadcasted_iota(jnp.int32, (2 * DK, 2 * DV), 1) // DV
    return r == c


def _gla_consts():
    c = GLA_C
    t = np.arange(c)
    sub = t // GLA_SUB
    lc = np.stack([t[None, :] <= t[:, None], t[None, :] >= t[:, None]]).astype(np.float32)
    tt, jj = t[:, None], t[None, :]
    sub_end = (sub * GLA_SUB + GLA_SUB - 1)[None, :]
    sub_start = (sub * GLA_SUB)[None, :]
    mk_f = (tt > jj) & (tt <= sub_end)
    mk_b = (tt < jj) & (tt >= sub_start)
    me_f = tt > jj
    me_b = tt < jj
    ones = np.ones((c, c), bool)
    mcat = np.stack([np.concatenate([mk_f, me_f, ones], axis=1),
                     np.concatenate([mk_b, me_b, ones], axis=1)]).astype(np.float32)
    return jnp.asarray(lc, BF), jnp.asarray(mcat, BF)


def _gla_direction(dr, q_ref, kT_ref, v_ref, a_ref, aT_ref, wa2_ref, wa2T_ref, ba2_ref, ba2T_ref,
                   lc_ref, mcat_ref, state_ref, o_ref, nchunk):
    c = GLA_C
    inv = 1.0 / GLA_NORMALIZER
    g = (_log_sigmoid(_dot(a_ref[...], wa2_ref[dr]) + ba2_ref[dr]) * inv).astype(BF)
    gT = (_log_sigmoid(_dot(wa2T_ref[dr], aT_ref[...]) + ba2T_ref[dr]) * inv).astype(BF)
    col_sub = lax.broadcasted_iota(jnp.int32, (2 * DK, c), 1) // GLA_SUB
    ii = lax.broadcasted_iota(jnp.int32, (c, 2 * c), 0)
    jj = lax.broadcasted_iota(jnp.int32, (c, 2 * c), 1) % c
    tri = (jj <= ii) if dr == 0 else (jj >= ii)
    bd = _bd_mask()
    zero_kd = jnp.zeros((DK, c), BF)
    zero_v = jnp.zeros((c, DV), BF)
    state = [state_ref[dr, 0], state_ref[dr, 1]]
    order = range(nchunk) if dr == 0 else range(nchunk - 1, -1, -1)
    for ci in order:
        rows = slice(ci * c, (ci + 1) * c)
        b = _dot(lc_ref[dr], g[rows])
        fm = _dot(gT[:, rows], mcat_ref[dr])
        kc = kT_ref[:, rows].astype(F32)
        ksub = (kc * jnp.exp(fm[:, 0:c])).astype(BF)
        kend = (kc * jnp.exp(fm[:, c:2 * c])).astype(BF)
        dec = jnp.exp(fm[:, 2 * c:3 * c])
        qc = q_ref[rows, :].astype(F32)
        qb = (qc * jnp.exp(b)).astype(BF)
        for p in range(2):
            lanes = slice(p * LANES, (p + 1) * LANES)
            qp = qc[:, lanes]
            bp = b[:, lanes]
            kp = ksub[lanes]
            qblocks = []
            kblocks = []
            for j in range(GLA_NSUB):
                if dr == 0:
                    ref_row, lo, hi = j * GLA_SUB + GLA_SUB - 1, j * GLA_SUB, c
                else:
                    ref_row, lo, hi = j * GLA_SUB, 0, (j + 1) * GLA_SUB
                live = (qp[lo:hi] * jnp.exp(bp[lo:hi] - bp[ref_row:ref_row + 1, :])).astype(BF)
                parts = []
                if lo > 0:
                    parts.append(jnp.zeros((lo, LANES), BF))
                parts.append(live)
                if hi < c:
                    parts.append(jnp.zeros((c - hi, LANES), BF))
                qblocks.append(jnp.concatenate(parts, axis=0) if len(parts) > 1 else live)
                kj = jnp.where(col_sub == j, kp, jnp.zeros_like(kp))
                kblocks.append(jnp.concatenate([
                    jnp.concatenate([kj[:DK], zero_kd], axis=1),
                    jnp.concatenate([zero_kd, kj[DK:]], axis=1)], axis=0))
            qhat = jnp.concatenate(qblocks, axis=1)
            khat = jnp.concatenate(kblocks, axis=0)
            att = jnp.where(tri, _dot(qhat, khat), 0.0).astype(BF)
            vp = v_ref[rows, p * 2 * DV:(p + 1) * 2 * DV]
            vbd = jnp.concatenate([
                jnp.concatenate([vp[:, :DV], zero_v], axis=1),
                jnp.concatenate([zero_v, vp[:, DV:]], axis=1)], axis=0)
            o_p = _dot(att, vbd) + _dot(qb[:, lanes], state[p].astype(BF))
            o_ref[rows, p * 2 * DV:(p + 1) * 2 * DV] = o_p.astype(BF)
            upd = _dot(kend[lanes], vp)
            dp = dec[lanes]
            state[p] = jnp.concatenate([dp, dp], axis=1) * state[p] + jnp.where(bd, upd, 0.0)
    state_ref[dr, 0] = state[0]
    state_ref[dr, 1] = state[1]


def _gla_kernel(qf_ref, kTf_ref, vf_ref, af_ref, aTf_ref, qb_ref, kTb_ref, vb_ref, ab_ref, aTb_ref,
                s0_ref, wa2_ref, wa2T_ref, ba2_ref, ba2T_ref, lc_ref, mcat_ref,
                of_ref, ob_ref, sfin_ref, state_ref, *, tm, tiles_per_batch):
    @pl.when(pl.program_id(0) % tiles_per_batch == 0)
    def _():
        state_ref[...] = s0_ref[...]

    shared = (wa2_ref, wa2T_ref, ba2_ref, ba2T_ref, lc_ref, mcat_ref, state_ref)
    _gla_direction(0, qf_ref, kTf_ref, vf_ref, af_ref, aTf_ref, *shared, of_ref, tm // GLA_C)
    _gla_direction(1, qb_ref, kTb_ref, vb_ref, ab_ref, aTb_ref, *shared, ob_ref, tm // GLA_C)
    sfin_ref[...] = state_ref[...]


def _gla(q, kT, v, a, aT, s0, w, *, tm, tiles_per_batch):
    n = q.shape[0]
    nt = n // tm
    batch = nt // tiles_per_batch
    lc, mcat = _gla_consts()
    tpb = tiles_per_batch
    fwd = lambda i: i
    bwd = lambda i: (i // tpb) * tpb + (tpb - 1 - i % tpb)

    def tile_specs(idx):
        return [
            pl.BlockSpec((tm, HK), lambda i: (idx(i), 0)),
            pl.BlockSpec((HK, tm), lambda i: (0, idx(i))),
            pl.BlockSpec((tm, BW), lambda i: (idx(i), 0)),
            pl.BlockSpec((tm, LANES), lambda i: (idx(i), 0)),
            pl.BlockSpec((2 * RANK, tm), lambda i: (0, idx(i))),
        ]

    state_block = (None, 2, 2, 2 * DK, 2 * DV)
    in_specs = tile_specs(fwd) + tile_specs(bwd) + [
        pl.BlockSpec(state_block, lambda i: (i // tpb, 0, 0, 0, 0)),
        _const_spec((2, LANES, HK)), _const_spec((2, HK, 2 * RANK)),
        _const_spec((2, 1, HK)), _const_spec((2, HK, 1)),
        _const_spec((2, GLA_C, GLA_C)), _const_spec((2, GLA_C, 3 * GLA_C)),
    ]
    out_specs = [
        pl.BlockSpec((tm, BW), lambda i: (fwd(i), 0)),
        pl.BlockSpec((tm, BW), lambda i: (bwd(i), 0)),
        pl.BlockSpec(state_block, lambda i: (i // tpb, 0, 0, 0, 0)),
    ]
    out_shape = [
        jax.ShapeDtypeStruct((n, BW), BF), jax.ShapeDtypeStruct((n, BW), BF),
        jax.ShapeDtypeStruct((batch, 2, 2, 2 * DK, 2 * DV), F32),
    ]
    tiles = (q, kT, v, a, aT)
    return pl.pallas_call(
        functools.partial(_gla_kernel, tm=tm, tiles_per_batch=tpb),
        grid=(nt,),
        in_specs=in_specs,
        out_specs=out_specs,
        out_shape=out_shape,
        scratch_shapes=[pltpu.VMEM((2, 2, 2 * DK, 2 * DV), F32)],
        compiler_params=_params(1),
        name="gla",
    )(*tiles, *tiles, s0, w["a2"], w["a2T"], w["ba2"], w["ba2T"], lc, mcat)


def _fft1_kernel(wr_ref, wi_ref, f_ref, tc_ref, ts_ref, zr_ref, zi_ref):
    x = jnp.concatenate([wr_ref[...], wi_ref[...]], axis=0)
    y = _dot(f_ref[...], x)
    yr, yi = y[:FFT_N1], y[FFT_N1:]
    for j in range(FFT_COLS // BW):
        cols = slice(j * BW, (j + 1) * BW)
        tc = jnp.concatenate([tc_ref[j]] * (BW // LANES), axis=1)
        ts = jnp.concatenate([ts_ref[j]] * (BW // LANES), axis=1)
        a, b = yr[:, cols], yi[:, cols]
        zr_ref[:, cols] = (a * tc + b * ts).astype(BF)
        zi_ref[:, cols] = (b * tc - a * ts).astype(BF)


def _fft2_kernel(zr_ref, zi_ref, f_ref, o_ref):
    z = jnp.concatenate([zr_ref[...], zi_ref[...]], axis=0)
    o_ref[...] = _dot(f_ref[...], z).astype(BF)


def _dft_cos_sin(n):
    k = np.arange(n)
    ang = 2.0 * np.pi * ((k[:, None] * k[None, :]) % n) / n
    return np.cos(ang), np.sin(ang)


def _fft_latent(wr, wi, batch, seq):
    n1, n2 = FFT_N1, FFT_N2
    ncol = n2 * BW
    c1, s1 = _dft_cos_sin(n1)
    f1 = np.block([[c1, s1], [-s1, c1]]) / math.sqrt(n1)
    k1 = np.arange(n1)[None, :, None]
    m2 = np.arange(n2)[:, None, None]
    ang = 2.0 * np.pi * (k1 * m2) / (n1 * n2)
    tc = np.broadcast_to(np.cos(ang), (n2, n1, LANES)).astype(np.float32)
    ts = np.broadcast_to(np.sin(ang), (n2, n1, LANES)).astype(np.float32)
    nj = FFT_COLS // BW
    blk = pl.BlockSpec((None, n1, FFT_COLS), lambda b, j: (b, 0, j))
    tw = pl.BlockSpec((nj, n1, LANES), lambda b, j: (j, 0, 0))
    zr, zi = pl.pallas_call(
        _fft1_kernel,
        grid=(batch, ncol // FFT_COLS),
        in_specs=[blk, blk, _const_spec((2 * n1, 2 * n1)), tw, tw],
        out_specs=[blk, blk],
        out_shape=[jax.ShapeDtypeStruct((batch, n1, ncol), BF)] * 2,
        compiler_params=_params(2),
        name="fft1",
    )(wr.reshape(batch, n1, ncol), wi.reshape(batch, n1, ncol), _bf16_const(f1),
      jnp.asarray(tc), jnp.asarray(ts))
    c2, s2 = _dft_cos_sin(n2)
    f2 = np.concatenate([c2, s2], axis=1) / math.sqrt(n2 * GC)
    zblk = pl.BlockSpec((None, None, n2, BW), lambda b, k: (b, k, 0, 0))
    y = pl.pallas_call(
        _fft2_kernel,
        grid=(batch, n1),
        in_specs=[zblk, zblk, _const_spec((n2, 2 * n2))],
        out_specs=pl.BlockSpec((None, n2, BW), lambda b, k: (b, 0, k)),
        out_shape=jax.ShapeDtypeStruct((batch, n2, n1 * BW), BF),
        compiler_params=_params(2),
        name="fft2",
    )(zr.reshape(batch, n1, n2, BW), zi.reshape(batch, n1, n2, BW), _bf16_const(f2))
    return y.reshape(batch * seq, BW)


def _fft_direct(wr, wi, batch, seq):
    c2, s2 = _dft_cos_sin(seq)
    f2 = np.concatenate([c2, s2], axis=1) / math.sqrt(seq * GC)
    zblk = pl.BlockSpec((None, seq, BW), lambda b: (b, 0, 0))
    y = pl.pallas_call(
        _fft2_kernel,
        grid=(batch,),
        in_specs=[zblk, zblk, _const_spec((seq, 2 * seq))],
        out_specs=zblk,
        out_shape=jax.ShapeDtypeStruct((batch, seq, BW), BF),
        compiler_params=_params(1),
        name="fft_direct",
    )(wr.reshape(batch, seq, BW), wi.reshape(batch, seq, BW), _bf16_const(f2))
    return y.reshape(batch * seq, BW)


def _mix_kernel(x_ref, h_ref, of_ref, ob_ref, yf_ref, gt_ref, wr_ref, go_ref, wsu_ref, wsv_ref,
                wcb_ref, wcc_ref, wcx_ref, wsgu_ref, bsgu_ref, wconv_ref, wbr_ref, wgate_ref,
                bgate_ref, wout_ref, o_ref, br_ref, *, tm, row_len):
    h = h_ref[...]

    def gated(k, branch):
        gate = _sigmoid(_dot(h, wgate_ref[k]) + bgate_ref[k])
        return gate * _dot(branch, wbr_ref[k])

    r = _dot(h, wr_ref[...])
    for hd in range(HEADS):
        lanes = slice(hd * DV, (hd + 1) * DV)
        oh = of_ref[:, lanes].astype(F32) + ob_ref[:, lanes].astype(F32)
        on = oh * lax.rsqrt(jnp.mean(oh * oh, axis=-1, keepdims=True) + EPS) * go_ref[:, lanes]
        br_ref[:, lanes] = (on * _silu(r[:, lanes])).astype(BF)
    m = gated(0, br_ref[...])
    su = _dot(h, wsu_ref[...])
    sv = _dot(h, wsv_ref[...])
    for g in range(NGROUP):
        lanes = slice(g * GC, (g + 1) * GC)
        vg = sv[:, lanes]
        vc = vg - jnp.mean(vg, axis=-1, keepdims=True)
        z = (vc * lax.rsqrt(jnp.mean(vc * vc, axis=-1, keepdims=True) + EPS)).astype(BF)
        for n in range(tm // SGU_CHUNK):
            rows = slice(n * SGU_CHUNK, (n + 1) * SGU_CHUNK)
            s = _dot(wsgu_ref[g], z[rows]) + bsgu_ref[:, lanes]
            br_ref[rows, lanes] = (su[rows, lanes] * s).astype(BF)
    m = m + gated(1, br_ref[...])
    m = m + gated(2, yf_ref[...])
    zc = _dot(h, wcc_ref[...]) * _dot(h, wcx_ref[...])
    pos = lax.broadcasted_iota(jnp.int32, (tm, BW), 0) % row_len
    left = jnp.where(pos == 0, 0.0, pltpu.roll(zc, 1, 0))
    right = jnp.where(pos == row_len - 1, 0.0, pltpu.roll(zc, tm - 1, 0))
    y = left * wconv_ref[0:1, :] + zc * wconv_ref[1:2, :] + right * wconv_ref[2:3, :]
    conv = (_dot(h, wcb_ref[...]) * y).astype(BF)
    m = m + gated(3, conv)
    o_ref[...] = x_ref[...] + gt_ref[...] * _dot(m.astype(BF), wout_ref[...])


def _mix(x, h, of, ob, yf, gate, w, *, tm, tiles_per_batch, per_batch, row_len):
    n = x.shape[0]
    ms = _mod_spec(tiles_per_batch, per_batch)
    tile = lambda width: pl.BlockSpec((tm, width), lambda i: (i, 0))
    in_specs = [
        tile(D), tile(D), tile(BW), tile(BW), tile(BW), ms,
        _const_spec((D, BW)), _const_spec((1, BW)),
        _const_spec((D, BW)), _const_spec((D, BW)), _const_spec((D, BW)), _const_spec((D, BW)),
        _const_spec((D, BW)),
        _const_spec((NGROUP, SGU_CHUNK, SGU_CHUNK)), _const_spec((SGU_CHUNK, BW)),
        _const_spec((3, BW)), _const_spec((4, BW, D)), _const_spec((4, D, D)),
        _const_spec((4, 1, D)), _const_spec((D, D)),
    ]
    return pl.pallas_call(
        functools.partial(_mix_kernel, tm=tm, row_len=row_len),
        grid=(n // tm,),
        in_specs=in_specs,
        out_specs=tile(D),
        out_shape=jax.ShapeDtypeStruct((n, D), F32),
        scratch_shapes=[pltpu.VMEM((tm, BW), BF)],
        compiler_params=_params(1),
        name="mix",
    )(x, h, of, ob, yf, gate, w["r"], w["go"], w["su"], w["sv"], w["cb"], w["cc"], w["cx"],
      w["sgu"], w["bsgu"], w["conv"], w["branch"], w["gate"], w["bgate"], w["out"])


def _layer_weights(i, w_in, w_gla_a2, b_gla_a2, g_gla_norm, w_sgu, b_sgu, w_conv, w_branch,
                   w_gate, b_gate, w_out):
    wi = w_in[i].astype(BF)
    edges = np.cumsum([0, HK, HK, BW, BW, 2 * RANK, BW, BW, BW, BW, BW, BW])
    q, k, v, r, a, su, sv, fn, cb, cc, cx = [wi[:, edges[j]:edges[j + 1]] for j in range(11)]
    a2 = w_gla_a2[i].astype(BF)
    a2_tok = jnp.zeros((2, LANES, HK), BF)
    a2_tok = a2_tok.at[0, :RANK].set(a2[0]).at[1, RANK:2 * RANK].set(a2[1])
    cc_, sc_ = _dft_cos_sin(GC)
    return {
        "q": q, "kT": k.T, "v": v, "r": r,
        "a": jnp.pad(a, ((0, 0), (0, LANES - 2 * RANK))), "aT": a.T,
        "a2": a2_tok, "a2T": jnp.transpose(a2_tok[:, :2 * RANK, :], (0, 2, 1)),
        "ba2": b_gla_a2[i][:, None, :], "ba2T": b_gla_a2[i][:, :, None],
        "go": g_gla_norm[i].reshape(1, BW),
        "su": su, "sv": sv, "fn": fn, "cb": cb, "cc": cc, "cx": cx,
        "ccs": _bf16_const(np.concatenate([cc_, -sc_], axis=1)),
        "sgu": w_sgu[i].astype(BF),
        "bsgu": jnp.repeat(jnp.transpose(b_sgu[i]), GC, axis=1),
        "conv": w_conv[i],
        "branch": w_branch[i].astype(BF), "gate": w_gate[i].astype(BF),
        "bgate": b_gate[i][:, None, :], "out": w_out[i].astype(BF),
    }


def kernel(x, c, ctx, c_ctx, w_ada, b_ada, g_norm, w_ff1, w_ff3, w_ff2, w_in, w_gla_a2, b_gla_a2,
           g_gla_norm, w_sgu, b_sgu, w_conv, w_branch, w_gate, b_gate, w_out, g_final):
    batch, seq, _ = x.shape
    ctx_len = ctx.shape[1]
    assert seq % TM_LAT == 0 and TM_LAT % GRID_W == 0 and seq == FFT_N1 * FFT_N2
    assert ctx_len % GLA_C == 0
    tpb = seq // TM_LAT

    cvec = jnp.concatenate([c, c_ctx[None, :]], axis=0)
    cb = jnp.broadcast_to(cvec[:, :, None], (3, D, LANES))
    mods = _adaln(cb, w_ada, b_ada).reshape(DEPTH, 3, N_MOD, 1, D)

    xs = x.reshape(batch * seq, D)
    cs = ctx.reshape(batch * ctx_len, D)
    lat = dict(tm=TM_LAT, tiles_per_batch=tpb, per_batch=True)
    cx_ = dict(tm=ctx_len, tiles_per_batch=1, per_batch=False)
    s_zero = jnp.zeros((batch, 2, 2, 2 * DK, 2 * DV), F32)

    for i in range(DEPTH):
        last = i == DEPTH - 1
        ml = [mods[i, 0:2, j] for j in range(N_MOD)]
        mc = [mods[i, 2:3, j] for j in range(N_MOD)]
        gn = g_norm[i][:, None, :]
        ff = [(w_ff1[i, j].astype(BF), w_ff3[i, j].astype(BF), w_ff2[i, j].astype(BF))
              for j in range(2)]
        w = _layer_weights(i, w_in, w_gla_a2, b_gla_a2, g_gla_norm, w_sgu, b_sgu, w_conv,
                           w_branch, w_gate, b_gate, w_out)

        xs = _ffn(xs, gn[0], ml[0], ml[1], ml[2], *ff[0], **lat)
        cs = _ffn(cs, gn[0], mc[0], mc[1], mc[2], *ff[0], **cx_)

        hc, wrc, wic, *gla_c = _prep(cs, gn[1], mc[3], mc[4], w, **cx_)
        ofc, obc, s_ctx = _gla(*gla_c, s_zero, w, tm=ctx_len, tiles_per_batch=1)

        hl, wrl, wil, *gla_l = _prep(xs, gn[1], ml[3], ml[4], w, **lat)
        ofl, obl, _ = _gla(*gla_l, s_ctx, w, tm=TM_LAT, tiles_per_batch=tpb)
        yf = _fft_latent(wrl, wil, batch, seq)
        xs = _mix(xs, hl, ofl, obl, yf, ml[5], w, row_len=GRID_W, **lat)
        xs = _ffn(xs, gn[2], ml[6], ml[7], ml[8], *ff[1], g_final=g_final[None, :] if last else None,
                  **lat)

        if not last:
            yfc = _fft_direct(wrc, wic, batch, ctx_len)
            cs = _mix(cs, hc, ofc, obc, yfc, mc[5], w, row_len=ctx_len, **cx_)
            cs = _ffn(cs, gn[2], mc[6], mc[7], mc[8], *ff[1], **cx_)
    return xs.reshape(batch, seq, D)
```

```python
import functools
import math

import numpy as np
import jax
import jax.numpy as jnp
from jax import lax
from jax.experimental import pallas as pl
from jax.experimental.pallas import tpu as pltpu

D = 1024
DEPTH = 2
GRID_W = 64
N_MOD = 9
D_FF = 2816
BW = 512
HEADS = 4
DV = 128
DK = 64
HK = HEADS * DK
RANK = 16
GLA_NORMALIZER = 16.0
SGU_CHUNK = 128
GC = 128
NGROUP = 4
EPS = 1e-6
LANES = 128

TM_LAT = 512
FF_CHUNK = 256
GLA_C = 128
GLA_SUB = 32
GLA_NSUB = GLA_C // GLA_SUB
FFT_N1 = 64
FFT_N2 = 128
FFT_SUB = 8
FFT_K1_BLOCK = 8
ADA_TN = 1152
VMEM_LIMIT = 56 * 1024 * 1024

BF = jnp.bfloat16
F32 = jnp.float32

_NT = (((1,), (1,)), ((), ()))


def _dot(a, b):
    return jnp.dot(a, b, preferred_element_type=F32)


def _dot_nt(a, b):
    return lax.dot_general(a, b, _NT, preferred_element_type=F32)


def _sigmoid(x):
    return 1.0 / (1.0 + jnp.exp(-x))


def _silu(x):
    return x * _sigmoid(x)


def _log_sigmoid(x):
    return jnp.minimum(x, 0.0) - jnp.log(1.0 + jnp.exp(-jnp.abs(x)))


def _bf16_const(a):
    return jnp.asarray(a, F32).astype(BF)


def _const_spec(shape):
    nd = len(shape)
    return pl.BlockSpec(shape, lambda *_: (0,) * nd, pipeline_mode=pl.Buffered(1))


def _pick_spec(lead, shape):
    nd = len(shape)
    return pl.BlockSpec((None,) * len(lead) + tuple(shape), lambda *_: tuple(lead) + (0,) * nd,
                        pipeline_mode=pl.Buffered(1))


def _params(n_grid):
    return pltpu.CompilerParams(dimension_semantics=("arbitrary",) * n_grid,
                                vmem_limit_bytes=VMEM_LIMIT)


def _norm_mod(x, g, shift, scale):
    hn = x * lax.rsqrt(jnp.mean(x * x, axis=-1, keepdims=True) + EPS) * g
    return hn * (1.0 + scale) + shift


def _adaln_kernel(cb_ref, w_ref, b_ref, o_ref):
    for r in range(3):
        cv = cb_ref[r]
        s = _silu(cv)
        for j in range(ADA_TN // LANES):
            w = w_ref[:, j * LANES:(j + 1) * LANES]
            o_ref[r:r + 1, j * LANES:(j + 1) * LANES] = (
                jnp.sum(w * s, axis=0, keepdims=True) + b_ref[:, j * LANES:(j + 1) * LANES])


def _adaln(cb, w_ada, b_ada):
    nmod = N_MOD * D
    return pl.pallas_call(
        _adaln_kernel,
        grid=(DEPTH, nmod // ADA_TN),
        in_specs=[
            pl.BlockSpec((3, D, LANES), lambda l, j: (0, 0, 0)),
            pl.BlockSpec((None, D, ADA_TN), lambda l, j: (l, 0, j)),
            pl.BlockSpec((None, 1, ADA_TN), lambda l, j: (l, 0, j)),
        ],
        out_specs=pl.BlockSpec((None, 3, ADA_TN), lambda l, j: (l, 0, j)),
        out_shape=jax.ShapeDtypeStruct((DEPTH, 3, nmod), F32),
        compiler_params=_params(2),
        name="adaln",
    )(cb, w_ada, b_ada.reshape(DEPTH, 1, nmod))


def _ffn_kernel(*refs, final):
    if final:
        s_ref, g_ref, sh_ref, sc_ref, gt_ref, w1_ref, w3_ref, w2_ref, gf_ref, o_ref = refs
    else:
        s_ref, g_ref, sh_ref, sc_ref, gt_ref, w1_ref, w3_ref, w2_ref, o_ref = refs
    s = s_ref[...]
    h = _norm_mod(s, g_ref[...], sh_ref[...], sc_ref[...]).astype(BF)
    acc = jnp.zeros(s.shape, F32)
    for j in range(D_FF // FF_CHUNK):
        cols = slice(j * FF_CHUNK, (j + 1) * FF_CHUNK)
        a = _dot(h, w1_ref[:, cols])
        b = _dot(h, w3_ref[:, cols])
        u = (_silu(a) * b).astype(BF)
        acc = acc + _dot(u, w2_ref[cols, :])
    out = s + 0.5 * gt_ref[...] * acc
    if final:
        out = out * lax.rsqrt(jnp.mean(out * out, axis=-1, keepdims=True) + EPS) * gf_ref[...]
    o_ref[...] = out


def _mod_spec(tiles_per_batch, per_batch):
    if per_batch:
        return pl.BlockSpec((None, 1, D), lambda i: (i // tiles_per_batch, 0, 0))
    return pl.BlockSpec((None, 1, D), lambda i: (0, 0, 0))


def _ffn(s, g, shift, scale, gate, w1, w3, w2, which, *, tm, tiles_per_batch, per_batch,
         g_final=None):
    n = s.shape[0]
    final = g_final is not None
    ms = _mod_spec(tiles_per_batch, per_batch)
    in_specs = [
        pl.BlockSpec((tm, D), lambda i: (i, 0)),
        _const_spec((1, D)), ms, ms, ms,
        _pick_spec(which, (D, D_FF)), _pick_spec(which, (D, D_FF)), _pick_spec(which, (D_FF, D)),
    ]
    args = [s, g, shift, scale, gate, w1, w3, w2]
    if final:
        in_specs.append(_const_spec((1, D)))
        args.append(g_final)
    return pl.pallas_call(
        functools.partial(_ffn_kernel, final=final),
        grid=(n // tm,),
        in_specs=in_specs,
        out_specs=pl.BlockSpec((tm, D), lambda i: (i, 0)),
        out_shape=jax.ShapeDtypeStruct((n, D), F32),
        compiler_params=_params(1),
        name="ffn_final" if final else "ffn",
    )(*args)


def _prep_kernel(*refs, with_fnet):
    if with_fnet:
        (x_ref, g_ref, sh_ref, sc_ref, wq_ref, wkT_ref, wv_ref, wa_ref, waT_ref, wfn_ref, ccs_ref,
         h_ref, q_ref, kT_ref, v_ref, a_ref, aT_ref, wr_ref, wi_ref) = refs
    else:
        (x_ref, g_ref, sh_ref, sc_ref, wq_ref, wkT_ref, wv_ref, wa_ref, waT_ref,
         h_ref, q_ref, kT_ref, v_ref, a_ref, aT_ref) = refs
    h = _norm_mod(x_ref[...], g_ref[...], sh_ref[...], sc_ref[...]).astype(BF)
    h_ref[...] = h
    if with_fnet:
        fn = _dot(h, wfn_ref[...]).astype(BF)
        for g in range(NGROUP):
            pq = _dot(fn[:, g * GC:(g + 1) * GC], ccs_ref[...])
            wr_ref[:, g * GC:(g + 1) * GC] = pq[:, :GC]
            wi_ref[:, g * GC:(g + 1) * GC] = pq[:, GC:]
    q_ref[...] = (_dot(h, wq_ref[...]) * (DK ** -0.5)).astype(BF)
    kT_ref[...] = _dot_nt(wkT_ref[...], h).astype(BF)
    v_ref[...] = _dot(h, wv_ref[...]).astype(BF)
    a_ref[...] = _dot(h, wa_ref[...]).astype(BF)
    aT_ref[...] = _dot_nt(waT_ref[...], h).astype(BF)


def _prep(x, g, shift, scale, w, *, tm, tiles_per_batch, per_batch, with_fnet):
    n = x.shape[0]
    ms = _mod_spec(tiles_per_batch, per_batch)
    rows = lambda width: pl.BlockSpec((tm, width), lambda i: (i, 0))
    cols = lambda height: pl.BlockSpec((height, tm), lambda i: (0, i))
    in_specs = [
        rows(D), _const_spec((1, D)), ms, ms,
        _const_spec((D, HK)), _const_spec((HK, D)), _const_spec((D, BW)), _const_spec((D, LANES)),
        _const_spec((2 * RANK, D)),
    ]
    args = [x, g, shift, scale, w["q"], w["kT"], w["v"], w["a"], w["aT"]]
    out_specs = [rows(D), rows(HK), cols(HK), rows(BW), rows(LANES), cols(2 * RANK)]
    out_shape = [
        jax.ShapeDtypeStruct((n, D), BF), jax.ShapeDtypeStruct((n, HK), BF),
        jax.ShapeDtypeStruct((HK, n), BF), jax.ShapeDtypeStruct((n, BW), BF),
        jax.ShapeDtypeStruct((n, LANES), BF), jax.ShapeDtypeStruct((2 * RANK, n), BF),
    ]
    if with_fnet:
        in_specs += [_const_spec((D, BW)), _const_spec((GC, 2 * GC))]
        args += [w["fn"], w["ccs"]]
        out_specs += [rows(BW), rows(BW)]
        out_shape += [jax.ShapeDtypeStruct((n, BW), F32)] * 2
    return pl.pallas_call(
        functools.partial(_prep_kernel, with_fnet=with_fnet),
        grid=(n // tm,),
        in_specs=in_specs,
        out_specs=out_specs,
        out_shape=out_shape,
        compiler_params=_params(1),
        name="prep_fnet" if with_fnet else "prep",
    )(*args)


def _bd_mask():
    r = lax.broadcasted_iota(jnp.int32, (2 * DK, 2 * DV), 0) // DK
    c = lax.broadcasted_iota(jnp.int32, (2 * DK, 2 * DV), 1) // DV
    return r == c


def _gla_consts():
    c = GLA_C
    t = np.arange(c)
    sub = t // GLA_SUB
    lc = np.stack([t[None, :] <= t[:, None], t[None, :] >= t[:, None]]).astype(np.float32)
    tt, jj = t[:, None], t[None, :]
    sub_end = (sub * GLA_SUB + GLA_SUB - 1)[None, :]
    sub_start = (sub * GLA_SUB)[None, :]
    mk_f = (tt > jj) & (tt <= sub_end)
    mk_b = (tt < jj) & (tt >= sub_start)
    me_f = tt > jj
    me_b = tt < jj
    ones = np.ones((c, c), bool)
    mcat = np.stack([np.concatenate([mk_f, me_f, ones], axis=1),
                     np.concatenate([mk_b, me_b, ones], axis=1)]).astype(np.float32)
    return jnp.asarray(lc, BF), jnp.asarray(mcat, BF)


def _gla_direction(dr, q_ref, kT_ref, v_ref, a_ref, aT_ref, wa2_ref, wa2T_ref, ba2_ref, ba2T_ref,
                   lc_ref, mcat_ref, state_ref, o_ref, nchunk):
    c = GLA_C
    inv = 1.0 / GLA_NORMALIZER
    g = (_log_sigmoid(_dot(a_ref[...], wa2_ref[dr]) + ba2_ref[dr]) * inv).astype(BF)
    gT = (_log_sigmoid(_dot(wa2T_ref[dr], aT_ref[...]) + ba2T_ref[dr]) * inv).astype(BF)
    col_sub = lax.broadcasted_iota(jnp.int32, (2 * DK, c), 1) // GLA_SUB
    ii = lax.broadcasted_iota(jnp.int32, (c, 2 * c), 0)
    jj = lax.broadcasted_iota(jnp.int32, (c, 2 * c), 1) % c
    tri = (jj <= ii) if dr == 0 else (jj >= ii)
    bd = _bd_mask()
    zero_kd = jnp.zeros((DK, c), BF)
    zero_v = jnp.zeros((c, DV), BF)
    state = [state_ref[dr, 0], state_ref[dr, 1]]
    order = range(nchunk) if dr == 0 else range(nchunk - 1, -1, -1)
    for ci in order:
        rows = slice(ci * c, (ci + 1) * c)
        b = _dot(lc_ref[dr], g[rows])
        fm = _dot(gT[:, rows], mcat_ref[dr])
        kc = kT_ref[:, rows].astype(F32)
        ksub = (kc * jnp.exp(fm[:, 0:c])).astype(BF)
        kend = (kc * jnp.exp(fm[:, c:2 * c])).astype(BF)
        dec = jnp.exp(fm[:, 2 * c:3 * c])
        qc = q_ref[rows, :].astype(F32)
        qb = (qc * jnp.exp(b)).astype(BF)
        for p in range(2):
            lanes = slice(p * LANES, (p + 1) * LANES)
            qp = qc[:, lanes]
            bp = b[:, lanes]
            kp = ksub[lanes]
            qblocks = []
            kblocks = []
            for j in range(GLA_NSUB):
                if dr == 0:
                    ref_row, lo, hi = j * GLA_SUB + GLA_SUB - 1, j * GLA_SUB, c
                else:
                    ref_row, lo, hi = j * GLA_SUB, 0, (j + 1) * GLA_SUB
                live = (qp[lo:hi] * jnp.exp(bp[lo:hi] - bp[ref_row:ref_row + 1, :])).astype(BF)
                parts = []
                if lo > 0:
                    parts.append(jnp.zeros((lo, LANES), BF))
                parts.append(live)
                if hi < c:
                    parts.append(jnp.zeros((c - hi, LANES), BF))
                qblocks.append(jnp.concatenate(parts, axis=0) if len(parts) > 1 else live)
                kj = jnp.where(col_sub == j, kp, jnp.zeros_like(kp))
                kblocks.append(jnp.concatenate([
                    jnp.concatenate([kj[:DK], zero_kd], axis=1),
                    jnp.concatenate([zero_kd, kj[DK:]], axis=1)], axis=0))
            qhat = jnp.concatenate(qblocks, axis=1)
            khat = jnp.concatenate(kblocks, axis=0)
            att = jnp.where(tri, _dot(qhat, khat), 0.0).astype(BF)
            vp = v_ref[rows, p * 2 * DV:(p + 1) * 2 * DV]
            vbd = jnp.concatenate([
                jnp.concatenate([vp[:, :DV], zero_v], axis=1),
                jnp.concatenate([zero_v, vp[:, DV:]], axis=1)], axis=0)
            o_p = _dot(att, vbd) + _dot(qb[:, lanes], state[p].astype(BF))
            o_ref[rows, p * 2 * DV:(p + 1) * 2 * DV] = o_p.astype(BF)
            upd = _dot(kend[lanes], vp)
            dp = dec[lanes]
            state[p] = jnp.concatenate([dp, dp], axis=1) * state[p] + jnp.where(bd, upd, 0.0)
    state_ref[dr, 0] = state[0]
    state_ref[dr, 1] = state[1]


def _gla_kernel(qf_ref, kTf_ref, vf_ref, af_ref, aTf_ref, qb_ref, kTb_ref, vb_ref, ab_ref, aTb_ref,
                s0_ref, wa2_ref, wa2T_ref, ba2_ref, ba2T_ref, lc_ref, mcat_ref,
                of_ref, ob_ref, sfin_ref, state_ref, *, tm, tiles_per_batch):
    @pl.when(pl.program_id(0) % tiles_per_batch == 0)
    def _():
        state_ref[...] = s0_ref[...]

    shared = (wa2_ref, wa2T_ref, ba2_ref, ba2T_ref, lc_ref, mcat_ref, state_ref)
    _gla_direction(0, qf_ref, kTf_ref, vf_ref, af_ref, aTf_ref, *shared, of_ref, tm // GLA_C)
    _gla_direction(1, qb_ref, kTb_ref, vb_ref, ab_ref, aTb_ref, *shared, ob_ref, tm // GLA_C)
    sfin_ref[...] = state_ref[...]


def _gla(q, kT, v, a, aT, s0, w, *, tm, tiles_per_batch):
    n = q.shape[0]
    nt = n // tm
    batch = nt // tiles_per_batch
    lc, mcat = _gla_consts()
    tpb = tiles_per_batch
    fwd = lambda i: i
    bwd = lambda i: (i // tpb) * tpb + (tpb - 1 - i % tpb)

    def tile_specs(idx):
        return [
            pl.BlockSpec((tm, HK), lambda i: (idx(i), 0)),
            pl.BlockSpec((HK, tm), lambda i: (0, idx(i))),
            pl.BlockSpec((tm, BW), lambda i: (idx(i), 0)),
            pl.BlockSpec((tm, LANES), lambda i: (idx(i), 0)),
            pl.BlockSpec((2 * RANK, tm), lambda i: (0, idx(i))),
        ]

    state_block = (None, 2, 2, 2 * DK, 2 * DV)
    in_specs = tile_specs(fwd) + tile_specs(bwd) + [
        pl.BlockSpec(state_block, lambda i: (i // tpb, 0, 0, 0, 0)),
        _const_spec((2, LANES, HK)), _const_spec((2, HK, 2 * RANK)),
        _const_spec((2, 1, HK)), _const_spec((2, HK, 1)),
        _const_spec((2, GLA_C, GLA_C)), _const_spec((2, GLA_C, 3 * GLA_C)),
    ]
    out_specs = [
        pl.BlockSpec((tm, BW), lambda i: (fwd(i), 0)),
        pl.BlockSpec((tm, BW), lambda i: (bwd(i), 0)),
        pl.BlockSpec(state_block, lambda i: (i // tpb, 0, 0, 0, 0)),
    ]
    out_shape = [
        jax.ShapeDtypeStruct((n, BW), BF), jax.ShapeDtypeStruct((n, BW), BF),
        jax.ShapeDtypeStruct((batch, 2, 2, 2 * DK, 2 * DV), F32),
    ]
    tiles = (q, kT, v, a, aT)
    return pl.pallas_call(
        functools.partial(_gla_kernel, tm=tm, tiles_per_batch=tpb),
        grid=(nt,),
        in_specs=in_specs,
        out_specs=out_specs,
        out_shape=out_shape,
        scratch_shapes=[pltpu.VMEM((2, 2, 2 * DK, 2 * DV), F32)],
        compiler_params=_params(1),
        name="gla",
    )(*tiles, *tiles, s0, w["a2"], w["a2T"], w["ba2"], w["ba2T"], lc, mcat)


def _dft_cos_sin(n):
    k = np.arange(n)
    ang = 2.0 * np.pi * ((k[:, None] * k[None, :]) % n) / n
    return np.cos(ang), np.sin(ang)


def _fnet1_kernel(x_ref, g_ref, sh_ref, sc_ref, wfn_ref, ccs_ref, f_ref, tc_ref, ts_ref,
                  zr_ref, zi_ref):
    rows = FFT_N1 * FFT_SUB
    x = x_ref[...].reshape(rows, D)
    h = _norm_mod(x, g_ref[...], sh_ref[...], sc_ref[...]).astype(BF)
    fn = _dot(h, wfn_ref[...]).astype(BF)
    re, im = [], []
    for g in range(NGROUP):
        pq = _dot(fn[:, g * GC:(g + 1) * GC], ccs_ref[...])
        re.append(pq[:, :GC].astype(BF))
        im.append(pq[:, GC:].astype(BF))
    w = jnp.concatenate([jnp.concatenate(re, axis=1), jnp.concatenate(im, axis=1)], axis=0)
    y = _dot(f_ref[...], w)
    yr, yi = y[:rows], y[rows:]
    tc = jnp.concatenate([tc_ref[...]] * (BW // LANES), axis=1)
    ts = jnp.concatenate([ts_ref[...]] * (BW // LANES), axis=1)
    zr_ref[...] = (yr * tc + yi * ts).reshape(FFT_N1, FFT_SUB, BW)
    zi_ref[...] = (yi * tc - yr * ts).reshape(FFT_N1, FFT_SUB, BW)


def _fft2_kernel(zr_ref, zi_ref, f_ref, o_ref):
    for k in range(zr_ref.shape[0]):
        z = jnp.concatenate([zr_ref[k].astype(BF), zi_ref[k].astype(BF)], axis=0)
        o_ref[k] = _dot(f_ref[...], z)


def _fnet_latent(x, g, shift, scale, w, batch):
    n1, n2, sub = FFT_N1, FFT_N2, FFT_SUB
    rows = n1 * sub
    c1, s1 = _dft_cos_sin(n1)
    eye = np.eye(sub)
    f1 = np.block([[np.kron(c1, eye), np.kron(s1, eye)],
                   [np.kron(-s1, eye), np.kron(c1, eye)]]) / math.sqrt(n1)
    k1 = np.arange(n1)[None, :, None]
    m2 = np.arange(n2).reshape(n2 // sub, 1, sub)
    ang = (2.0 * np.pi * (k1 * m2) / (n1 * n2)).reshape(n2 // sub, rows, 1)
    tc = np.broadcast_to(np.cos(ang), (n2 // sub, rows, LANES)).astype(np.float32)
    ts = np.broadcast_to(np.sin(ang), (n2 // sub, rows, LANES)).astype(np.float32)
    mod = pl.BlockSpec((None, 1, D), lambda b, j: (b, 0, 0))
    zblk = pl.BlockSpec((None, n1, sub, BW), lambda b, j: (b, 0, j, 0))
    tw = pl.BlockSpec((None, rows, LANES), lambda b, j: (j, 0, 0))
    zr, zi = pl.pallas_call(
        _fnet1_kernel,
        grid=(batch, n2 // sub),
        in_specs=[pl.BlockSpec((None, n1, sub, D), lambda b, j: (b, 0, j, 0)),
                  _const_spec((1, D)), mod, mod, _const_spec((D, BW)), _const_spec((GC, 2 * GC)),
                  _const_spec((2 * rows, 2 * rows)), tw, tw],
        out_specs=[zblk, zblk],
        out_shape=[jax.ShapeDtypeStruct((batch, n1, n2, BW), F32)] * 2,
        compiler_params=_params(2),
        name="fnet1",
    )(x.reshape(batch, n1, n2, D), g, shift, scale, w["fn"], w["ccs"], _bf16_const(f1),
      jnp.asarray(tc), jnp.asarray(ts))
    c2, s2 = _dft_cos_sin(n2)
    f2 = np.concatenate([c2, s2], axis=1) / math.sqrt(n2 * GC)
    kblk = pl.BlockSpec((None, FFT_K1_BLOCK, n2, BW), lambda b, k: (b, k, 0, 0))
    return pl.pallas_call(
        _fft2_kernel,
        grid=(batch, n1 // FFT_K1_BLOCK),
        in_specs=[kblk, kblk, _const_spec((n2, 2 * n2))],
        out_specs=kblk,
        out_shape=jax.ShapeDtypeStruct((batch, n1, n2, BW), F32),
        compiler_params=_params(2),
        name="fft2",
    )(zr, zi, _bf16_const(f2))


def _fft_direct(wr, wi, batch, seq):
    c2, s2 = _dft_cos_sin(seq)
    f2 = np.concatenate([c2, s2], axis=1) / math.sqrt(seq * GC)
    zblk = pl.BlockSpec((1, seq, BW), lambda b: (b, 0, 0))
    y = pl.pallas_call(
        _fft2_kernel,
        grid=(batch,),
        in_specs=[zblk, zblk, _const_spec((seq, 2 * seq))],
        out_specs=zblk,
        out_shape=jax.ShapeDtypeStruct((batch, seq, BW), F32),
        compiler_params=_params(1),
        name="fft_direct",
    )(wr.reshape(batch, seq, BW), wi.reshape(batch, seq, BW), _bf16_const(f2))
    return y.reshape(batch * seq, BW)


def _mix_kernel(*refs, tm, row_len, k1_major):
    if k1_major:
        perm_ref, refs = refs[0], refs[1:]
    (x_ref, h_ref, of_ref, ob_ref, yf_ref, gt_ref, wr_ref, go_ref, wsu_ref, wsv_ref,
     wcb_ref, wcc_ref, wcx_ref, wsgu_ref, bsgu_ref, wconv_ref, wbr_ref, wgate_ref,
     bgate_ref, wout_ref, o_ref, br_ref) = refs
    h = h_ref[...]
    if k1_major:
        yf = _dot(perm_ref[...], yf_ref[...].reshape(tm, BW).astype(BF)).astype(BF)
    else:
        yf = yf_ref[...].astype(BF)

    def gated(k, branch):
        gate = _sigmoid(_dot(h, wgate_ref[k]) + bgate_ref[k])
        return gate * _dot(branch, wbr_ref[k])

    r = _dot(h, wr_ref[...])
    for hd in range(HEADS):
        lanes = slice(hd * DV, (hd + 1) * DV)
        oh = of_ref[:, lanes].astype(F32) + ob_ref[:, lanes].astype(F32)
        on = oh * lax.rsqrt(jnp.mean(oh * oh, axis=-1, keepdims=True) + EPS) * go_ref[:, lanes]
        br_ref[:, lanes] = (on * _silu(r[:, lanes])).astype(BF)
    m = gated(0, br_ref[...])
    su = _dot(h, wsu_ref[...])
    sv = _dot(h, wsv_ref[...])
    for g in range(NGROUP):
        lanes = slice(g * GC, (g + 1) * GC)
        vg = sv[:, lanes]
        vc = vg - jnp.mean(vg, axis=-1, keepdims=True)
        z = (vc * lax.rsqrt(jnp.mean(vc * vc, axis=-1, keepdims=True) + EPS)).astype(BF)
        for n in range(tm // SGU_CHUNK):
            rows = slice(n * SGU_CHUNK, (n + 1) * SGU_CHUNK)
            s = _dot(wsgu_ref[g], z[rows]) + bsgu_ref[:, lanes]
            br_ref[rows, lanes] = (su[rows, lanes] * s).astype(BF)
    m = m + gated(1, br_ref[...])
    m = m + gated(2, yf)
    zc = _dot(h, wcc_ref[...]) * _dot(h, wcx_ref[...])
    pos = lax.broadcasted_iota(jnp.int32, (tm, BW), 0) % row_len
    left = jnp.where(pos == 0, 0.0, pltpu.roll(zc, 1, 0))
    right = jnp.where(pos == row_len - 1, 0.0, pltpu.roll(zc, tm - 1, 0))
    y = left * wconv_ref[0:1, :] + zc * wconv_ref[1:2, :] + right * wconv_ref[2:3, :]
    conv = (_dot(h, wcb_ref[...]) * y).astype(BF)
    m = m + gated(3, conv)
    o_ref[...] = x_ref[...] + gt_ref[...] * _dot(m.astype(BF), wout_ref[...])


def _mix(x, h, of, ob, yf, gate, w, *, tm, tiles_per_batch, per_batch, row_len):
    n = x.shape[0]
    layer = (w["layer"],)
    ms = _mod_spec(tiles_per_batch, per_batch)
    tile = lambda width: pl.BlockSpec((tm, width), lambda i: (i, 0))
    k1_major = yf.ndim == 4
    pre_specs, pre_args = [], []
    if k1_major:
        k2_tile = tm // FFT_N1
        assert tm % FFT_N1 == 0 and k2_tile % 8 == 0
        tpb = tiles_per_batch
        yf_spec = pl.BlockSpec((None, FFT_N1, k2_tile, BW), lambda i: (i // tpb, 0, i % tpb, 0))
        r = np.arange(tm)
        src = (r % FFT_N1) * k2_tile + r // FFT_N1
        perm = np.zeros((tm, tm), np.float32)
        perm[r, src] = 1.0
        pre_specs, pre_args = [_const_spec((tm, tm))], [jnp.asarray(perm, BF)]
    else:
        yf_spec = tile(BW)
    in_specs = pre_specs + [
        tile(D), tile(D), tile(BW), tile(BW), yf_spec, ms,
        _const_spec((D, BW)), _const_spec((1, BW)),
        _const_spec((D, BW)), _const_spec((D, BW)), _const_spec((D, BW)), _const_spec((D, BW)),
        _const_spec((D, BW)),
        _pick_spec(layer, (NGROUP, SGU_CHUNK, SGU_CHUNK)), _const_spec((SGU_CHUNK, BW)),
        _const_spec((3, BW)), _pick_spec(layer, (4, BW, D)), _pick_spec(layer, (4, D, D)),
        _const_spec((4, 1, D)), _pick_spec(layer, (D, D)),
    ]
    return pl.pallas_call(
        functools.partial(_mix_kernel, tm=tm, row_len=row_len, k1_major=k1_major),
        grid=(n // tm,),
        in_specs=in_specs,
        out_specs=tile(D),
        out_shape=jax.ShapeDtypeStruct((n, D), F32),
        scratch_shapes=[pltpu.VMEM((tm, BW), BF)],
        compiler_params=_params(1),
        name="mix",
    )(*pre_args, x, h, of, ob, yf, gate, w["r"], w["go"], w["su"], w["sv"], w["cb"], w["cc"], w["cx"],
      w["sgu"], w["bsgu"], w["conv"], w["branch"], w["gate"], w["bgate"], w["out"])


def _layer_weights(i, w_in, w_gla_a2, b_gla_a2, g_gla_norm, w_sgu, b_sgu, w_conv, w_branch,
                   w_gate, b_gate, w_out):
    edges = np.cumsum([0, HK, HK, BW, BW, 2 * RANK, BW, BW, BW, BW, BW, BW])
    q, k, v, r, a, su, sv, fn, cb, cc, cx = [
        w_in[i, :, edges[j]:edges[j + 1]].astype(BF) for j in range(11)]
    a2 = w_gla_a2[i].astype(BF)
    a2_tok = jnp.zeros((2, LANES, HK), BF)
    a2_tok = a2_tok.at[0, :RANK].set(a2[0]).at[1, RANK:2 * RANK].set(a2[1])
    cc_, sc_ = _dft_cos_sin(GC)
    return {
        "q": q, "kT": k.T, "v": v, "r": r,
        "a": jnp.pad(a, ((0, 0), (0, LANES - 2 * RANK))), "aT": a.T,
        "a2": a2_tok, "a2T": jnp.transpose(a2_tok[:, :2 * RANK, :], (0, 2, 1)),
        "ba2": b_gla_a2[i][:, None, :], "ba2T": b_gla_a2[i][:, :, None],
        "go": g_gla_norm[i].reshape(1, BW),
        "su": su, "sv": sv, "fn": fn, "cb": cb, "cc": cc, "cx": cx,
        "ccs": _bf16_const(np.concatenate([cc_, -sc_], axis=1)),
        "layer": i,
        "sgu": w_sgu.astype(BF), "branch": w_branch.astype(BF), "gate": w_gate.astype(BF),
        "out": w_out.astype(BF),
        "bsgu": jnp.repeat(jnp.transpose(b_sgu[i]), GC, axis=1),
        "conv": w_conv[i],
        "bgate": b_gate[i][:, None, :],
    }


def kernel(x, c, ctx, c_ctx, w_ada, b_ada, g_norm, w_ff1, w_ff3, w_ff2, w_in, w_gla_a2, b_gla_a2,
           g_gla_norm, w_sgu, b_sgu, w_conv, w_branch, w_gate, b_gate, w_out, g_final):
    batch, seq, _ = x.shape
    ctx_len = ctx.shape[1]
    assert seq % TM_LAT == 0 and TM_LAT % GRID_W == 0 and seq == FFT_N1 * FFT_N2
    assert ctx_len % GLA_C == 0
    tpb = seq // TM_LAT

    cvec = jnp.concatenate([c, c_ctx[None, :]], axis=0)
    cb = jnp.broadcast_to(cvec[:, :, None], (3, D, LANES))
    mods = _adaln(cb, w_ada, b_ada).reshape(DEPTH, 3, N_MOD, 1, D)

    xs = x.reshape(batch * seq, D)
    cs = ctx.reshape(batch * ctx_len, D)
    lat = dict(tm=TM_LAT, tiles_per_batch=tpb, per_batch=True)
    cx_ = dict(tm=ctx_len, tiles_per_batch=1, per_batch=False)
    s_zero = jnp.zeros((batch, 2, 2, 2 * DK, 2 * DV), F32)
    ff = (w_ff1.astype(BF), w_ff3.astype(BF), w_ff2.astype(BF))

    for i in range(DEPTH):
        last = i == DEPTH - 1
        ml = [mods[i, 0:2, j] for j in range(N_MOD)]
        mc = [mods[i, 2:3, j] for j in range(N_MOD)]
        gn = g_norm[i][:, None, :]
        w = _layer_weights(i, w_in, w_gla_a2, b_gla_a2, g_gla_norm, w_sgu, b_sgu, w_conv,
                           w_branch, w_gate, b_gate, w_out)

        xs = _ffn(xs, gn[0], ml[0], ml[1], ml[2], *ff, (i, 0), **lat)
        cs = _ffn(cs, gn[0], mc[0], mc[1], mc[2], *ff, (i, 0), **cx_)

        hc, *gla_c, wrc, wic = _prep(cs, gn[1], mc[3], mc[4], w, with_fnet=True, **cx_)
        ofc, obc, s_ctx = _gla(*gla_c, s_zero, w, tm=ctx_len, tiles_per_batch=1)

        hl, *gla_l = _prep(xs, gn[1], ml[3], ml[4], w, with_fnet=False, **lat)
        ofl, obl, _ = _gla(*gla_l, s_ctx, w, tm=TM_LAT, tiles_per_batch=tpb)
        yf = _fnet_latent(xs.reshape(batch, seq, D), gn[1], ml[3], ml[4], w, batch)
        xs = _mix(xs, hl, ofl, obl, yf, ml[5], w, row_len=GRID_W, **lat)
        xs = _ffn(xs, gn[2], ml[6], ml[7], ml[8], *ff, (i, 1),
                  g_final=g_final[None, :] if last else None, **lat)

        if not last:
            yfc = _fft_direct(wrc, wic, batch, ctx_len)
            cs = _mix(cs, hc, ofc, obc, yfc, mc[5], w, row_len=ctx_len, **cx_)
            cs = _ffn(cs, gn[2], mc[6], mc[7], mc[8], *ff, (i, 1), **cx_)
    return xs.reshape(batch, seq, D)
```

```python
import functools
import math

import numpy as np
import jax
import jax.numpy as jnp
from jax import lax
from jax.experimental import pallas as pl
from jax.experimental.pallas import tpu as pltpu

D = 1024
DEPTH = 2
GRID_W = 64
N_MOD = 9
D_FF = 2816
BW = 512
HEADS = 4
DV = 128
DK = 64
HK = HEADS * DK
RANK = 16
GLA_NORMALIZER = 16.0
SGU_CHUNK = 128
GC = 128
NGROUP = 4
EPS = 1e-6
LANES = 128

TM_LAT = 512
TM_FFN = 1024
FF_CHUNK = 256
GLA_C = 128
GLA_SUB = 32
GLA_NSUB = GLA_C // GLA_SUB
FFT_N1 = 64
FFT_N2 = 128
FFT_SUB = 8
FFT_K1_BLOCK = 8
ADA_TN = 1152
VMEM_LIMIT = 56 * 1024 * 1024

BF = jnp.bfloat16
F32 = jnp.float32

_NT = (((1,), (1,)), ((), ()))


def _dot(a, b):
    return jnp.dot(a, b, preferred_element_type=F32)


def _dot_nt(a, b):
    return lax.dot_general(a, b, _NT, preferred_element_type=F32)


def _sigmoid(x):
    return 1.0 / (1.0 + jnp.exp(-x))


def _silu(x):
    return x * _sigmoid(x)


def _log_sigmoid(x):
    return jnp.minimum(x, 0.0) - jnp.log(1.0 + jnp.exp(-jnp.abs(x)))


def _bf16_const(a):
    return jnp.asarray(a, F32).astype(BF)


def _const_spec(shape):
    nd = len(shape)
    return pl.BlockSpec(shape, lambda *_: (0,) * nd, pipeline_mode=pl.Buffered(1))


def _pick_spec(lead, shape):
    nd = len(shape)
    return pl.BlockSpec((None,) * len(lead) + tuple(shape), lambda *_: tuple(lead) + (0,) * nd,
                        pipeline_mode=pl.Buffered(1))


def _params(n_grid):
    return pltpu.CompilerParams(dimension_semantics=("arbitrary",) * n_grid,
                                vmem_limit_bytes=VMEM_LIMIT)


def _norm_mod(x, g, shift, scale):
    hn = x * lax.rsqrt(jnp.mean(x * x, axis=-1, keepdims=True) + EPS) * g
    return hn * (1.0 + scale) + shift


def _adaln_kernel(cb_ref, w_ref, b_ref, o_ref):
    for r in range(3):
        cv = cb_ref[r]
        s = _silu(cv)
        for j in range(ADA_TN // LANES):
            w = w_ref[:, j * LANES:(j + 1) * LANES]
            o_ref[r:r + 1, j * LANES:(j + 1) * LANES] = (
                jnp.sum(w * s, axis=0, keepdims=True) + b_ref[:, j * LANES:(j + 1) * LANES])


def _adaln(cb, w_ada, b_ada):
    nmod = N_MOD * D
    return pl.pallas_call(
        _adaln_kernel,
        grid=(DEPTH, nmod // ADA_TN),
        in_specs=[
            pl.BlockSpec((3, D, LANES), lambda l, j: (0, 0, 0)),
            pl.BlockSpec((None, D, ADA_TN), lambda l, j: (l, 0, j)),
            pl.BlockSpec((None, 1, ADA_TN), lambda l, j: (l, 0, j)),
        ],
        out_specs=pl.BlockSpec((None, 3, ADA_TN), lambda l, j: (l, 0, j)),
        out_shape=jax.ShapeDtypeStruct((DEPTH, 3, nmod), F32),
        compiler_params=_params(2),
        name="adaln",
    )(cb, w_ada, b_ada.reshape(DEPTH, 1, nmod))


def _ffn_kernel(*refs, tail):
    s_ref, g_ref, sh_ref, sc_ref, gt_ref, w1_ref, w3_ref, w2_ref = refs[:8]
    rest = refs[8:]
    s = s_ref[...]
    h = _norm_mod(s, g_ref[...], sh_ref[...], sc_ref[...]).astype(BF)
    acc = jnp.zeros(s.shape, F32)
    for j in range(D_FF // FF_CHUNK):
        cols = slice(j * FF_CHUNK, (j + 1) * FF_CHUNK)
        a = _dot(h, w1_ref[:, cols])
        b = _dot(h, w3_ref[:, cols])
        u = (_silu(a) * b).astype(BF)
        acc = acc + _dot(u, w2_ref[cols, :])
    out = s + 0.5 * gt_ref[...] * acc
    if tail is None:
        rest[0][...] = out
    elif tail == "final":
        gf_ref, o_ref = rest
        o_ref[...] = out * lax.rsqrt(jnp.mean(out * out, axis=-1, keepdims=True) + EPS) * gf_ref[...]
    else:
        (g2_ref, sh2_ref, sc2_ref, wq_ref, wkT_ref, wv_ref, wa_ref, waT_ref) = rest[:8]
        rest = rest[8:]
        if tail == "prep_fnet":
            wfn_ref, ccs_ref = rest[:2]
            rest = rest[2:]
        o_ref, h_ref, q_ref, kT_ref, v_ref, a_ref, aT_ref = rest[:7]
        o_ref[...] = out
        h2 = _norm_mod(out, g2_ref[...], sh2_ref[...], sc2_ref[...]).astype(BF)
        h_ref[...] = h2
        q_ref[...] = (_dot(h2, wq_ref[...]) * (DK ** -0.5)).astype(BF)
        kT_ref[...] = _dot_nt(wkT_ref[...], h2).astype(BF)
        v_ref[...] = _dot(h2, wv_ref[...]).astype(BF)
        a_ref[...] = _dot(h2, wa_ref[...]).astype(BF)
        aT_ref[...] = _dot_nt(waT_ref[...], h2).astype(BF)
        if tail == "prep_fnet":
            wr_ref, wi_ref = rest[7:]
            fn = _dot(h2, wfn_ref[...]).astype(BF)
            for g in range(NGROUP):
                pq = _dot(fn[:, g * GC:(g + 1) * GC], ccs_ref[...])
                wr_ref[:, g * GC:(g + 1) * GC] = pq[:, :GC]
                wi_ref[:, g * GC:(g + 1) * GC] = pq[:, GC:]


def _mod_spec(tiles_per_batch, per_batch):
    if per_batch:
        return pl.BlockSpec((None, 1, D), lambda i: (i // tiles_per_batch, 0, 0))
    return pl.BlockSpec((None, 1, D), lambda i: (0, 0, 0))


def _ffn(s, g, shift, scale, gate, w1, w3, w2, which, *, tm, tiles_per_batch, per_batch,
         g_final=None, prep=None):
    n = s.shape[0]
    ms = _mod_spec(tiles_per_batch, per_batch)
    rows = lambda width: pl.BlockSpec((tm, width), lambda i: (i, 0))
    cols = lambda height: pl.BlockSpec((height, tm), lambda i: (0, i))
    in_specs = [
        rows(D), _const_spec((1, D)), ms, ms, ms,
        _pick_spec(which, (D, D_FF)), _pick_spec(which, (D, D_FF)), _pick_spec(which, (D_FF, D)),
    ]
    args = [s, g, shift, scale, gate, w1, w3, w2]
    out_specs = [rows(D)]
    out_shape = [jax.ShapeDtypeStruct((n, D), F32)]
    tail = None
    if g_final is not None:
        tail = "final"
        in_specs.append(_const_spec((1, D)))
        args.append(g_final)
    elif prep is not None:
        g2, shift2, scale2, w, with_fnet = prep
        tail = "prep_fnet" if with_fnet else "prep"
        in_specs += [_const_spec((1, D)), ms, ms,
                     _const_spec((D, HK)), _const_spec((HK, D)), _const_spec((D, BW)),
                     _const_spec((D, LANES)), _const_spec((2 * RANK, D))]
        args += [g2, shift2, scale2, w["q"], w["kT"], w["v"], w["a"], w["aT"]]
        out_specs += [rows(D), rows(HK), cols(HK), rows(BW), rows(LANES), cols(2 * RANK)]
        out_shape += [
            jax.ShapeDtypeStruct((n, D), BF), jax.ShapeDtypeStruct((n, HK), BF),
            jax.ShapeDtypeStruct((HK, n), BF), jax.ShapeDtypeStruct((n, BW), BF),
            jax.ShapeDtypeStruct((n, LANES), BF), jax.ShapeDtypeStruct((2 * RANK, n), BF),
        ]
        if with_fnet:
            in_specs += [_const_spec((D, BW)), _const_spec((GC, 2 * GC))]
            args += [w["fn"], w["ccs"]]
            out_specs += [rows(BW), rows(BW)]
            out_shape += [jax.ShapeDtypeStruct((n, BW), F32)] * 2
    res = pl.pallas_call(
        functools.partial(_ffn_kernel, tail=tail),
        grid=(n // tm,),
        in_specs=in_specs,
        out_specs=out_specs,
        out_shape=out_shape,
        compiler_params=_params(1),
        name="ffn" if tail is None else "ffn_" + tail,
    )(*args)
    return res if prep is not None else res[0]


def _bd_mask():
    r = lax.broadcasted_iota(jnp.int32, (2 * DK, 2 * DV), 0) // DK
    c = lax.broadcasted_iota(jnp.int32, (2 * DK, 2 * DV), 1) // DV
    return r == c


def _gla_consts():
    c = GLA_C
    t = np.arange(c)
    sub = t // GLA_SUB
    lc = np.stack([t[None, :] <= t[:, None], t[None, :] >= t[:, None]]).astype(np.float32)
    tt, jj = t[:, None], t[None, :]
    sub_end = (sub * GLA_SUB + GLA_SUB - 1)[None, :]
    sub_start = (sub * GLA_SUB)[None, :]
    mk_f = (tt > jj) & (tt <= sub_end)
    mk_b = (tt < jj) & (tt >= sub_start)
    me_f = tt > jj
    me_b = tt < jj
    ones = np.ones((c, c), bool)
    mcat = np.stack([np.concatenate([mk_f, me_f, ones], axis=1),
                     np.concatenate([mk_b, me_b, ones], axis=1)]).astype(np.float32)
    return jnp.asarray(lc, BF), jnp.asarray(mcat, BF)


def _gla_direction(dr, q_ref, kT_ref, v_ref, a_ref, aT_ref, wa2_ref, wa2T_ref, ba2_ref, ba2T_ref,
                   lc_ref, mcat_ref, state_ref, o_ref, nchunk):
    c = GLA_C
    inv = 1.0 / GLA_NORMALIZER
    g = (_log_sigmoid(_dot(a_ref[...], wa2_ref[dr]) + ba2_ref[dr]) * inv).astype(BF)
    gT = (_log_sigmoid(_dot(wa2T_ref[dr], aT_ref[...]) + ba2T_ref[dr]) * inv).astype(BF)
    col_sub = lax.broadcasted_iota(jnp.int32, (2 * DK, c), 1) // GLA_SUB
    ii = lax.broadcasted_iota(jnp.int32, (c, 2 * c), 0)
    jj = lax.broadcasted_iota(jnp.int32, (c, 2 * c), 1) % c
    tri = (jj <= ii) if dr == 0 else (jj >= ii)
    bd = _bd_mask()
    zero_kd = jnp.zeros((DK, c), BF)
    zero_v = jnp.zeros((c, DV), BF)
    state = [state_ref[dr, 0], state_ref[dr, 1]]
    order = range(nchunk) if dr == 0 else range(nchunk - 1, -1, -1)
    for ci in order:
        rows = slice(ci * c, (ci + 1) * c)
        b = _dot(lc_ref[dr], g[rows])
        fm = _dot(gT[:, rows], mcat_ref[dr])
        kc = kT_ref[:, rows].astype(F32)
        ksub = (kc * jnp.exp(fm[:, 0:c])).astype(BF)
        kend = (kc * jnp.exp(fm[:, c:2 * c])).astype(BF)
        dec = jnp.exp(fm[:, 2 * c:3 * c])
        qc = q_ref[rows, :].astype(F32)
        qb = (qc * jnp.exp(b)).astype(BF)
        for p in range(2):
            lanes = slice(p * LANES, (p + 1) * LANES)
            qp = qc[:, lanes]
            bp = b[:, lanes]
            kp = ksub[lanes]
            qblocks = []
            kblocks = []
            for j in range(GLA_NSUB):
                if dr == 0:
                    ref_row, lo, hi = j * GLA_SUB + GLA_SUB - 1, j * GLA_SUB, c
                else:
                    ref_row, lo, hi = j * GLA_SUB, 0, (j + 1) * GLA_SUB
                live = (qp[lo:hi] * jnp.exp(bp[lo:hi] - bp[ref_row:ref_row + 1, :])).astype(BF)
                parts = []
                if lo > 0:
                    parts.append(jnp.zeros((lo, LANES), BF))
                parts.append(live)
                if hi < c:
                    parts.append(jnp.zeros((c - hi, LANES), BF))
                qblocks.append(jnp.concatenate(parts, axis=0) if len(parts) > 1 else live)
                kj = jnp.where(col_sub == j, kp, jnp.zeros_like(kp))
                kblocks.append(jnp.concatenate([
                    jnp.concatenate([kj[:DK], zero_kd], axis=1),
                    jnp.concatenate([zero_kd, kj[DK:]], axis=1)], axis=0))
            qhat = jnp.concatenate(qblocks, axis=1)
            khat = jnp.concatenate(kblocks, axis=0)
            att = jnp.where(tri, _dot(qhat, khat), 0.0).astype(BF)
            vp = v_ref[rows, p * 2 * DV:(p + 1) * 2 * DV]
            vbd = jnp.concatenate([
                jnp.concatenate([vp[:, :DV], zero_v], axis=1),
                jnp.concatenate([zero_v, vp[:, DV:]], axis=1)], axis=0)
            o_p = _dot(att, vbd) + _dot(qb[:, lanes], state[p].astype(BF))
            o_ref[rows, p * 2 * DV:(p + 1) * 2 * DV] = o_p.astype(BF)
            upd = _dot(kend[lanes], vp)
            dp = dec[lanes]
            state[p] = jnp.concatenate([dp, dp], axis=1) * state[p] + jnp.where(bd, upd, 0.0)
    state_ref[dr, 0] = state[0]
    state_ref[dr, 1] = state[1]


def _gla_kernel(qf_ref, kTf_ref, vf_ref, af_ref, aTf_ref, qb_ref, kTb_ref, vb_ref, ab_ref, aTb_ref,
                s0_ref, wa2_ref, wa2T_ref, ba2_ref, ba2T_ref, lc_ref, mcat_ref,
                of_ref, ob_ref, sfin_ref, state_ref, *, tm, tiles_per_batch):
    @pl.when(pl.program_id(0) % tiles_per_batch == 0)
    def _():
        state_ref[...] = s0_ref[...]

    shared = (wa2_ref, wa2T_ref, ba2_ref, ba2T_ref, lc_ref, mcat_ref, state_ref)
    _gla_direction(0, qf_ref, kTf_ref, vf_ref, af_ref, aTf_ref, *shared, of_ref, tm // GLA_C)
    _gla_direction(1, qb_ref, kTb_ref, vb_ref, ab_ref, aTb_ref, *shared, ob_ref, tm // GLA_C)
    sfin_ref[...] = state_ref[...]


def _gla(q, kT, v, a, aT, s0, w, *, tm, tiles_per_batch):
    n = q.shape[0]
    nt = n // tm
    batch = nt // tiles_per_batch
    lc, mcat = _gla_consts()
    tpb = tiles_per_batch
    fwd = lambda i: i
    bwd = lambda i: (i // tpb) * tpb + (tpb - 1 - i % tpb)

    def tile_specs(idx):
        return [
            pl.BlockSpec((tm, HK), lambda i: (idx(i), 0)),
            pl.BlockSpec((HK, tm), lambda i: (0, idx(i))),
            pl.BlockSpec((tm, BW), lambda i: (idx(i), 0)),
            pl.BlockSpec((tm, LANES), lambda i: (idx(i), 0)),
            pl.BlockSpec((2 * RANK, tm), lambda i: (0, idx(i))),
        ]

    state_block = (None, 2, 2, 2 * DK, 2 * DV)
    in_specs = tile_specs(fwd) + tile_specs(bwd) + [
        pl.BlockSpec(state_block, lambda i: (i // tpb, 0, 0, 0, 0)),
        _const_spec((2, LANES, HK)), _const_spec((2, HK, 2 * RANK)),
        _const_spec((2, 1, HK)), _const_spec((2, HK, 1)),
        _const_spec((2, GLA_C, GLA_C)), _const_spec((2, GLA_C, 3 * GLA_C)),
    ]
    out_specs = [
        pl.BlockSpec((tm, BW), lambda i: (fwd(i), 0)),
        pl.BlockSpec((tm, BW), lambda i: (bwd(i), 0)),
        pl.BlockSpec(state_block, lambda i: (i // tpb, 0, 0, 0, 0)),
    ]
    out_shape = [
        jax.ShapeDtypeStruct((n, BW), BF), jax.ShapeDtypeStruct((n, BW), BF),
        jax.ShapeDtypeStruct((batch, 2, 2, 2 * DK, 2 * DV), F32),
    ]
    tiles = (q, kT, v, a, aT)
    return pl.pallas_call(
        functools.partial(_gla_kernel, tm=tm, tiles_per_batch=tpb),
        grid=(nt,),
        in_specs=in_specs,
        out_specs=out_specs,
        out_shape=out_shape,
        scratch_shapes=[pltpu.VMEM((2, 2, 2 * DK, 2 * DV), F32)],
        compiler_params=_params(1),
        name="gla",
    )(*tiles, *tiles, s0, w["a2"], w["a2T"], w["ba2"], w["ba2T"], lc, mcat)


def _dft_cos_sin(n):
    k = np.arange(n)
    ang = 2.0 * np.pi * ((k[:, None] * k[None, :]) % n) / n
    return np.cos(ang), np.sin(ang)


def _fnet1_kernel(x_ref, g_ref, sh_ref, sc_ref, wfn_ref, ccs_ref, f_ref, tc_ref, ts_ref,
                  zr_ref, zi_ref):
    rows = FFT_N1 * FFT_SUB
    x = x_ref[...].reshape(rows, D)
    h = _norm_mod(x, g_ref[...], sh_ref[...], sc_ref[...]).astype(BF)
    fn = _dot(h, wfn_ref[...]).astype(BF)
    re, im = [], []
    for g in range(NGROUP):
        pq = _dot(fn[:, g * GC:(g + 1) * GC], ccs_ref[...])
        re.append(pq[:, :GC].astype(BF))
        im.append(pq[:, GC:].astype(BF))
    w = jnp.concatenate([jnp.concatenate(re, axis=1), jnp.concatenate(im, axis=1)], axis=0)
    y = _dot(f_ref[...], w)
    yr, yi = y[:rows], y[rows:]
    tc = jnp.concatenate([tc_ref[...]] * (BW // LANES), axis=1)
    ts = jnp.concatenate([ts_ref[...]] * (BW // LANES), axis=1)
    zr_ref[...] = (yr * tc + yi * ts).reshape(FFT_N1, FFT_SUB, BW)
    zi_ref[...] = (yi * tc - yr * ts).reshape(FFT_N1, FFT_SUB, BW)


def _fft2_kernel(zr_ref, zi_ref, f_ref, o_ref):
    for k in range(zr_ref.shape[0]):
        z = jnp.concatenate([zr_ref[k].astype(BF), zi_ref[k].astype(BF)], axis=0)
        o_ref[k] = _dot(f_ref[...], z)


def _fnet_latent(x, g, shift, scale, w, batch):
    n1, n2, sub = FFT_N1, FFT_N2, FFT_SUB
    rows = n1 * sub
    c1, s1 = _dft_cos_sin(n1)
    eye = np.eye(sub)
    f1 = np.block([[np.kron(c1, eye), np.kron(s1, eye)],
                   [np.kron(-s1, eye), np.kron(c1, eye)]]) / math.sqrt(n1)
    k1 = np.arange(n1)[None, :, None]
    m2 = np.arange(n2).reshape(n2 // sub, 1, sub)
    ang = (2.0 * np.pi * (k1 * m2) / (n1 * n2)).reshape(n2 // sub, rows, 1)
    tc = np.broadcast_to(np.cos(ang), (n2 // sub, rows, LANES)).astype(np.float32)
    ts = np.broadcast_to(np.sin(ang), (n2 // sub, rows, LANES)).astype(np.float32)
    mod = pl.BlockSpec((None, 1, D), lambda b, j: (b, 0, 0))
    zblk = pl.BlockSpec((None, n1, sub, BW), lambda b, j: (b, 0, j, 0))
    tw = pl.BlockSpec((None, rows, LANES), lambda b, j: (j, 0, 0))
    zr, zi = pl.pallas_call(
        _fnet1_kernel,
        grid=(batch, n2 // sub),
        in_specs=[pl.BlockSpec((None, n1, sub, D), lambda b, j: (b, 0, j, 0)),
                  _const_spec((1, D)), mod, mod, _const_spec((D, BW)), _const_spec((GC, 2 * GC)),
                  _const_spec((2 * rows, 2 * rows)), tw, tw],
        out_specs=[zblk, zblk],
        out_shape=[jax.ShapeDtypeStruct((batch, n1, n2, BW), F32)] * 2,
        compiler_params=_params(2),
        name="fnet1",
    )(x.reshape(batch, n1, n2, D), g, shift, scale, w["fn"], w["ccs"], _bf16_const(f1),
      jnp.asarray(tc), jnp.asarray(ts))
    c2, s2 = _dft_cos_sin(n2)
    f2 = np.concatenate([c2, s2], axis=1) / math.sqrt(n2 * GC)
    kblk = pl.BlockSpec((None, FFT_K1_BLOCK, n2, BW), lambda b, k: (b, k, 0, 0))
    return pl.pallas_call(
        _fft2_kernel,
        grid=(batch, n1 // FFT_K1_BLOCK),
        in_specs=[kblk, kblk, _const_spec((n2, 2 * n2))],
        out_specs=kblk,
        out_shape=jax.ShapeDtypeStruct((batch, n1, n2, BW), F32),
        compiler_params=_params(2),
        name="fft2",
    )(zr, zi, _bf16_const(f2))


def _fft_direct(wr, wi, batch, seq):
    c2, s2 = _dft_cos_sin(seq)
    f2 = np.concatenate([c2, s2], axis=1) / math.sqrt(seq * GC)
    zblk = pl.BlockSpec((1, seq, BW), lambda b: (b, 0, 0))
    y = pl.pallas_call(
        _fft2_kernel,
        grid=(batch,),
        in_specs=[zblk, zblk, _const_spec((seq, 2 * seq))],
        out_specs=zblk,
        out_shape=jax.ShapeDtypeStruct((batch, seq, BW), F32),
        compiler_params=_params(1),
        name="fft_direct",
    )(wr.reshape(batch, seq, BW), wi.reshape(batch, seq, BW), _bf16_const(f2))
    return y.reshape(batch * seq, BW)


def _mix_kernel(*refs, tm, row_len, k1_major):
    if k1_major:
        perm_ref, refs = refs[0], refs[1:]
    (x_ref, h_ref, of_ref, ob_ref, yf_ref, gt_ref, wr_ref, go_ref, wsu_ref, wsv_ref,
     wcb_ref, wcc_ref, wcx_ref, wsgu_ref, bsgu_ref, wconv_ref, wbr_ref, wgate_ref,
     bgate_ref, wout_ref, o_ref, br_ref) = refs
    h = h_ref[...]
    if k1_major:
        yf = _dot(perm_ref[...], yf_ref[...].reshape(tm, BW).astype(BF)).astype(BF)
    else:
        yf = yf_ref[...].astype(BF)

    def gated(k, branch):
        gate = _sigmoid(_dot(h, wgate_ref[k]) + bgate_ref[k])
        return gate * _dot(branch, wbr_ref[k])

    r = _dot(h, wr_ref[...])
    for hd in range(HEADS):
        lanes = slice(hd * DV, (hd + 1) * DV)
        oh = of_ref[:, lanes].astype(F32) + ob_ref[:, lanes].astype(F32)
        on = oh * lax.rsqrt(jnp.mean(oh * oh, axis=-1, keepdims=True) + EPS) * go_ref[:, lanes]
        br_ref[:, lanes] = (on * _silu(r[:, lanes])).astype(BF)
    m = gated(0, br_ref[...])
    su = _dot(h, wsu_ref[...])
    sv = _dot(h, wsv_ref[...])
    for g in range(NGROUP):
        lanes = slice(g * GC, (g + 1) * GC)
        vg = sv[:, lanes]
        vc = vg - jnp.mean(vg, axis=-1, keepdims=True)
        z = (vc * lax.rsqrt(jnp.mean(vc * vc, axis=-1, keepdims=True) + EPS)).astype(BF)
        for n in range(tm // SGU_CHUNK):
            rows = slice(n * SGU_CHUNK, (n + 1) * SGU_CHUNK)
            s = _dot(wsgu_ref[g], z[rows]) + bsgu_ref[:, lanes]
            br_ref[rows, lanes] = (su[rows, lanes] * s).astype(BF)
    m = m + gated(1, br_ref[...])
    m = m + gated(2, yf)
    zc = _dot(h, wcc_ref[...]) * _dot(h, wcx_ref[...])
    pos = lax.broadcasted_iota(jnp.int32, (tm, BW), 0) % row_len
    left = jnp.where(pos == 0, 0.0, pltpu.roll(zc, 1, 0))
    right = jnp.where(pos == row_len - 1, 0.0, pltpu.roll(zc, tm - 1, 0))
    y = left * wconv_ref[0:1, :] + zc * wconv_ref[1:2, :] + right * wconv_ref[2:3, :]
    conv = (_dot(h, wcb_ref[...]) * y).astype(BF)
    m = m + gated(3, conv)
    o_ref[...] = x_ref[...] + gt_ref[...] * _dot(m.astype(BF), wout_ref[...])


def _mix(x, h, of, ob, yf, gate, w, *, tm, tiles_per_batch, per_batch, row_len):
    n = x.shape[0]
    layer = (w["layer"],)
    ms = _mod_spec(tiles_per_batch, per_batch)
    tile = lambda width: pl.BlockSpec((tm, width), lambda i: (i, 0))
    k1_major = yf.ndim == 4
    pre_specs, pre_args = [], []
    if k1_major:
        k2_tile = tm // FFT_N1
        assert tm % FFT_N1 == 0 and k2_tile % 8 == 0
        tpb = tiles_per_batch
        yf_spec = pl.BlockSpec((None, FFT_N1, k2_tile, BW), lambda i: (i // tpb, 0, i % tpb, 0))
        r = np.arange(tm)
        src = (r % FFT_N1) * k2_tile + r // FFT_N1
        perm = np.zeros((tm, tm), np.float32)
        perm[r, src] = 1.0
        pre_specs, pre_args = [_const_spec((tm, tm))], [jnp.asarray(perm, BF)]
    else:
        yf_spec = tile(BW)
    in_specs = pre_specs + [
        tile(D), tile(D), tile(BW), tile(BW), yf_spec, ms,
        _const_spec((D, BW)), _const_spec((1, BW)),
        _const_spec((D, BW)), _const_spec((D, BW)), _const_spec((D, BW)), _const_spec((D, BW)),
        _const_spec((D, BW)),
        _pick_spec(layer, (NGROUP, SGU_CHUNK, SGU_CHUNK)), _const_spec((SGU_CHUNK, BW)),
        _const_spec((3, BW)), _pick_spec(layer, (4, BW, D)), _pick_spec(layer, (4, D, D)),
        _const_spec((4, 1, D)), _pick_spec(layer, (D, D)),
    ]
    return pl.pallas_call(
        functools.partial(_mix_kernel, tm=tm, row_len=row_len, k1_major=k1_major),
        grid=(n // tm,),
        in_specs=in_specs,
        out_specs=tile(D),
        out_shape=jax.ShapeDtypeStruct((n, D), F32),
        scratch_shapes=[pltpu.VMEM((tm, BW), BF)],
        compiler_params=_params(1),
        name="mix",
    )(*pre_args, x, h, of, ob, yf, gate, w["r"], w["go"], w["su"], w["sv"], w["cb"], w["cc"], w["cx"],
      w["sgu"], w["bsgu"], w["conv"], w["branch"], w["gate"], w["bgate"], w["out"])


def _layer_weights(i, w_in, w_gla_a2, b_gla_a2, g_gla_norm, w_sgu, b_sgu, w_conv, w_branch,
                   w_gate, b_gate, w_out):
    edges = np.cumsum([0, HK, HK, BW, BW, 2 * RANK, BW, BW, BW, BW, BW, BW])
    q, k, v, r, a, su, sv, fn, cb, cc, cx = [
        w_in[i, :, edges[j]:edges[j + 1]].astype(BF) for j in range(11)]
    a2 = w_gla_a2[i].astype(BF)
    a2_tok = jnp.zeros((2, LANES, HK), BF)
    a2_tok = a2_tok.at[0, :RANK].set(a2[0]).at[1, RANK:2 * RANK].set(a2[1])
    cc_, sc_ = _dft_cos_sin(GC)
    return {
        "q": q, "kT": k.T, "v": v, "r": r,
        "a": jnp.pad(a, ((0, 0), (0, LANES - 2 * RANK))), "aT": a.T,
        "a2": a2_tok, "a2T": jnp.transpose(a2_tok[:, :2 * RANK, :], (0, 2, 1)),
        "ba2": b_gla_a2[i][:, None, :], "ba2T": b_gla_a2[i][:, :, None],
        "go": g_gla_norm[i].reshape(1, BW),
        "su": su, "sv": sv, "fn": fn, "cb": cb, "cc": cc, "cx": cx,
        "ccs": _bf16_const(np.concatenate([cc_, -sc_], axis=1)),
        "layer": i,
        "sgu": w_sgu.astype(BF), "branch": w_branch.astype(BF), "gate": w_gate.astype(BF),
        "out": w_out.astype(BF),
        "bsgu": jnp.repeat(jnp.transpose(b_sgu[i]), GC, axis=1),
        "conv": w_conv[i],
        "bgate": b_gate[i][:, None, :],
    }


def kernel(x, c, ctx, c_ctx, w_ada, b_ada, g_norm, w_ff1, w_ff3, w_ff2, w_in, w_gla_a2, b_gla_a2,
           g_gla_norm, w_sgu, b_sgu, w_conv, w_branch, w_gate, b_gate, w_out, g_final):
    batch, seq, _ = x.shape
    ctx_len = ctx.shape[1]
    assert seq % TM_LAT == 0 and TM_LAT % GRID_W == 0 and seq == FFT_N1 * FFT_N2
    assert seq % TM_FFN == 0
    assert ctx_len % GLA_C == 0
    tpb = seq // TM_LAT

    cvec = jnp.concatenate([c, c_ctx[None, :]], axis=0)
    cb = jnp.broadcast_to(cvec[:, :, None], (3, D, LANES))
    mods = _adaln(cb, w_ada, b_ada).reshape(DEPTH, 3, N_MOD, 1, D)

    xs = x.reshape(batch * seq, D)
    cs = ctx.reshape(batch * ctx_len, D)
    lat = dict(tm=TM_LAT, tiles_per_batch=tpb, per_batch=True)
    lat_ffn = dict(tm=TM_FFN, tiles_per_batch=seq // TM_FFN, per_batch=True)
    cx_ = dict(tm=ctx_len, tiles_per_batch=1, per_batch=False)
    s_zero = jnp.zeros((batch, 2, 2, 2 * DK, 2 * DV), F32)
    ff = (w_ff1.astype(BF), w_ff3.astype(BF), w_ff2.astype(BF))

    for i in range(DEPTH):
        last = i == DEPTH - 1
        ml = [mods[i, 0:2, j] for j in range(N_MOD)]
        mc = [mods[i, 2:3, j] for j in range(N_MOD)]
        gn = g_norm[i][:, None, :]
        w = _layer_weights(i, w_in, w_gla_a2, b_gla_a2, g_gla_norm, w_sgu, b_sgu, w_conv,
                           w_branch, w_gate, b_gate, w_out)

        xs, hl, *gla_l = _ffn(xs, gn[0], ml[0], ml[1], ml[2], *ff, (i, 0),
                              prep=(gn[1], ml[3], ml[4], w, False), **lat_ffn)
        cs, hc, *gla_c, wrc, wic = _ffn(cs, gn[0], mc[0], mc[1], mc[2], *ff, (i, 0),
                                        prep=(gn[1], mc[3], mc[4], w, True), **cx_)

        ofc, obc, s_ctx = _gla(*gla_c, s_zero, w, tm=ctx_len, tiles_per_batch=1)
        ofl, obl, _ = _gla(*gla_l, s_ctx, w, tm=TM_LAT, tiles_per_batch=tpb)
        yf = _fnet_latent(xs.reshape(batch, seq, D), gn[1], ml[3], ml[4], w, batch)
        xs = _mix(xs, hl, ofl, obl, yf, ml[5], w, row_len=GRID_W, **lat)
        xs = _ffn(xs, gn[2], ml[6], ml[7], ml[8], *ff, (i, 1),
                  g_final=g_final[None, :] if last else None, **lat_ffn)

        if not last:
            yfc = _fft_direct(wrc, wic, batch, ctx_len)
            cs = _mix(cs, hc, ofc, obc, yfc, mc[5], w, row_len=ctx_len, **cx_)
            cs = _ffn(cs, gn[2], mc[6], mc[7], mc[8], *ff, (i, 1), **cx_)
    return xs.reshape(batch, seq, D)
```

```python
import functools
import math

import numpy as np
import jax
import jax.numpy as jnp
from jax import lax
from jax.experimental import pallas as pl
from jax.experimental.pallas import tpu as pltpu

D = 1024
DEPTH = 2
GRID_W = 64
N_MOD = 9
D_FF = 2816
BW = 512
HEADS = 4
DV = 128
DK = 64
HK = HEADS * DK
RANK = 16
GLA_NORMALIZER = 16.0
SGU_CHUNK = 128
GC = 128
NGROUP = 4
EPS = 1e-6
LANES = 128

TM_LAT = 512
TM_FFN = 1024
FF_CHUNK = 256
GLA_C = 128
GLA_SUB = 32
GLA_NSUB = GLA_C // GLA_SUB
FFT_N1 = 64
FFT_N2 = 128
FFT_SUB = 8
FFT_K1_BLOCK = 8
ADA_TN = 1152
VMEM_LIMIT = 56 * 1024 * 1024

BF = jnp.bfloat16
F32 = jnp.float32

_NT = (((1,), (1,)), ((), ()))


def _dot(a, b):
    return jnp.dot(a, b, preferred_element_type=F32)


def _dot_nt(a, b):
    return lax.dot_general(a, b, _NT, preferred_element_type=F32)


def _sigmoid(x):
    return 1.0 / (1.0 + jnp.exp(-x))


def _silu(x):
    return x * _sigmoid(x)


def _log_sigmoid(x):
    return jnp.minimum(x, 0.0) - jnp.log(1.0 + jnp.exp(-jnp.abs(x)))


def _bf16_const(a):
    return jnp.asarray(a, F32).astype(BF)


def _const_spec(shape):
    nd = len(shape)
    return pl.BlockSpec(shape, lambda *_: (0,) * nd, pipeline_mode=pl.Buffered(1))


def _pick_spec(lead, shape):
    nd = len(shape)
    return pl.BlockSpec((None,) * len(lead) + tuple(shape), lambda *_: tuple(lead) + (0,) * nd,
                        pipeline_mode=pl.Buffered(1))


def _cast_job(src, lead, rows, cols):
    return (src, tuple(lead), rows, cols)


def _with_casts(kernel_fn, n_in, n_out, jobs, n_steps):
    bf16_rows = 16
    chunks = [max(k for k in range(1, n_steps + 1) if rows % (k * bf16_rows) == 0)
              for _, _, rows, _ in jobs]
    in_specs, args, out_specs, out_shape = [], [], [], []
    for (src, lead, rows, cols), ch in zip(jobs, chunks):
        r = rows // ch
        in_specs.append(pl.BlockSpec(
            (None,) * len(lead) + (r, cols),
            lambda i, lead=lead, ch=ch: lead + (jnp.minimum(i, ch - 1), 0)))
        out_specs.append(pl.BlockSpec((r, cols), lambda i, ch=ch: (jnp.minimum(i, ch - 1), 0)))
        args.append(src)
        out_shape.append(jax.ShapeDtypeStruct((rows, cols), BF))
    k = len(jobs)

    def kernel(*refs):
        ins, cast_in = refs[:n_in], refs[n_in:n_in + k]
        outs = refs[n_in + k:n_in + k + n_out]
        cast_out = refs[n_in + k + n_out:n_in + 2 * k + n_out]
        kernel_fn(*ins, *outs, *refs[n_in + 2 * k + n_out:])
        step = pl.program_id(0)
        for src_ref, dst_ref, ch in zip(cast_in, cast_out, chunks):
            if ch == n_steps:
                dst_ref[...] = src_ref[...].astype(BF)
            else:
                @pl.when(step < ch)
                def _(src_ref=src_ref, dst_ref=dst_ref):
                    dst_ref[...] = src_ref[...].astype(BF)

    return kernel, in_specs, args, out_specs, out_shape


def _params(n_grid):
    return pltpu.CompilerParams(dimension_semantics=("arbitrary",) * n_grid,
                                vmem_limit_bytes=VMEM_LIMIT)


def _norm_mod(x, g, shift, scale):
    hn = x * lax.rsqrt(jnp.mean(x * x, axis=-1, keepdims=True) + EPS) * g
    return hn * (1.0 + scale) + shift


def _adaln_kernel(cb_ref, w_ref, b_ref, o_ref):
    for r in range(3):
        cv = cb_ref[r]
        s = _silu(cv)
        for j in range(ADA_TN // LANES):
            w = w_ref[:, j * LANES:(j + 1) * LANES]
            o_ref[r:r + 1, j * LANES:(j + 1) * LANES] = (
                jnp.sum(w * s, axis=0, keepdims=True) + b_ref[:, j * LANES:(j + 1) * LANES])


def _adaln(cb, w_ada, b_ada):
    nmod = N_MOD * D
    return pl.pallas_call(
        _adaln_kernel,
        grid=(DEPTH, nmod // ADA_TN),
        in_specs=[
            pl.BlockSpec((3, D, LANES), lambda l, j: (0, 0, 0)),
            pl.BlockSpec((None, D, ADA_TN), lambda l, j: (l, 0, j)),
            pl.BlockSpec((None, 1, ADA_TN), lambda l, j: (l, 0, j)),
        ],
        out_specs=pl.BlockSpec((None, 3, ADA_TN), lambda l, j: (l, 0, j)),
        out_shape=jax.ShapeDtypeStruct((DEPTH, 3, nmod), F32),
        compiler_params=_params(2),
        name="adaln",
    )(cb, w_ada, b_ada.reshape(DEPTH, 1, nmod))


def _ffn_kernel(*refs, tail):
    s_ref, g_ref, sh_ref, sc_ref, gt_ref, w1_ref, w3_ref, w2_ref = refs[:8]
    rest = refs[8:]
    s = s_ref[...]
    h = _norm_mod(s, g_ref[...], sh_ref[...], sc_ref[...]).astype(BF)
    acc = jnp.zeros(s.shape, F32)
    for j in range(D_FF // FF_CHUNK):
        cols = slice(j * FF_CHUNK, (j + 1) * FF_CHUNK)
        a = _dot(h, w1_ref[:, cols])
        b = _dot(h, w3_ref[:, cols])
        u = (_silu(a) * b).astype(BF)
        acc = acc + _dot(u, w2_ref[cols, :])
    out = s + 0.5 * gt_ref[...] * acc
    if tail is None:
        rest[0][...] = out
    elif tail == "final":
        gf_ref, o_ref = rest
        o_ref[...] = out * lax.rsqrt(jnp.mean(out * out, axis=-1, keepdims=True) + EPS) * gf_ref[...]
    else:
        (g2_ref, sh2_ref, sc2_ref, wq_ref, wkT_ref, wv_ref, wa_ref, waT_ref) = rest[:8]
        rest = rest[8:]
        if tail == "prep_fnet":
            wfn_ref, ccs_ref = rest[:2]
            rest = rest[2:]
        o_ref, h_ref, q_ref, kT_ref, v_ref, a_ref, aT_ref = rest[:7]
        o_ref[...] = out
        h2 = _norm_mod(out, g2_ref[...], sh2_ref[...], sc2_ref[...]).astype(BF)
        h_ref[...] = h2
        q_ref[...] = (_dot(h2, wq_ref[...]) * (DK ** -0.5)).astype(BF)
        kT_ref[...] = _dot_nt(wkT_ref[...], h2).astype(BF)
        v_ref[...] = _dot(h2, wv_ref[...]).astype(BF)
        a_ref[...] = _dot(h2, wa_ref[...]).astype(BF)
        aT_ref[...] = _dot_nt(waT_ref[...], h2).astype(BF)
        if tail == "prep_fnet":
            wr_ref, wi_ref = rest[7:]
            fn = _dot(h2, wfn_ref[...]).astype(BF)
            for g in range(NGROUP):
                pq = _dot(fn[:, g * GC:(g + 1) * GC], ccs_ref[...])
                wr_ref[:, g * GC:(g + 1) * GC] = pq[:, :GC]
                wi_ref[:, g * GC:(g + 1) * GC] = pq[:, GC:]


def _mod_spec(tiles_per_batch, per_batch):
    if per_batch:
        return pl.BlockSpec((None, 1, D), lambda i: (i // tiles_per_batch, 0, 0))
    return pl.BlockSpec((None, 1, D), lambda i: (0, 0, 0))


def _ffn(s, g, shift, scale, gate, w1, w3, w2, *, tm, tiles_per_batch, per_batch,
         g_final=None, prep=None, casts=()):
    n = s.shape[0]
    ms = _mod_spec(tiles_per_batch, per_batch)
    rows = lambda width: pl.BlockSpec((tm, width), lambda i: (i, 0))
    cols = lambda height: pl.BlockSpec((height, tm), lambda i: (0, i))
    in_specs = [
        rows(D), _const_spec((1, D)), ms, ms, ms,
        _const_spec((D, D_FF)), _const_spec((D, D_FF)), _const_spec((D_FF, D)),
    ]
    args = [s, g, shift, scale, gate, w1, w3, w2]
    out_specs = [rows(D)]
    out_shape = [jax.ShapeDtypeStruct((n, D), F32)]
    tail = None
    if g_final is not None:
        tail = "final"
        in_specs.append(_const_spec((1, D)))
        args.append(g_final)
    elif prep is not None:
        g2, shift2, scale2, w, with_fnet = prep
        tail = "prep_fnet" if with_fnet else "prep"
        in_specs += [_const_spec((1, D)), ms, ms,
                     _const_spec((D, HK)), _const_spec((HK, D)), _const_spec((D, BW)),
                     _const_spec((D, LANES)), _const_spec((2 * RANK, D))]
        args += [g2, shift2, scale2, w["q"], w["kT"], w["v"], w["a"], w["aT"]]
        out_specs += [rows(D), rows(HK), cols(HK), rows(BW), rows(LANES), cols(2 * RANK)]
        out_shape += [
            jax.ShapeDtypeStruct((n, D), BF), jax.ShapeDtypeStruct((n, HK), BF),
            jax.ShapeDtypeStruct((HK, n), BF), jax.ShapeDtypeStruct((n, BW), BF),
            jax.ShapeDtypeStruct((n, LANES), BF), jax.ShapeDtypeStruct((2 * RANK, n), BF),
        ]
        if with_fnet:
            in_specs += [_const_spec((D, BW)), _const_spec((GC, 2 * GC))]
            args += [w["fn"], w["ccs"]]
            out_specs += [rows(BW), rows(BW)]
            out_shape += [jax.ShapeDtypeStruct((n, BW), F32)] * 2
    body = functools.partial(_ffn_kernel, tail=tail)
    n_main = len(out_shape)
    if casts:
        body, c_in, c_args, c_out, c_shape = _with_casts(body, len(args), n_main, casts, n // tm)
        in_specs, args = in_specs + c_in, args + c_args
        out_specs, out_shape = out_specs + c_out, out_shape + c_shape
    res = pl.pallas_call(
        body,
        grid=(n // tm,),
        in_specs=in_specs,
        out_specs=out_specs,
        out_shape=out_shape,
        compiler_params=_params(1),
        name="ffn" if tail is None else "ffn_" + tail,
    )(*args)
    main = list(res[:n_main]) if prep is not None else [res[0]]
    if casts:
        main.append(list(res[n_main:]))
    return main if len(main) > 1 else main[0]


def _bd_mask():
    r = lax.broadcasted_iota(jnp.int32, (2 * DK, 2 * DV), 0) // DK
    c = lax.broadcasted_iota(jnp.int32, (2 * DK, 2 * DV), 1) // DV
    return r == c


def _gla_consts():
    c = GLA_C
    t = np.arange(c)
    sub = t // GLA_SUB
    lc = np.stack([t[None, :] <= t[:, None], t[None, :] >= t[:, None]]).astype(np.float32)
    tt, jj = t[:, None], t[None, :]
    sub_end = (sub * GLA_SUB + GLA_SUB - 1)[None, :]
    sub_start = (sub * GLA_SUB)[None, :]
    mk_f = (tt > jj) & (tt <= sub_end)
    mk_b = (tt < jj) & (tt >= sub_start)
    me_f = tt > jj
    me_b = tt < jj
    ones = np.ones((c, c), bool)
    mcat = np.stack([np.concatenate([mk_f, me_f, ones], axis=1),
                     np.concatenate([mk_b, me_b, ones], axis=1)]).astype(np.float32)
    return jnp.asarray(lc, BF), jnp.asarray(mcat, BF)


def _gla_direction(dr, q_ref, kT_ref, v_ref, a_ref, aT_ref, wa2_ref, wa2T_ref, ba2_ref, ba2T_ref,
                   lc_ref, mcat_ref, state_ref, o_ref, nchunk):
    c = GLA_C
    inv = 1.0 / GLA_NORMALIZER
    g = (_log_sigmoid(_dot(a_ref[...], wa2_ref[dr]) + ba2_ref[dr]) * inv).astype(BF)
    gT = (_log_sigmoid(_dot(wa2T_ref[dr], aT_ref[...]) + ba2T_ref[dr]) * inv).astype(BF)
    col_sub = lax.broadcasted_iota(jnp.int32, (2 * DK, c), 1) // GLA_SUB
    ii = lax.broadcasted_iota(jnp.int32, (c, 2 * c), 0)
    jj = lax.broadcasted_iota(jnp.int32, (c, 2 * c), 1) % c
    tri = (jj <= ii) if dr == 0 else (jj >= ii)
    bd = _bd_mask()
    zero_kd = jnp.zeros((DK, c), BF)
    zero_v = jnp.zeros((c, DV), BF)
    state = [state_ref[dr, 0], state_ref[dr, 1]]
    order = range(nchunk) if dr == 0 else range(nchunk - 1, -1, -1)
    for ci in order:
        rows = slice(ci * c, (ci + 1) * c)
        b = _dot(lc_ref[dr], g[rows])
        fm = _dot(gT[:, rows], mcat_ref[dr])
        kc = kT_ref[:, rows].astype(F32)
        ksub = (kc * jnp.exp(fm[:, 0:c])).astype(BF)
        kend = (kc * jnp.exp(fm[:, c:2 * c])).astype(BF)
        dec = jnp.exp(fm[:, 2 * c:3 * c])
        qc = q_ref[rows, :].astype(F32)
        qb = (qc * jnp.exp(b)).astype(BF)
        for p in range(2):
            lanes = slice(p * LANES, (p + 1) * LANES)
            qp = qc[:, lanes]
            bp = b[:, lanes]
            kp = ksub[lanes]
            qblocks = []
            kblocks = []
            for j in range(GLA_NSUB):
                if dr == 0:
                    ref_row, lo, hi = j * GLA_SUB + GLA_SUB - 1, j * GLA_SUB, c
                else:
                    ref_row, lo, hi = j * GLA_SUB, 0, (j + 1) * GLA_SUB
                live = (qp[lo:hi] * jnp.exp(bp[lo:hi] - bp[ref_row:ref_row + 1, :])).astype(BF)
                parts = []
                if lo > 0:
                    parts.append(jnp.zeros((lo, LANES), BF))
                parts.append(live)
                if hi < c:
                    parts.append(jnp.zeros((c - hi, LANES), BF))
                qblocks.append(jnp.concatenate(parts, axis=0) if len(parts) > 1 else live)
                kj = jnp.where(col_sub == j, kp, jnp.zeros_like(kp))
                kblocks.append(jnp.concatenate([
                    jnp.concatenate([kj[:DK], zero_kd], axis=1),
                    jnp.concatenate([zero_kd, kj[DK:]], axis=1)], axis=0))
            qhat = jnp.concatenate(qblocks, axis=1)
            khat = jnp.concatenate(kblocks, axis=0)
            att = jnp.where(tri, _dot(qhat, khat), 0.0).astype(BF)
            vp = v_ref[rows, p * 2 * DV:(p + 1) * 2 * DV]
            vbd = jnp.concatenate([
                jnp.concatenate([vp[:, :DV], zero_v], axis=1),
                jnp.concatenate([zero_v, vp[:, DV:]], axis=1)], axis=0)
            o_p = _dot(att, vbd) + _dot(qb[:, lanes], state[p].astype(BF))
            o_ref[rows, p * 2 * DV:(p + 1) * 2 * DV] = o_p.astype(BF)
            upd = _dot(kend[lanes], vp)
            dp = dec[lanes]
            state[p] = jnp.concatenate([dp, dp], axis=1) * state[p] + jnp.where(bd, upd, 0.0)
    state_ref[dr, 0] = state[0]
    state_ref[dr, 1] = state[1]


def _gla_kernel(qf_ref, kTf_ref, vf_ref, af_ref, aTf_ref, qb_ref, kTb_ref, vb_ref, ab_ref, aTb_ref,
                s0_ref, wa2_ref, wa2T_ref, ba2_ref, ba2T_ref, lc_ref, mcat_ref,
                of_ref, ob_ref, sfin_ref, state_ref, *, tm, tiles_per_batch):
    @pl.when(pl.program_id(0) % tiles_per_batch == 0)
    def _():
        state_ref[...] = s0_ref[...]

    shared = (wa2_ref, wa2T_ref, ba2_ref, ba2T_ref, lc_ref, mcat_ref, state_ref)
    _gla_direction(0, qf_ref, kTf_ref, vf_ref, af_ref, aTf_ref, *shared, of_ref, tm // GLA_C)
    _gla_direction(1, qb_ref, kTb_ref, vb_ref, ab_ref, aTb_ref, *shared, ob_ref, tm // GLA_C)
    sfin_ref[...] = state_ref[...]


def _gla(q, kT, v, a, aT, s0, w, *, tm, tiles_per_batch, casts=()):
    n = q.shape[0]
    nt = n // tm
    batch = nt // tiles_per_batch
    lc, mcat = _gla_consts()
    tpb = tiles_per_batch
    fwd = lambda i: i
    bwd = lambda i: (i // tpb) * tpb + (tpb - 1 - i % tpb)

    def tile_specs(idx):
        return [
            pl.BlockSpec((tm, HK), lambda i: (idx(i), 0)),
            pl.BlockSpec((HK, tm), lambda i: (0, idx(i))),
            pl.BlockSpec((tm, BW), lambda i: (idx(i), 0)),
            pl.BlockSpec((tm, LANES), lambda i: (idx(i), 0)),
            pl.BlockSpec((2 * RANK, tm), lambda i: (0, idx(i))),
        ]

    state_block = (None, 2, 2, 2 * DK, 2 * DV)
    in_specs = tile_specs(fwd) + tile_specs(bwd) + [
        pl.BlockSpec(state_block, lambda i: (i // tpb, 0, 0, 0, 0)),
        _const_spec((2, LANES, HK)), _const_spec((2, HK, 2 * RANK)),
        _const_spec((2, 1, HK)), _const_spec((2, HK, 1)),
        _const_spec((2, GLA_C, GLA_C)), _const_spec((2, GLA_C, 3 * GLA_C)),
    ]
    out_specs = [
        pl.BlockSpec((tm, BW), lambda i: (fwd(i), 0)),
        pl.BlockSpec((tm, BW), lambda i: (bwd(i), 0)),
        pl.BlockSpec(state_block, lambda i: (i // tpb, 0, 0, 0, 0)),
    ]
    out_shape = [
        jax.ShapeDtypeStruct((n, BW), BF), jax.ShapeDtypeStruct((n, BW), BF),
        jax.ShapeDtypeStruct((batch, 2, 2, 2 * DK, 2 * DV), F32),
    ]
    tiles = (q, kT, v, a, aT)
    args = [*tiles, *tiles, s0, w["a2"], w["a2T"], w["ba2"], w["ba2T"], lc, mcat]
    body = functools.partial(_gla_kernel, tm=tm, tiles_per_batch=tpb)
    if casts:
        body, c_in, c_args, c_out, c_shape = _with_casts(body, len(args), 3, casts, nt)
        in_specs, args = in_specs + c_in, args + c_args
        out_specs, out_shape = out_specs + c_out, out_shape + c_shape
    res = pl.pallas_call(
        body,
        grid=(nt,),
        in_specs=in_specs,
        out_specs=out_specs,
        out_shape=out_shape,
        scratch_shapes=[pltpu.VMEM((2, 2, 2 * DK, 2 * DV), F32)],
        compiler_params=_params(1),
        name="gla",
    )(*args)
    return (*res[:3], list(res[3:])) if casts else res


def _dft_cos_sin(n):
    k = np.arange(n)
    ang = 2.0 * np.pi * ((k[:, None] * k[None, :]) % n) / n
    return np.cos(ang), np.sin(ang)


def _fnet1_kernel(x_ref, g_ref, sh_ref, sc_ref, wfn_ref, ccs_ref, f_ref, tc_ref, ts_ref,
                  zr_ref, zi_ref):
    rows = FFT_N1 * FFT_SUB
    x = x_ref[...].reshape(rows, D)
    h = _norm_mod(x, g_ref[...], sh_ref[...], sc_ref[...]).astype(BF)
    fn = _dot(h, wfn_ref[...]).astype(BF)
    re, im = [], []
    for g in range(NGROUP):
        pq = _dot(fn[:, g * GC:(g + 1) * GC], ccs_ref[...])
        re.append(pq[:, :GC].astype(BF))
        im.append(pq[:, GC:].astype(BF))
    w = jnp.concatenate([jnp.concatenate(re, axis=1), jnp.concatenate(im, axis=1)], axis=0)
    y = _dot(f_ref[...], w)
    yr, yi = y[:rows], y[rows:]
    tc = jnp.concatenate([tc_ref[...]] * (BW // LANES), axis=1)
    ts = jnp.concatenate([ts_ref[...]] * (BW // LANES), axis=1)
    zr_ref[...] = (yr * tc + yi * ts).reshape(FFT_N1, FFT_SUB, BW)
    zi_ref[...] = (yi * tc - yr * ts).reshape(FFT_N1, FFT_SUB, BW)


def _fft2_kernel(zr_ref, zi_ref, f_ref, o_ref):
    for k in range(zr_ref.shape[0]):
        z = jnp.concatenate([zr_ref[k].astype(BF), zi_ref[k].astype(BF)], axis=0)
        o_ref[k] = _dot(f_ref[...], z)


def _fnet_latent(x, g, shift, scale, w, batch):
    n1, n2, sub = FFT_N1, FFT_N2, FFT_SUB
    rows = n1 * sub
    c1, s1 = _dft_cos_sin(n1)
    eye = np.eye(sub)
    f1 = np.block([[np.kron(c1, eye), np.kron(s1, eye)],
                   [np.kron(-s1, eye), np.kron(c1, eye)]]) / math.sqrt(n1)
    k1 = np.arange(n1)[None, :, None]
    m2 = np.arange(n2).reshape(n2 // sub, 1, sub)
    ang = (2.0 * np.pi * (k1 * m2) / (n1 * n2)).reshape(n2 // sub, rows, 1)
    tc = np.broadcast_to(np.cos(ang), (n2 // sub, rows, LANES)).astype(np.float32)
    ts = np.broadcast_to(np.sin(ang), (n2 // sub, rows, LANES)).astype(np.float32)
    mod = pl.BlockSpec((None, 1, D), lambda b, j: (b, 0, 0))
    zblk = pl.BlockSpec((None, n1, sub, BW), lambda b, j: (b, 0, j, 0))
    tw = pl.BlockSpec((None, rows, LANES), lambda b, j: (j, 0, 0))
    zr, zi = pl.pallas_call(
        _fnet1_kernel,
        grid=(batch, n2 // sub),
        in_specs=[pl.BlockSpec((None, n1, sub, D), lambda b, j: (b, 0, j, 0)),
                  _const_spec((1, D)), mod, mod, _const_spec((D, BW)), _const_spec((GC, 2 * GC)),
                  _const_spec((2 * rows, 2 * rows)), tw, tw],
        out_specs=[zblk, zblk],
        out_shape=[jax.ShapeDtypeStruct((batch, n1, n2, BW), F32)] * 2,
        compiler_params=_params(2),
        name="fnet1",
    )(x.reshape(batch, n1, n2, D), g, shift, scale, w["fn"], w["ccs"], _bf16_const(f1),
      jnp.asarray(tc), jnp.asarray(ts))
    c2, s2 = _dft_cos_sin(n2)
    f2 = np.concatenate([c2, s2], axis=1) / math.sqrt(n2 * GC)
    kblk = pl.BlockSpec((None, FFT_K1_BLOCK, n2, BW), lambda b, k: (b, k, 0, 0))
    return pl.pallas_call(
        _fft2_kernel,
        grid=(batch, n1 // FFT_K1_BLOCK),
        in_specs=[kblk, kblk, _const_spec((n2, 2 * n2))],
        out_specs=kblk,
        out_shape=jax.ShapeDtypeStruct((batch, n1, n2, BW), F32),
        compiler_params=_params(2),
        name="fft2",
    )(zr, zi, _bf16_const(f2))


def _fft_direct(wr, wi, batch, seq):
    c2, s2 = _dft_cos_sin(seq)
    f2 = np.concatenate([c2, s2], axis=1) / math.sqrt(seq * GC)
    zblk = pl.BlockSpec((1, seq, BW), lambda b: (b, 0, 0))
    y = pl.pallas_call(
        _fft2_kernel,
        grid=(batch,),
        in_specs=[zblk, zblk, _const_spec((seq, 2 * seq))],
        out_specs=zblk,
        out_shape=jax.ShapeDtypeStruct((batch, seq, BW), F32),
        compiler_params=_params(1),
        name="fft_direct",
    )(wr.reshape(batch, seq, BW), wi.reshape(batch, seq, BW), _bf16_const(f2))
    return y.reshape(batch * seq, BW)


def _mix_kernel(*refs, tm, row_len, k1_major):
    if k1_major:
        perm_ref, refs = refs[0], refs[1:]
    (x_ref, h_ref, of_ref, ob_ref, yf_ref, gt_ref, wr_ref, go_ref, wsu_ref, wsv_ref,
     wcb_ref, wcc_ref, wcx_ref, wsgu_ref, bsgu_ref, wconv_ref, wbr_ref, wgate_ref,
     bgate_ref, wout_ref, o_ref, br_ref) = refs
    h = h_ref[...]
    if k1_major:
        yf = _dot(perm_ref[...], yf_ref[...].reshape(tm, BW).astype(BF)).astype(BF)
    else:
        yf = yf_ref[...].astype(BF)

    def gated(k, branch):
        gate = _sigmoid(_dot(h, wgate_ref[k]) + bgate_ref[k])
        return gate * _dot(branch, wbr_ref[k])

    r = _dot(h, wr_ref[...])
    for hd in range(HEADS):
        lanes = slice(hd * DV, (hd + 1) * DV)
        oh = of_ref[:, lanes].astype(F32) + ob_ref[:, lanes].astype(F32)
        on = oh * lax.rsqrt(jnp.mean(oh * oh, axis=-1, keepdims=True) + EPS) * go_ref[:, lanes]
        br_ref[:, lanes] = (on * _silu(r[:, lanes])).astype(BF)
    m = gated(0, br_ref[...])
    su = _dot(h, wsu_ref[...])
    sv = _dot(h, wsv_ref[...])
    for g in range(NGROUP):
        lanes = slice(g * GC, (g + 1) * GC)
        vg = sv[:, lanes]
        vc = vg - jnp.mean(vg, axis=-1, keepdims=True)
        z = (vc * lax.rsqrt(jnp.mean(vc * vc, axis=-1, keepdims=True) + EPS)).astype(BF)
        for n in range(tm // SGU_CHUNK):
            rows = slice(n * SGU_CHUNK, (n + 1) * SGU_CHUNK)
            s = _dot(wsgu_ref[g], z[rows]) + bsgu_ref[:, lanes]
            br_ref[rows, lanes] = (su[rows, lanes] * s).astype(BF)
    m = m + gated(1, br_ref[...])
    m = m + gated(2, yf)
    zc = _dot(h, wcc_ref[...]) * _dot(h, wcx_ref[...])
    pos = lax.broadcasted_iota(jnp.int32, (tm, BW), 0) % row_len
    left = jnp.where(pos == 0, 0.0, pltpu.roll(zc, 1, 0))
    right = jnp.where(pos == row_len - 1, 0.0, pltpu.roll(zc, tm - 1, 0))
    y = left * wconv_ref[0:1, :] + zc * wconv_ref[1:2, :] + right * wconv_ref[2:3, :]
    conv = (_dot(h, wcb_ref[...]) * y).astype(BF)
    m = m + gated(3, conv)
    o_ref[...] = x_ref[...] + gt_ref[...] * _dot(m.astype(BF), wout_ref[...])


def _mix(x, h, of, ob, yf, gate, w, *, tm, tiles_per_batch, per_batch, row_len):
    n = x.shape[0]
    layer = (w["layer"],)
    ms = _mod_spec(tiles_per_batch, per_batch)
    tile = lambda width: pl.BlockSpec((tm, width), lambda i: (i, 0))
    k1_major = yf.ndim == 4
    pre_specs, pre_args = [], []
    if k1_major:
        k2_tile = tm // FFT_N1
        assert tm % FFT_N1 == 0 and k2_tile % 8 == 0
        tpb = tiles_per_batch
        yf_spec = pl.BlockSpec((None, FFT_N1, k2_tile, BW), lambda i: (i // tpb, 0, i % tpb, 0))
        r = np.arange(tm)
        src = (r % FFT_N1) * k2_tile + r // FFT_N1
        perm = np.zeros((tm, tm), np.float32)
        perm[r, src] = 1.0
        pre_specs, pre_args = [_const_spec((tm, tm))], [jnp.asarray(perm, BF)]
    else:
        yf_spec = tile(BW)
    in_specs = pre_specs + [
        tile(D), tile(D), tile(BW), tile(BW), yf_spec, ms,
        _const_spec((D, BW)), _const_spec((1, BW)),
        _const_spec((D, BW)), _const_spec((D, BW)), _const_spec((D, BW)), _const_spec((D, BW)),
        _const_spec((D, BW)),
        _pick_spec(layer, (NGROUP, SGU_CHUNK, SGU_CHUNK)), _const_spec((SGU_CHUNK, BW)),
        _const_spec((3, BW)), _const_spec((4, BW, D)), _const_spec((4, D, D)),
        _const_spec((4, 1, D)), _const_spec((D, D)),
    ]
    return pl.pallas_call(
        functools.partial(_mix_kernel, tm=tm, row_len=row_len, k1_major=k1_major),
        grid=(n // tm,),
        in_specs=in_specs,
        out_specs=tile(D),
        out_shape=jax.ShapeDtypeStruct((n, D), F32),
        scratch_shapes=[pltpu.VMEM((tm, BW), BF)],
        compiler_params=_params(1),
        name="mix",
    )(*pre_args, x, h, of, ob, yf, gate, w["r"], w["go"], w["su"], w["sv"], w["cb"], w["cc"], w["cx"],
      w["sgu"], w["bsgu"], w["conv"], w["branch"], w["gate"], w["bgate"], w["out"])


def _layer_weights(i, w_in, w_gla_a2, b_gla_a2, g_gla_norm, w_sgu, b_sgu, w_conv, w_branch,
                   w_gate, b_gate, w_out):
    edges = np.cumsum([0, HK, HK, BW, BW, 2 * RANK, BW, BW, BW, BW, BW, BW])
    q, k, v, r, a, su, sv, fn, cb, cc, cx = [
        w_in[i, :, edges[j]:edges[j + 1]].astype(BF) for j in range(11)]
    a2 = w_gla_a2[i].astype(BF)
    a2_tok = jnp.zeros((2, LANES, HK), BF)
    a2_tok = a2_tok.at[0, :RANK].set(a2[0]).at[1, RANK:2 * RANK].set(a2[1])
    cc_, sc_ = _dft_cos_sin(GC)
    return {
        "q": q, "kT": k.T, "v": v, "r": r,
        "a": jnp.pad(a, ((0, 0), (0, LANES - 2 * RANK))), "aT": a.T,
        "a2": a2_tok, "a2T": jnp.transpose(a2_tok[:, :2 * RANK, :], (0, 2, 1)),
        "ba2": b_gla_a2[i][:, None, :], "ba2T": b_gla_a2[i][:, :, None],
        "go": g_gla_norm[i].reshape(1, BW),
        "su": su, "sv": sv, "fn": fn, "cb": cb, "cc": cc, "cx": cx,
        "ccs": _bf16_const(np.concatenate([cc_, -sc_], axis=1)),
        "layer": i,
        "sgu": w_sgu.astype(BF),
        "merge_casts": (_cast_job(w_gate.reshape(DEPTH, 4 * D, D), (i,), 4 * D, D),
                        _cast_job(w_branch.reshape(DEPTH, 4 * BW, D), (i,), 4 * BW, D),
                        _cast_job(w_out, (i,), D, D)),
        "bsgu": jnp.repeat(jnp.transpose(b_sgu[i]), GC, axis=1),
        "conv": w_conv[i],
        "bgate": b_gate[i][:, None, :],
    }


def kernel(x, c, ctx, c_ctx, w_ada, b_ada, g_norm, w_ff1, w_ff3, w_ff2, w_in, w_gla_a2, b_gla_a2,
           g_gla_norm, w_sgu, b_sgu, w_conv, w_branch, w_gate, b_gate, w_out, g_final):
    batch, seq, _ = x.shape
    ctx_len = ctx.shape[1]
    assert seq % TM_LAT == 0 and TM_LAT % GRID_W == 0 and seq == FFT_N1 * FFT_N2
    assert seq % TM_FFN == 0
    assert ctx_len % GLA_C == 0
    tpb = seq // TM_LAT

    cvec = jnp.concatenate([c, c_ctx[None, :]], axis=0)
    cb = jnp.broadcast_to(cvec[:, :, None], (3, D, LANES))
    mods = _adaln(cb, w_ada, b_ada).reshape(DEPTH, 3, N_MOD, 1, D)

    xs = x.reshape(batch * seq, D)
    cs = ctx.reshape(batch * ctx_len, D)
    lat = dict(tm=TM_LAT, tiles_per_batch=tpb, per_batch=True)
    lat_ffn = dict(tm=TM_FFN, tiles_per_batch=seq // TM_FFN, per_batch=True)
    cx_ = dict(tm=ctx_len, tiles_per_batch=1, per_batch=False)
    s_zero = jnp.zeros((batch, 2, 2, 2 * DK, 2 * DV), F32)

    def ff_casts(layer, half):
        lead = (layer, half)
        return (_cast_job(w_ff1, lead, D, D_FF), _cast_job(w_ff3, lead, D, D_FF),
                _cast_job(w_ff2, lead, D_FF, D))

    ff = [w_ff1[0, 0].astype(BF), w_ff3[0, 0].astype(BF), w_ff2[0, 0].astype(BF)]

    for i in range(DEPTH):
        last = i == DEPTH - 1
        ml = [mods[i, 0:2, j] for j in range(N_MOD)]
        mc = [mods[i, 2:3, j] for j in range(N_MOD)]
        gn = g_norm[i][:, None, :]
        w = _layer_weights(i, w_in, w_gla_a2, b_gla_a2, g_gla_norm, w_sgu, b_sgu, w_conv,
                           w_branch, w_gate, b_gate, w_out)

        xs, hl, *gla_l, merge_w = _ffn(xs, gn[0], ml[0], ml[1], ml[2], *ff,
                                       prep=(gn[1], ml[3], ml[4], w, False),
                                       casts=w["merge_casts"], **lat_ffn)
        w["gate"] = merge_w[0].reshape(4, D, D)
        w["branch"] = merge_w[1].reshape(4, BW, D)
        w["out"] = merge_w[2]
        cs, hc, *gla_c, wrc, wic = _ffn(cs, gn[0], mc[0], mc[1], mc[2], *ff,
                                        prep=(gn[1], mc[3], mc[4], w, True), **cx_)

        ofc, obc, s_ctx = _gla(*gla_c, s_zero, w, tm=ctx_len, tiles_per_batch=1)
        ofl, obl, _, ff = _gla(*gla_l, s_ctx, w, tm=TM_LAT, tiles_per_batch=tpb,
                               casts=ff_casts(i, 1))
        yf = _fnet_latent(xs.reshape(batch, seq, D), gn[1], ml[3], ml[4], w, batch)
        xs = _mix(xs, hl, ofl, obl, yf, ml[5], w, row_len=GRID_W, **lat)
        if last:
            xs = _ffn(xs, gn[2], ml[6], ml[7], ml[8], *ff, g_final=g_final[None, :], **lat_ffn)
        else:
            yfc = _fft_direct(wrc, wic, batch, ctx_len)
            cs = _mix(cs, hc, ofc, obc, yfc, mc[5], w, row_len=ctx_len, **cx_)
            cs = _ffn(cs, gn[2], mc[6], mc[7], mc[8], *ff, **cx_)
            xs, ff = _ffn(xs, gn[2], ml[6], ml[7], ml[8], *ff, casts=ff_casts(i + 1, 0), **lat_ffn)
    return xs.reshape(batch, seq, D)
```

```python
import functools
import math

import numpy as np
import jax
import jax.numpy as jnp
from jax import lax
from jax.experimental import pallas as pl
from jax.experimental.pallas import tpu as pltpu

D = 1024
DEPTH = 2
GRID_W = 64
N_MOD = 9
D_FF = 2816
BW = 512
HEADS = 4
DV = 128
DK = 64
HK = HEADS * DK
RANK = 16
GLA_NORMALIZER = 16.0
SGU_CHUNK = 128
GC = 128
NGROUP = 4
EPS = 1e-6
LANES = 128

TM_LAT = 512
TM_FFN = 1024
FF_CHUNK = 256
GLA_C = 128
GLA_SUB = 32
GLA_NSUB = GLA_C // GLA_SUB
FFT_N1 = 64
FFT_N2 = 128
FFT_SUB = 8
FFT_K1_BLOCK = 8
ADA_TN = 1152
VMEM_LIMIT = 56 * 1024 * 1024

BF = jnp.bfloat16
F32 = jnp.float32

_NT = (((1,), (1,)), ((), ()))


def _dot(a, b):
    return jnp.dot(a, b, preferred_element_type=F32)


def _dot_nt(a, b):
    return lax.dot_general(a, b, _NT, preferred_element_type=F32)


def _sigmoid(x):
    return 1.0 / (1.0 + jnp.exp(-x))


def _silu(x):
    return x * _sigmoid(x)


def _log_sigmoid(x):
    return jnp.minimum(x, 0.0) - jnp.log(1.0 + jnp.exp(-jnp.abs(x)))


def _bf16_const(a):
    return jnp.asarray(a, F32).astype(BF)


def _const_spec(shape):
    nd = len(shape)
    return pl.BlockSpec(shape, lambda *_: (0,) * nd, pipeline_mode=pl.Buffered(1))


def _pick_spec(lead, shape):
    nd = len(shape)
    return pl.BlockSpec((None,) * len(lead) + tuple(shape), lambda *_: tuple(lead) + (0,) * nd,
                        pipeline_mode=pl.Buffered(1))


def _cast_job(src, lead, rows, cols):
    return (src, tuple(lead), rows, cols)


def _with_casts(kernel_fn, n_in, n_out, jobs, n_steps):
    bf16_rows = 16
    chunks = [max(k for k in range(1, n_steps + 1) if rows % (k * bf16_rows) == 0)
              for _, _, rows, _ in jobs]
    in_specs, args, out_specs, out_shape = [], [], [], []
    for (src, lead, rows, cols), ch in zip(jobs, chunks):
        r = rows // ch
        in_specs.append(pl.BlockSpec(
            (None,) * len(lead) + (r, cols),
            lambda i, lead=lead, ch=ch: lead + (jnp.minimum(i, ch - 1), 0)))
        out_specs.append(pl.BlockSpec((r, cols), lambda i, ch=ch: (jnp.minimum(i, ch - 1), 0)))
        args.append(src)
        out_shape.append(jax.ShapeDtypeStruct((rows, cols), BF))
    k = len(jobs)

    def kernel(*refs):
        ins, cast_in = refs[:n_in], refs[n_in:n_in + k]
        outs = refs[n_in + k:n_in + k + n_out]
        cast_out = refs[n_in + k + n_out:n_in + 2 * k + n_out]
        kernel_fn(*ins, *outs, *refs[n_in + 2 * k + n_out:])
        step = pl.program_id(0)
        for src_ref, dst_ref, ch in zip(cast_in, cast_out, chunks):
            if ch == n_steps:
                dst_ref[...] = src_ref[...].astype(BF)
            else:
                @pl.when(step < ch)
                def _(src_ref=src_ref, dst_ref=dst_ref):
                    dst_ref[...] = src_ref[...].astype(BF)

    return kernel, in_specs, args, out_specs, out_shape


def _params(n_grid):
    return pltpu.CompilerParams(dimension_semantics=("arbitrary",) * n_grid,
                                vmem_limit_bytes=VMEM_LIMIT)


def _norm_mod(x, g, shift, scale):
    hn = x * lax.rsqrt(jnp.mean(x * x, axis=-1, keepdims=True) + EPS) * g
    return hn * (1.0 + scale) + shift


def _adaln_kernel(cb_ref, w_ref, b_ref, o_ref):
    for r in range(3):
        cv = cb_ref[r]
        s = _silu(cv)
        for j in range(ADA_TN // LANES):
            w = w_ref[:, j * LANES:(j + 1) * LANES]
            o_ref[r:r + 1, j * LANES:(j + 1) * LANES] = (
                jnp.sum(w * s, axis=0, keepdims=True) + b_ref[:, j * LANES:(j + 1) * LANES])


def _adaln(cb, w_ada, b_ada):
    nmod = N_MOD * D
    return pl.pallas_call(
        _adaln_kernel,
        grid=(DEPTH, nmod // ADA_TN),
        in_specs=[
            pl.BlockSpec((3, D, LANES), lambda l, j: (0, 0, 0)),
            pl.BlockSpec((None, D, ADA_TN), lambda l, j: (l, 0, j)),
            pl.BlockSpec((None, 1, ADA_TN), lambda l, j: (l, 0, j)),
        ],
        out_specs=pl.BlockSpec((None, 3, ADA_TN), lambda l, j: (l, 0, j)),
        out_shape=jax.ShapeDtypeStruct((DEPTH, 3, nmod), F32),
        compiler_params=_params(2),
        name="adaln",
    )(cb, w_ada, b_ada.reshape(DEPTH, 1, nmod))


def _ffn_kernel(*refs, tail):
    s_ref, g_ref, sh_ref, sc_ref, gt_ref, w1_ref, w3_ref, w2_ref = refs[:8]
    rest = refs[8:]
    s = s_ref[...]
    h = _norm_mod(s, g_ref[...], sh_ref[...], sc_ref[...]).astype(BF)
    acc = jnp.zeros(s.shape, F32)
    for j in range(D_FF // FF_CHUNK):
        cols = slice(j * FF_CHUNK, (j + 1) * FF_CHUNK)
        a = _dot(h, w1_ref[:, cols])
        b = _dot(h, w3_ref[:, cols])
        u = (_silu(a) * b).astype(BF)
        acc = acc + _dot(u, w2_ref[cols, :])
    out = s + 0.5 * gt_ref[...] * acc
    if tail is None:
        rest[0][...] = out
    elif tail == "final":
        gf_ref, o_ref = rest
        o_ref[...] = out * lax.rsqrt(jnp.mean(out * out, axis=-1, keepdims=True) + EPS) * gf_ref[...]
    else:
        (g2_ref, sh2_ref, sc2_ref, wq_ref, wkT_ref, wv_ref, wa_ref, waT_ref) = rest[:8]
        rest = rest[8:]
        if tail == "prep_fnet":
            wfn_ref, ccs_ref = rest[:2]
            rest = rest[2:]
        o_ref, h_ref, q_ref, kT_ref, v_ref, a_ref, aT_ref = rest[:7]
        o_ref[...] = out
        h2 = _norm_mod(out, g2_ref[...], sh2_ref[...], sc2_ref[...]).astype(BF)
        h_ref[...] = h2
        q_ref[...] = (_dot(h2, wq_ref[...]) * (DK ** -0.5)).astype(BF)
        kT_ref[...] = _dot_nt(wkT_ref[...], h2).astype(BF)
        v_ref[...] = _dot(h2, wv_ref[...]).astype(BF)
        a_ref[...] = _dot(h2, wa_ref[...]).astype(BF)
        aT_ref[...] = _dot_nt(waT_ref[...], h2).astype(BF)
        if tail == "prep_fnet":
            wr_ref, wi_ref = rest[7:]
            fn = _dot(h2, wfn_ref[...]).astype(BF)
            for g in range(NGROUP):
                pq = _dot(fn[:, g * GC:(g + 1) * GC], ccs_ref[...])
                wr_ref[:, g * GC:(g + 1) * GC] = pq[:, :GC]
                wi_ref[:, g * GC:(g + 1) * GC] = pq[:, GC:]


def _mod_spec(tiles_per_batch, per_batch):
    if per_batch:
        return pl.BlockSpec((None, 1, D), lambda i: (i // tiles_per_batch, 0, 0))
    return pl.BlockSpec((None, 1, D), lambda i: (0, 0, 0))


def _ffn(s, g, shift, scale, gate, w1, w3, w2, *, tm, tiles_per_batch, per_batch,
         g_final=None, prep=None, casts=()):
    n = s.shape[0]
    ms = _mod_spec(tiles_per_batch, per_batch)
    rows = lambda width: pl.BlockSpec((tm, width), lambda i: (i, 0))
    cols = lambda height: pl.BlockSpec((height, tm), lambda i: (0, i))
    in_specs = [
        rows(D), _const_spec((1, D)), ms, ms, ms,
        _const_spec((D, D_FF)), _const_spec((D, D_FF)), _const_spec((D_FF, D)),
    ]
    args = [s, g, shift, scale, gate, w1, w3, w2]
    out_specs = [rows(D)]
    out_shape = [jax.ShapeDtypeStruct((n, D), F32)]
    tail = None
    if g_final is not None:
        tail = "final"
        in_specs.append(_const_spec((1, D)))
        args.append(g_final)
    elif prep is not None:
        g2, shift2, scale2, w, with_fnet = prep
        tail = "prep_fnet" if with_fnet else "prep"
        in_specs += [_const_spec((1, D)), ms, ms,
                     _const_spec((D, HK)), _const_spec((HK, D)), _const_spec((D, BW)),
                     _const_spec((D, LANES)), _const_spec((2 * RANK, D))]
        args += [g2, shift2, scale2, w["q"], w["kT"], w["v"], w["a"], w["aT"]]
        out_specs += [rows(D), rows(HK), cols(HK), rows(BW), rows(LANES), cols(2 * RANK)]
        out_shape += [
            jax.ShapeDtypeStruct((n, D), BF), jax.ShapeDtypeStruct((n, HK), BF),
            jax.ShapeDtypeStruct((HK, n), BF), jax.ShapeDtypeStruct((n, BW), BF),
            jax.ShapeDtypeStruct((n, LANES), BF), jax.ShapeDtypeStruct((2 * RANK, n), BF),
        ]
        if with_fnet:
            in_specs += [_const_spec((D, BW)), _const_spec((GC, 2 * GC))]
            args += [w["fn"], w["ccs"]]
            out_specs += [rows(BW), rows(BW)]
            out_shape += [jax.ShapeDtypeStruct((n, BW), F32)] * 2
    body = functools.partial(_ffn_kernel, tail=tail)
    n_main = len(out_shape)
    if casts:
        body, c_in, c_args, c_out, c_shape = _with_casts(body, len(args), n_main, casts, n // tm)
        in_specs, args = in_specs + c_in, args + c_args
        out_specs, out_shape = out_specs + c_out, out_shape + c_shape
    res = pl.pallas_call(
        body,
        grid=(n // tm,),
        in_specs=in_specs,
        out_specs=out_specs,
        out_shape=out_shape,
        compiler_params=_params(1),
        name="ffn" if tail is None else "ffn_" + tail,
    )(*args)
    main = list(res[:n_main]) if prep is not None else [res[0]]
    if casts:
        main.append(list(res[n_main:]))
    return main if len(main) > 1 else main[0]


def _bd_mask():
    r = lax.broadcasted_iota(jnp.int32, (2 * DK, 2 * DV), 0) // DK
    c = lax.broadcasted_iota(jnp.int32, (2 * DK, 2 * DV), 1) // DV
    return r == c


def _gla_consts():
    c = GLA_C
    t = np.arange(c)
    sub = t // GLA_SUB
    lc = np.stack([t[None, :] <= t[:, None], t[None, :] >= t[:, None]]).astype(np.float32)
    tt, jj = t[:, None], t[None, :]
    sub_end = (sub * GLA_SUB + GLA_SUB - 1)[None, :]
    sub_start = (sub * GLA_SUB)[None, :]
    mk_f = (tt > jj) & (tt <= sub_end)
    mk_b = (tt < jj) & (tt >= sub_start)
    me_f = tt > jj
    me_b = tt < jj
    ones = np.ones((c, c), bool)
    mcat = np.stack([np.concatenate([mk_f, me_f, ones], axis=1),
                     np.concatenate([mk_b, me_b, ones], axis=1)]).astype(np.float32)
    return jnp.asarray(lc, BF), jnp.asarray(mcat, BF)


def _gla_direction(dr, q_ref, kT_ref, v_ref, a_ref, aT_ref, wa2_ref, wa2T_ref, ba2_ref, ba2T_ref,
                   lc_ref, mcat_ref, state_ref, o_ref, nchunk):
    c = GLA_C
    inv = 1.0 / GLA_NORMALIZER
    g = (_log_sigmoid(_dot(a_ref[...], wa2_ref[dr]) + ba2_ref[dr]) * inv).astype(BF)
    gT = (_log_sigmoid(_dot(wa2T_ref[dr], aT_ref[...]) + ba2T_ref[dr]) * inv).astype(BF)
    col_sub = lax.broadcasted_iota(jnp.int32, (2 * DK, c), 1) // GLA_SUB
    ii = lax.broadcasted_iota(jnp.int32, (c, c), 0)
    jj = lax.broadcasted_iota(jnp.int32, (c, c), 1)
    tri = (jj <= ii) if dr == 0 else (jj >= ii)
    low_half = lax.broadcasted_iota(jnp.int32, (c, LANES), 1) < DK
    bd = _bd_mask()
    zero_v = jnp.zeros((c, DV), BF)
    state = [state_ref[dr, 0], state_ref[dr, 1]]
    order = range(nchunk) if dr == 0 else range(nchunk - 1, -1, -1)
    for ci in order:
        rows = slice(ci * c, (ci + 1) * c)
        b = _dot(lc_ref[dr], g[rows])
        fm = _dot(gT[:, rows], mcat_ref[dr])
        kc = kT_ref[:, rows].astype(F32)
        ksub = (kc * jnp.exp(fm[:, 0:c])).astype(BF)
        kend = (kc * jnp.exp(fm[:, c:2 * c])).astype(BF)
        dec = jnp.exp(fm[:, 2 * c:3 * c])
        qc = q_ref[rows, :].astype(F32)
        qb = (qc * jnp.exp(b)).astype(BF)
        for p in range(2):
            lanes = slice(p * LANES, (p + 1) * LANES)
            qp = qc[:, lanes]
            bp = b[:, lanes]
            kp = ksub[lanes]
            qblocks = []
            kblocks = []
            for j in range(GLA_NSUB):
                if dr == 0:
                    ref_row, lo, hi = j * GLA_SUB + GLA_SUB - 1, j * GLA_SUB, c
                else:
                    ref_row, lo, hi = j * GLA_SUB, 0, (j + 1) * GLA_SUB
                live = qp[lo:hi] * jnp.exp(bp[lo:hi] - bp[ref_row:ref_row + 1, :])
                parts = []
                if lo > 0:
                    parts.append(jnp.zeros((lo, LANES), F32))
                parts.append(live)
                if hi < c:
                    parts.append(jnp.zeros((c - hi, LANES), F32))
                qblocks.append(jnp.concatenate(parts, axis=0) if len(parts) > 1 else live)
                kblocks.append(jnp.where(col_sub == j, kp, jnp.zeros_like(kp)))
            heads = []
            for hd in range(2):
                qh = []
                for m in range(GLA_NSUB // 2):
                    even, odd = qblocks[2 * m], qblocks[2 * m + 1]
                    if hd == 0:
                        qh.append(jnp.where(low_half, even, pltpu.roll(odd, DK, 1)))
                    else:
                        qh.append(jnp.where(low_half, pltpu.roll(even, DK, 1), odd))
                qhat = jnp.concatenate(qh, axis=1).astype(BF)
                khat = jnp.concatenate([kj[hd * DK:(hd + 1) * DK] for kj in kblocks], axis=0)
                heads.append(jnp.where(tri, _dot(qhat, khat), 0.0).astype(BF))
            att = jnp.concatenate(heads, axis=1)
            vp = v_ref[rows, p * 2 * DV:(p + 1) * 2 * DV]
            vbd = jnp.concatenate([
                jnp.concatenate([vp[:, :DV], zero_v], axis=1),
                jnp.concatenate([zero_v, vp[:, DV:]], axis=1)], axis=0)
            o_p = _dot(att, vbd) + _dot(qb[:, lanes], state[p].astype(BF))
            o_ref[rows, p * 2 * DV:(p + 1) * 2 * DV] = o_p.astype(BF)
            upd = _dot(kend[lanes], vp)
            dp = dec[lanes]
            state[p] = jnp.concatenate([dp, dp], axis=1) * state[p] + jnp.where(bd, upd, 0.0)
        yield
    state_ref[dr, 0] = state[0]
    state_ref[dr, 1] = state[1]


def _gla_kernel(qf_ref, kTf_ref, vf_ref, af_ref, aTf_ref, qb_ref, kTb_ref, vb_ref, ab_ref, aTb_ref,
                s0_ref, wa2_ref, wa2T_ref, ba2_ref, ba2T_ref, lc_ref, mcat_ref,
                of_ref, ob_ref, sfin_ref, state_ref, *, tm, tiles_per_batch):
    @pl.when(pl.program_id(0) % tiles_per_batch == 0)
    def _():
        state_ref[...] = s0_ref[...]

    shared = (wa2_ref, wa2T_ref, ba2_ref, ba2T_ref, lc_ref, mcat_ref, state_ref)
    scans = [
        _gla_direction(0, qf_ref, kTf_ref, vf_ref, af_ref, aTf_ref, *shared, of_ref, tm // GLA_C),
        _gla_direction(1, qb_ref, kTb_ref, vb_ref, ab_ref, aTb_ref, *shared, ob_ref, tm // GLA_C),
    ]
    for _ in range(tm // GLA_C + 1):
        for scan in scans:
            next(scan, None)
    sfin_ref[...] = state_ref[...]


def _gla(q, kT, v, a, aT, s0, w, *, tm, tiles_per_batch, casts=()):
    n = q.shape[0]
    nt = n // tm
    batch = nt // tiles_per_batch
    lc, mcat = _gla_consts()
    tpb = tiles_per_batch
    fwd = lambda i: i
    bwd = lambda i: (i // tpb) * tpb + (tpb - 1 - i % tpb)

    def tile_specs(idx):
        return [
            pl.BlockSpec((tm, HK), lambda i: (idx(i), 0)),
            pl.BlockSpec((HK, tm), lambda i: (0, idx(i))),
            pl.BlockSpec((tm, BW), lambda i: (idx(i), 0)),
            pl.BlockSpec((tm, LANES), lambda i: (idx(i), 0)),
            pl.BlockSpec((2 * RANK, tm), lambda i: (0, idx(i))),
        ]

    state_block = (None, 2, 2, 2 * DK, 2 * DV)
    in_specs = tile_specs(fwd) + tile_specs(bwd) + [
        pl.BlockSpec(state_block, lambda i: (i // tpb, 0, 0, 0, 0)),
        _const_spec((2, LANES, HK)), _const_spec((2, HK, 2 * RANK)),
        _const_spec((2, 1, HK)), _const_spec((2, HK, 1)),
        _const_spec((2, GLA_C, GLA_C)), _const_spec((2, GLA_C, 3 * GLA_C)),
    ]
    out_specs = [
        pl.BlockSpec((tm, BW), lambda i: (fwd(i), 0)),
        pl.BlockSpec((tm, BW), lambda i: (bwd(i), 0)),
        pl.BlockSpec(state_block, lambda i: (i // tpb, 0, 0, 0, 0)),
    ]
    out_shape = [
        jax.ShapeDtypeStruct((n, BW), BF), jax.ShapeDtypeStruct((n, BW), BF),
        jax.ShapeDtypeStruct((batch, 2, 2, 2 * DK, 2 * DV), F32),
    ]
    tiles = (q, kT, v, a, aT)
    args = [*tiles, *tiles, s0, w["a2"], w["a2T"], w["ba2"], w["ba2T"], lc, mcat]
    body = functools.partial(_gla_kernel, tm=tm, tiles_per_batch=tpb)
    if casts:
        body, c_in, c_args, c_out, c_shape = _with_casts(body, len(args), 3, casts, nt)
        in_specs, args = in_specs + c_in, args + c_args
        out_specs, out_shape = out_specs + c_out, out_shape + c_shape
    res = pl.pallas_call(
        body,
        grid=(nt,),
        in_specs=in_specs,
        out_specs=out_specs,
        out_shape=out_shape,
        scratch_shapes=[pltpu.VMEM((2, 2, 2 * DK, 2 * DV), F32)],
        compiler_params=_params(1),
        name="gla",
    )(*args)
    return (*res[:3], list(res[3:])) if casts else res


def _dft_cos_sin(n):
    k = np.arange(n)
    ang = 2.0 * np.pi * ((k[:, None] * k[None, :]) % n) / n
    return np.cos(ang), np.sin(ang)


def _fnet1_kernel(wr_ref, wi_ref, f_ref, tc_ref, ts_ref, zr_ref, zi_ref):
    rows = FFT_N1 * FFT_SUB
    w = jnp.concatenate([wr_ref[...].reshape(rows, BW).astype(BF),
                         wi_ref[...].reshape(rows, BW).astype(BF)], axis=0)
    y = _dot(f_ref[...], w)
    yr, yi = y[:rows], y[rows:]
    tc = jnp.concatenate([tc_ref[...]] * (BW // LANES), axis=1)
    ts = jnp.concatenate([ts_ref[...]] * (BW // LANES), axis=1)
    zr_ref[...] = (yr * tc + yi * ts).reshape(FFT_N1, FFT_SUB, BW)
    zi_ref[...] = (yi * tc - yr * ts).reshape(FFT_N1, FFT_SUB, BW)


def _fft2_kernel(zr_ref, zi_ref, f_ref, o_ref):
    for k in range(zr_ref.shape[0]):
        z = jnp.concatenate([zr_ref[k].astype(BF), zi_ref[k].astype(BF)], axis=0)
        o_ref[k] = _dot(f_ref[...], z)


def _fnet_latent(wr, wi, batch):
    n1, n2, sub = FFT_N1, FFT_N2, FFT_SUB
    rows = n1 * sub
    c1, s1 = _dft_cos_sin(n1)
    eye = np.eye(sub)
    f1 = np.block([[np.kron(c1, eye), np.kron(s1, eye)],
                   [np.kron(-s1, eye), np.kron(c1, eye)]]) / math.sqrt(n1)
    k1 = np.arange(n1)[None, :, None]
    m2 = np.arange(n2).reshape(n2 // sub, 1, sub)
    ang = (2.0 * np.pi * (k1 * m2) / (n1 * n2)).reshape(n2 // sub, rows, 1)
    tc = np.broadcast_to(np.cos(ang), (n2 // sub, rows, LANES)).astype(np.float32)
    ts = np.broadcast_to(np.sin(ang), (n2 // sub, rows, LANES)).astype(np.float32)
    zblk = pl.BlockSpec((None, n1, sub, BW), lambda b, j: (b, 0, j, 0))
    tw = pl.BlockSpec((None, rows, LANES), lambda b, j: (j, 0, 0))
    zr, zi = pl.pallas_call(
        _fnet1_kernel,
        grid=(batch, n2 // sub),
        in_specs=[zblk, zblk, _const_spec((2 * rows, 2 * rows)), tw, tw],
        out_specs=[zblk, zblk],
        out_shape=[jax.ShapeDtypeStruct((batch, n1, n2, BW), F32)] * 2,
        compiler_params=_params(2),
        name="fnet1",
    )(wr.reshape(batch, n1, n2, BW), wi.reshape(batch, n1, n2, BW), _bf16_const(f1),
      jnp.asarray(tc), jnp.asarray(ts))
    c2, s2 = _dft_cos_sin(n2)
    f2 = np.concatenate([c2, s2], axis=1) / math.sqrt(n2 * GC)
    kblk = pl.BlockSpec((None, FFT_K1_BLOCK, n2, BW), lambda b, k: (b, k, 0, 0))
    return pl.pallas_call(
        _fft2_kernel,
        grid=(batch, n1 // FFT_K1_BLOCK),
        in_specs=[kblk, kblk, _const_spec((n2, 2 * n2))],
        out_specs=kblk,
        out_shape=jax.ShapeDtypeStruct((batch, n1, n2, BW), F32),
        compiler_params=_params(2),
        name="fft2",
    )(zr, zi, _bf16_const(f2))


def _fft_direct(wr, wi, batch, seq):
    c2, s2 = _dft_cos_sin(seq)
    f2 = np.concatenate([c2, s2], axis=1) / math.sqrt(seq * GC)
    zblk = pl.BlockSpec((1, seq, BW), lambda b: (b, 0, 0))
    y = pl.pallas_call(
        _fft2_kernel,
        grid=(batch,),
        in_specs=[zblk, zblk, _const_spec((seq, 2 * seq))],
        out_specs=zblk,
        out_shape=jax.ShapeDtypeStruct((batch, seq, BW), F32),
        compiler_params=_params(1),
        name="fft_direct",
    )(wr.reshape(batch, seq, BW), wi.reshape(batch, seq, BW), _bf16_const(f2))
    return y.reshape(batch * seq, BW)


def _mix_kernel(*refs, tm, row_len, k1_major):
    if k1_major:
        perm_ref, refs = refs[0], refs[1:]
    (x_ref, h_ref, of_ref, ob_ref, yf_ref, gt_ref, wr_ref, go_ref, wsu_ref, wsv_ref,
     wcb_ref, wcc_ref, wcx_ref, wsgu_ref, bsgu_ref, wconv_ref, wbr_ref, wgate_ref,
     bgate_ref, wout_ref, o_ref, br_ref) = refs
    h = h_ref[...]
    if k1_major:
        yf = _dot(perm_ref[...], yf_ref[...].reshape(tm, BW).astype(BF)).astype(BF)
    else:
        yf = yf_ref[...].astype(BF)

    def gated(k, branch):
        gate = _sigmoid(_dot(h, wgate_ref[k]) + bgate_ref[k])
        return gate * _dot(branch, wbr_ref[k])

    r = _dot(h, wr_ref[...])
    for hd in range(HEADS):
        lanes = slice(hd * DV, (hd + 1) * DV)
        oh = of_ref[:, lanes].astype(F32) + ob_ref[:, lanes].astype(F32)
        on = oh * lax.rsqrt(jnp.mean(oh * oh, axis=-1, keepdims=True) + EPS) * go_ref[:, lanes]
        br_ref[:, lanes] = (on * _silu(r[:, lanes])).astype(BF)
    m = gated(0, br_ref[...])
    su = _dot(h, wsu_ref[...])
    sv = _dot(h, wsv_ref[...])
    for g in range(NGROUP):
        lanes = slice(g * GC, (g + 1) * GC)
        vg = sv[:, lanes]
        vc = vg - jnp.mean(vg, axis=-1, keepdims=True)
        z = (vc * lax.rsqrt(jnp.mean(vc * vc, axis=-1, keepdims=True) + EPS)).astype(BF)
        for n in range(tm // SGU_CHUNK):
            rows = slice(n * SGU_CHUNK, (n + 1) * SGU_CHUNK)
            s = _dot(wsgu_ref[g], z[rows]) + bsgu_ref[:, lanes]
            br_ref[rows, lanes] = (su[rows, lanes] * s).astype(BF)
    m = m + gated(1, br_ref[...])
    m = m + gated(2, yf)
    zc = _dot(h, wcc_ref[...]) * _dot(h, wcx_ref[...])
    pos = lax.broadcasted_iota(jnp.int32, (tm, BW), 0) % row_len
    left = jnp.where(pos == 0, 0.0, pltpu.roll(zc, 1, 0))
    right = jnp.where(pos == row_len - 1, 0.0, pltpu.roll(zc, tm - 1, 0))
    y = left * wconv_ref[0:1, :] + zc * wconv_ref[1:2, :] + right * wconv_ref[2:3, :]
    conv = (_dot(h, wcb_ref[...]) * y).astype(BF)
    m = m + gated(3, conv)
    o_ref[...] = x_ref[...] + gt_ref[...] * _dot(m.astype(BF), wout_ref[...])


def _mix(x, h, of, ob, yf, gate, w, *, tm, tiles_per_batch, per_batch, row_len):
    n = x.shape[0]
    layer = (w["layer"],)
    ms = _mod_spec(tiles_per_batch, per_batch)
    tile = lambda width: pl.BlockSpec((tm, width), lambda i: (i, 0))
    k1_major = yf.ndim == 4
    pre_specs, pre_args = [], []
    if k1_major:
        k2_tile = tm // FFT_N1
        assert tm % FFT_N1 == 0 and k2_tile % 8 == 0
        tpb = tiles_per_batch
        yf_spec = pl.BlockSpec((None, FFT_N1, k2_tile, BW), lambda i: (i // tpb, 0, i % tpb, 0))
        r = np.arange(tm)
        src = (r % FFT_N1) * k2_tile + r // FFT_N1
        perm = np.zeros((tm, tm), np.float32)
        perm[r, src] = 1.0
        pre_specs, pre_args = [_const_spec((tm, tm))], [jnp.asarray(perm, BF)]
    else:
        yf_spec = tile(BW)
    in_specs = pre_specs + [
        tile(D), tile(D), tile(BW), tile(BW), yf_spec, ms,
        _const_spec((D, BW)), _const_spec((1, BW)),
        _const_spec((D, BW)), _const_spec((D, BW)), _const_spec((D, BW)), _const_spec((D, BW)),
        _const_spec((D, BW)),
        _pick_spec(layer, (NGROUP, SGU_CHUNK, SGU_CHUNK)), _const_spec((SGU_CHUNK, BW)),
        _const_spec((3, BW)), _const_spec((4, BW, D)), _const_spec((4, D, D)),
        _const_spec((4, 1, D)), _const_spec((D, D)),
    ]
    return pl.pallas_call(
        functools.partial(_mix_kernel, tm=tm, row_len=row_len, k1_major=k1_major),
        grid=(n // tm,),
        in_specs=in_specs,
        out_specs=tile(D),
        out_shape=jax.ShapeDtypeStruct((n, D), F32),
        scratch_shapes=[pltpu.VMEM((tm, BW), BF)],
        compiler_params=_params(1),
        name="mix",
    )(*pre_args, x, h, of, ob, yf, gate, w["r"], w["go"], w["su"], w["sv"], w["cb"], w["cc"], w["cx"],
      w["sgu"], w["bsgu"], w["conv"], w["branch"], w["gate"], w["bgate"], w["out"])


def _layer_weights(i, w_in, w_gla_a2, b_gla_a2, g_gla_norm, w_sgu, b_sgu, w_conv, w_branch,
                   w_gate, b_gate, w_out):
    edges = np.cumsum([0, HK, HK, BW, BW, 2 * RANK, BW, BW, BW, BW, BW, BW])
    q, k, v, r, a, su, sv, fn, cb, cc, cx = [
        w_in[i, :, edges[j]:edges[j + 1]].astype(BF) for j in range(11)]
    a2 = w_gla_a2[i].astype(BF)
    a2_tok = jnp.zeros((2, LANES, HK), BF)
    a2_tok = a2_tok.at[0, :RANK].set(a2[0]).at[1, RANK:2 * RANK].set(a2[1])
    cc_, sc_ = _dft_cos_sin(GC)
    return {
        "q": q, "kT": k.T, "v": v, "r": r,
        "a": jnp.pad(a, ((0, 0), (0, LANES - 2 * RANK))), "aT": a.T,
        "a2": a2_tok, "a2T": jnp.transpose(a2_tok[:, :2 * RANK, :], (0, 2, 1)),
        "ba2": b_gla_a2[i][:, None, :], "ba2T": b_gla_a2[i][:, :, None],
        "go": g_gla_norm[i].reshape(1, BW),
        "su": su, "sv": sv, "fn": fn, "cb": cb, "cc": cc, "cx": cx,
        "ccs": _bf16_const(np.concatenate([cc_, -sc_], axis=1)),
        "layer": i,
        "sgu": w_sgu.astype(BF),
        "merge_casts": (_cast_job(w_gate.reshape(DEPTH, 4 * D, D), (i,), 4 * D, D),
                        _cast_job(w_branch.reshape(DEPTH, 4 * BW, D), (i,), 4 * BW, D),
                        _cast_job(w_out, (i,), D, D)),
        "bsgu": jnp.repeat(jnp.transpose(b_sgu[i]), GC, axis=1),
        "conv": w_conv[i],
        "bgate": b_gate[i][:, None, :],
    }


def kernel(x, c, ctx, c_ctx, w_ada, b_ada, g_norm, w_ff1, w_ff3, w_ff2, w_in, w_gla_a2, b_gla_a2,
           g_gla_norm, w_sgu, b_sgu, w_conv, w_branch, w_gate, b_gate, w_out, g_final):
    batch, seq, _ = x.shape
    ctx_len = ctx.shape[1]
    assert seq % TM_LAT == 0 and TM_LAT % GRID_W == 0 and seq == FFT_N1 * FFT_N2
    assert seq % TM_FFN == 0
    assert ctx_len % GLA_C == 0
    tpb = seq // TM_LAT

    cvec = jnp.concatenate([c, c_ctx[None, :]], axis=0)
    cb = jnp.broadcast_to(cvec[:, :, None], (3, D, LANES))
    mods = _adaln(cb, w_ada, b_ada).reshape(DEPTH, 3, N_MOD, 1, D)

    xs = x.reshape(batch * seq, D)
    cs = ctx.reshape(batch * ctx_len, D)
    lat = dict(tm=TM_LAT, tiles_per_batch=tpb, per_batch=True)
    lat_ffn = dict(tm=TM_FFN, tiles_per_batch=seq // TM_FFN, per_batch=True)
    cx_ = dict(tm=batch * ctx_len, tiles_per_batch=1, per_batch=False)
    s_zero = jnp.zeros((batch, 2, 2, 2 * DK, 2 * DV), F32)

    def ff_casts(layer, half):
        lead = (layer, half)
        return (_cast_job(w_ff1, lead, D, D_FF), _cast_job(w_ff3, lead, D, D_FF),
                _cast_job(w_ff2, lead, D_FF, D))

    ff = [w_ff1[0, 0].astype(BF), w_ff3[0, 0].astype(BF), w_ff2[0, 0].astype(BF)]

    for i in range(DEPTH):
        last = i == DEPTH - 1
        ml = [mods[i, 0:2, j] for j in range(N_MOD)]
        mc = [mods[i, 2:3, j] for j in range(N_MOD)]
        gn = g_norm[i][:, None, :]
        w = _layer_weights(i, w_in, w_gla_a2, b_gla_a2, g_gla_norm, w_sgu, b_sgu, w_conv,
                           w_branch, w_gate, b_gate, w_out)

        xs, hl, *gla_l, wrl, wil, merge_w = _ffn(xs, gn[0], ml[0], ml[1], ml[2], *ff,
                                                 prep=(gn[1], ml[3], ml[4], w, True),
                                                 casts=w["merge_casts"], **lat)
        w["gate"] = merge_w[0].reshape(4, D, D)
        w["branch"] = merge_w[1].reshape(4, BW, D)
        w["out"] = merge_w[2]
        cs, hc, *gla_c, wrc, wic = _ffn(cs, gn[0], mc[0], mc[1], mc[2], *ff,
                                        prep=(gn[1], mc[3], mc[4], w, True), **cx_)

        ofc, obc, s_ctx = _gla(*gla_c, s_zero, w, tm=ctx_len, tiles_per_batch=1)
        ofl, obl, _, ff = _gla(*gla_l, s_ctx, w, tm=TM_LAT, tiles_per_batch=tpb,
                               casts=ff_casts(i, 1))
        yf = _fnet_latent(wrl, wil, batch)
        xs = _mix(xs, hl, ofl, obl, yf, ml[5], w, row_len=GRID_W, **lat)
        if last:
            xs = _ffn(xs, gn[2], ml[6], ml[7], ml[8], *ff, g_final=g_final[None, :], **lat_ffn)
        else:
            yfc = _fft_direct(wrc, wic, batch, ctx_len)
            cs = _mix(cs, hc, ofc, obc, yfc, mc[5], w, row_len=ctx_len, **cx_)
            cs = _ffn(cs, gn[2], mc[6], mc[7], mc[8], *ff, **cx_)
            xs, ff = _ffn(xs, gn[2], ml[6], ml[7], ml[8], *ff, casts=ff_casts(i + 1, 0), **lat_ffn)
    return xs.reshape(batch, seq, D)
```

```python
import functools
import math

import numpy as np
import jax
import jax.numpy as jnp
from jax import lax
from jax.experimental import pallas as pl
from jax.experimental.pallas import tpu as pltpu

D = 1024
DEPTH = 2
GRID_W = 64
N_MOD = 9
D_FF = 2816
BW = 512
HEADS = 4
DV = 128
DK = 64
HK = HEADS * DK
RANK = 16
GLA_NORMALIZER = 16.0
SGU_CHUNK = 128
GC = 128
NGROUP = 4
EPS = 1e-6
LANES = 128

TM_LAT = 512
TM_FFN = 1024
FF_CHUNK = 256
GLA_C = 128
GLA_SUB = 32
GLA_NSUB = GLA_C // GLA_SUB
FFT_N1 = 64
FFT_N2 = 128
FFT_SUB = 8
FFT_K1_BLOCK = 8
ADA_TN = 1152
VMEM_LIMIT = 56 * 1024 * 1024

BF = jnp.bfloat16
F32 = jnp.float32

_NT = (((1,), (1,)), ((), ()))


def _dot(a, b):
    return jnp.dot(a, b, preferred_element_type=F32)


def _dot_nt(a, b):
    return lax.dot_general(a, b, _NT, preferred_element_type=F32)


def _sigmoid(x):
    return 1.0 / (1.0 + jnp.exp(-x))


def _silu(x):
    return x * _sigmoid(x)


def _log_sigmoid(x):
    return jnp.minimum(x, 0.0) - jnp.log(1.0 + jnp.exp(-jnp.abs(x)))


def _bf16_const(a):
    return jnp.asarray(a, F32).astype(BF)


def _const_spec(shape):
    nd = len(shape)
    return pl.BlockSpec(shape, lambda *_: (0,) * nd, pipeline_mode=pl.Buffered(1))


def _pick_spec(lead, shape):
    nd = len(shape)
    return pl.BlockSpec((None,) * len(lead) + tuple(shape), lambda *_: tuple(lead) + (0,) * nd,
                        pipeline_mode=pl.Buffered(1))


def _cast_job(src, lead, rows, cols):
    return (src, tuple(lead), rows, cols)


def _with_casts(kernel_fn, n_in, n_out, jobs, n_steps):
    bf16_rows = 16
    chunks = [max(k for k in range(1, n_steps + 1) if rows % (k * bf16_rows) == 0)
              for _, _, rows, _ in jobs]
    in_specs, args, out_specs, out_shape = [], [], [], []
    for (src, lead, rows, cols), ch in zip(jobs, chunks):
        r = rows // ch
        in_specs.append(pl.BlockSpec(
            (None,) * len(lead) + (r, cols),
            lambda i, lead=lead, ch=ch: lead + (jnp.minimum(i, ch - 1), 0)))
        out_specs.append(pl.BlockSpec((r, cols), lambda i, ch=ch: (jnp.minimum(i, ch - 1), 0)))
        args.append(src)
        out_shape.append(jax.ShapeDtypeStruct((rows, cols), BF))
    k = len(jobs)

    def kernel(*refs):
        ins, cast_in = refs[:n_in], refs[n_in:n_in + k]
        outs = refs[n_in + k:n_in + k + n_out]
        cast_out = refs[n_in + k + n_out:n_in + 2 * k + n_out]
        kernel_fn(*ins, *outs, *refs[n_in + 2 * k + n_out:])
        step = pl.program_id(0)
        for src_ref, dst_ref, ch in zip(cast_in, cast_out, chunks):
            if ch == n_steps:
                dst_ref[...] = src_ref[...].astype(BF)
            else:
                @pl.when(step < ch)
                def _(src_ref=src_ref, dst_ref=dst_ref):
                    dst_ref[...] = src_ref[...].astype(BF)

    return kernel, in_specs, args, out_specs, out_shape


def _params(n_grid):
    return pltpu.CompilerParams(dimension_semantics=("arbitrary",) * n_grid,
                                vmem_limit_bytes=VMEM_LIMIT)


def _norm_mod(x, g, shift, scale):
    hn = x * lax.rsqrt(jnp.mean(x * x, axis=-1, keepdims=True) + EPS) * g
    return hn * (1.0 + scale) + shift


def _adaln_kernel(cb_ref, w_ref, b_ref, o_ref):
    for r in range(3):
        cv = cb_ref[r]
        s = _silu(cv)
        for j in range(ADA_TN // LANES):
            w = w_ref[:, j * LANES:(j + 1) * LANES]
            o_ref[r:r + 1, j * LANES:(j + 1) * LANES] = (
                jnp.sum(w * s, axis=0, keepdims=True) + b_ref[:, j * LANES:(j + 1) * LANES])


def _adaln(cb, w_ada, b_ada):
    nmod = N_MOD * D
    return pl.pallas_call(
        _adaln_kernel,
        grid=(DEPTH, nmod // ADA_TN),
        in_specs=[
            pl.BlockSpec((3, D, LANES), lambda l, j: (0, 0, 0)),
            pl.BlockSpec((None, D, ADA_TN), lambda l, j: (l, 0, j)),
            pl.BlockSpec((None, 1, ADA_TN), lambda l, j: (l, 0, j)),
        ],
        out_specs=pl.BlockSpec((None, 3, ADA_TN), lambda l, j: (l, 0, j)),
        out_shape=jax.ShapeDtypeStruct((DEPTH, 3, nmod), F32),
        compiler_params=_params(2),
        name="adaln",
    )(cb, w_ada, b_ada.reshape(DEPTH, 1, nmod))


def _ffn_kernel(*refs, tail):
    s_ref, g_ref, sh_ref, sc_ref, gt_ref, w1_ref, w3_ref, w2_ref = refs[:8]
    rest = refs[8:]
    s = s_ref[...]
    h = _norm_mod(s, g_ref[...], sh_ref[...], sc_ref[...]).astype(BF)
    acc = jnp.zeros(s.shape, F32)
    for j in range(D_FF // FF_CHUNK):
        cols = slice(j * FF_CHUNK, (j + 1) * FF_CHUNK)
        a = _dot(h, w1_ref[:, cols])
        b = _dot(h, w3_ref[:, cols])
        u = (_silu(a) * b).astype(BF)
        acc = acc + _dot(u, w2_ref[cols, :])
    out = s + 0.5 * gt_ref[...] * acc
    if tail is None:
        rest[0][...] = out
    elif tail == "final":
        gf_ref, o_ref = rest
        o_ref[...] = out * lax.rsqrt(jnp.mean(out * out, axis=-1, keepdims=True) + EPS) * gf_ref[...]
    else:
        (g2_ref, sh2_ref, sc2_ref, wq_ref, wkT_ref, wv_ref, wa_ref, waT_ref) = rest[:8]
        rest = rest[8:]
        if tail == "prep_fnet":
            wfn_ref, ccs_ref = rest[:2]
            rest = rest[2:]
        o_ref, h_ref, q_ref, kT_ref, v_ref, a_ref, aT_ref = rest[:7]
        o_ref[...] = out
        h2 = _norm_mod(out, g2_ref[...], sh2_ref[...], sc2_ref[...]).astype(BF)
        h_ref[...] = h2
        q_ref[...] = (_dot(h2, wq_ref[...]) * (DK ** -0.5)).astype(BF)
        kT_ref[...] = _dot_nt(wkT_ref[...], h2).astype(BF)
        v_ref[...] = _dot(h2, wv_ref[...]).astype(BF)
        a_ref[...] = _dot(h2, wa_ref[...]).astype(BF)
        aT_ref[...] = _dot_nt(waT_ref[...], h2).astype(BF)
        if tail == "prep_fnet":
            wr_ref, wi_ref = rest[7:]
            fn = _dot(h2, wfn_ref[...]).astype(BF)
            for g in range(NGROUP):
                pq = _dot(fn[:, g * GC:(g + 1) * GC], ccs_ref[...])
                wr_ref[:, g * GC:(g + 1) * GC] = pq[:, :GC]
                wi_ref[:, g * GC:(g + 1) * GC] = pq[:, GC:]


def _mod_spec(tiles_per_batch, per_batch):
    if per_batch:
        return pl.BlockSpec((None, 1, D), lambda i: (i // tiles_per_batch, 0, 0))
    return pl.BlockSpec((None, 1, D), lambda i: (0, 0, 0))


def _ffn(s, g, shift, scale, gate, w1, w3, w2, *, tm, tiles_per_batch, per_batch,
         g_final=None, prep=None, casts=()):
    n = s.shape[0]
    ms = _mod_spec(tiles_per_batch, per_batch)
    rows = lambda width: pl.BlockSpec((tm, width), lambda i: (i, 0))
    cols = lambda height: pl.BlockSpec((height, tm), lambda i: (0, i))
    in_specs = [
        rows(D), _const_spec((1, D)), ms, ms, ms,
        _const_spec((D, D_FF)), _const_spec((D, D_FF)), _const_spec((D_FF, D)),
    ]
    args = [s, g, shift, scale, gate, w1, w3, w2]
    out_specs = [rows(D)]
    out_shape = [jax.ShapeDtypeStruct((n, D), F32)]
    tail = None
    if g_final is not None:
        tail = "final"
        in_specs.append(_const_spec((1, D)))
        args.append(g_final)
    elif prep is not None:
        g2, shift2, scale2, w, with_fnet = prep
        tail = "prep_fnet" if with_fnet else "prep"
        in_specs += [_const_spec((1, D)), ms, ms,
                     _const_spec((D, HK)), _const_spec((HK, D)), _const_spec((D, BW)),
                     _const_spec((D, LANES)), _const_spec((2 * RANK, D))]
        args += [g2, shift2, scale2, w["q"], w["kT"], w["v"], w["a"], w["aT"]]
        out_specs += [rows(D), rows(HK), cols(HK), rows(BW), rows(LANES), cols(2 * RANK)]
        out_shape += [
            jax.ShapeDtypeStruct((n, D), BF), jax.ShapeDtypeStruct((n, HK), BF),
            jax.ShapeDtypeStruct((HK, n), BF), jax.ShapeDtypeStruct((n, BW), BF),
            jax.ShapeDtypeStruct((n, LANES), BF), jax.ShapeDtypeStruct((2 * RANK, n), BF),
        ]
        if with_fnet:
            in_specs += [_const_spec((D, BW)), _const_spec((GC, 2 * GC))]
            args += [w["fn"], w["ccs"]]
            out_specs += [rows(BW), rows(BW)]
            out_shape += [jax.ShapeDtypeStruct((n, BW), F32)] * 2
    body = functools.partial(_ffn_kernel, tail=tail)
    n_main = len(out_shape)
    if casts:
        body, c_in, c_args, c_out, c_shape = _with_casts(body, len(args), n_main, casts, n // tm)
        in_specs, args = in_specs + c_in, args + c_args
        out_specs, out_shape = out_specs + c_out, out_shape + c_shape
    res = pl.pallas_call(
        body,
        grid=(n // tm,),
        in_specs=in_specs,
        out_specs=out_specs,
        out_shape=out_shape,
        compiler_params=_params(1),
        name="ffn" if tail is None else "ffn_" + tail,
    )(*args)
    main = list(res[:n_main]) if prep is not None else [res[0]]
    if casts:
        main.append(list(res[n_main:]))
    return main if len(main) > 1 else main[0]


def _bd_mask():
    r = lax.broadcasted_iota(jnp.int32, (2 * DK, 2 * DV), 0) // DK
    c = lax.broadcasted_iota(jnp.int32, (2 * DK, 2 * DV), 1) // DV
    return r == c


def _gla_consts():
    c = GLA_C
    t = np.arange(c)
    sub = t // GLA_SUB
    lc = np.stack([t[None, :] <= t[:, None], t[None, :] >= t[:, None]]).astype(np.float32)
    tt, jj = t[:, None], t[None, :]
    sub_end = (sub * GLA_SUB + GLA_SUB - 1)[None, :]
    sub_start = (sub * GLA_SUB)[None, :]
    mk_f = (tt > jj) & (tt <= sub_end)
    mk_b = (tt < jj) & (tt >= sub_start)
    me_f = tt > jj
    me_b = tt < jj
    ones = np.ones((c, c), bool)
    mcat = np.stack([np.concatenate([mk_f, me_f, ones], axis=1),
                     np.concatenate([mk_b, me_b, ones], axis=1)]).astype(np.float32)
    return jnp.asarray(lc, BF), jnp.asarray(mcat, BF)


def _gla_direction(dr, q_ref, kT_ref, v_ref, a_ref, aT_ref, wa2_ref, wa2T_ref, ba2_ref, ba2T_ref,
                   lc_ref, mcat_ref, state_ref, o_ref, nchunk):
    c = GLA_C
    inv = 1.0 / GLA_NORMALIZER
    g = (_log_sigmoid(_dot(a_ref[...], wa2_ref[dr]) + ba2_ref[dr]) * inv).astype(BF)
    gT = (_log_sigmoid(_dot(wa2T_ref[dr], aT_ref[...]) + ba2T_ref[dr]) * inv).astype(BF)
    col_sub = lax.broadcasted_iota(jnp.int32, (2 * DK, c), 1) // GLA_SUB
    ii = lax.broadcasted_iota(jnp.int32, (c, c), 0)
    jj = lax.broadcasted_iota(jnp.int32, (c, c), 1)
    tri = (jj <= ii) if dr == 0 else (jj >= ii)
    low_half = lax.broadcasted_iota(jnp.int32, (c, LANES), 1) < DK
    bd = _bd_mask()
    zero_v = jnp.zeros((c, DV), BF)
    state = [state_ref[dr, 0], state_ref[dr, 1]]
    order = range(nchunk) if dr == 0 else range(nchunk - 1, -1, -1)
    sums = []
    for ci in order:
        rows = slice(ci * c, (ci + 1) * c)
        sums.append((_dot(lc_ref[dr], g[rows]),
                     _dot(gT[:, rows], mcat_ref[dr])))
    yield
    operands = []
    for ci, (b, fm) in zip(order, sums):
        rows = slice(ci * c, (ci + 1) * c)
        kc = kT_ref[:, rows].astype(F32)
        ksub = (kc * jnp.exp(fm[:, 0:c])).astype(BF)
        kend = (kc * jnp.exp(fm[:, c:2 * c])).astype(BF)
        dec = jnp.exp(fm[:, 2 * c:3 * c])
        qc = q_ref[rows, :].astype(F32)
        qb = (qc * jnp.exp(b)).astype(BF)
        for p in range(2):
            lanes = slice(p * LANES, (p + 1) * LANES)
            qp = qc[:, lanes]
            bp = b[:, lanes]
            kp = ksub[lanes]
            qblocks = []
            kblocks = []
            for j in range(GLA_NSUB):
                if dr == 0:
                    ref_row, lo, hi = j * GLA_SUB + GLA_SUB - 1, j * GLA_SUB, c
                else:
                    ref_row, lo, hi = j * GLA_SUB, 0, (j + 1) * GLA_SUB
                live = qp[lo:hi] * jnp.exp(bp[lo:hi] - bp[ref_row:ref_row + 1, :])
                parts = []
                if lo > 0:
                    parts.append(jnp.zeros((lo, LANES), F32))
                parts.append(live)
                if hi < c:
                    parts.append(jnp.zeros((c - hi, LANES), F32))
                qblocks.append(jnp.concatenate(parts, axis=0) if len(parts) > 1 else live)
                kblocks.append(jnp.where(col_sub == j, kp, jnp.zeros_like(kp)))
            hats = []
            for hd in range(2):
                qh = []
                for m in range(GLA_NSUB // 2):
                    even, odd = qblocks[2 * m], qblocks[2 * m + 1]
                    if hd == 0:
                        qh.append(jnp.where(low_half, even, pltpu.roll(odd, DK, 1)))
                    else:
                        qh.append(jnp.where(low_half, pltpu.roll(even, DK, 1), odd))
                hats.append((jnp.concatenate(qh, axis=1).astype(BF),
                             jnp.concatenate([kj[hd * DK:(hd + 1) * DK] for kj in kblocks], axis=0)))
            dp = dec[lanes]
            operands.append((ci, p, hats, qb[:, lanes], kend[lanes],
                             jnp.concatenate([dp, dp], axis=1)))
    yield
    scores = []
    for ci, p, hats, qbp, kendp, decay in operands:
        heads = [jnp.where(tri, _dot(qhat, khat), 0.0).astype(BF) for qhat, khat in hats]
        scores.append(jnp.concatenate(heads, axis=1))
    yield
    pending = []
    for (ci, p, hats, qbp, kendp, decay), att in zip(operands, scores):
        vp = v_ref[ci * c:(ci + 1) * c, p * 2 * DV:(p + 1) * 2 * DV]
        vbd = jnp.concatenate([
            jnp.concatenate([vp[:, :DV], zero_v], axis=1),
            jnp.concatenate([zero_v, vp[:, DV:]], axis=1)], axis=0)
        pending.append((ci, p, _dot(att, vbd), qbp, decay, jnp.where(bd, _dot(kendp, vp), 0.0)))
    yield
    for ci, p, intra, qbp, decay, upd in pending:
        o_p = intra + _dot(qbp, state[p].astype(BF))
        o_ref[ci * c:(ci + 1) * c, p * 2 * DV:(p + 1) * 2 * DV] = o_p.astype(BF)
        state[p] = decay * state[p] + upd
    state_ref[dr, 0] = state[0]
    state_ref[dr, 1] = state[1]


def _gla_kernel(qf_ref, kTf_ref, vf_ref, af_ref, aTf_ref, qb_ref, kTb_ref, vb_ref, ab_ref, aTb_ref,
                s0_ref, wa2_ref, wa2T_ref, ba2_ref, ba2T_ref, lc_ref, mcat_ref,
                of_ref, ob_ref, sfin_ref, state_ref, *, tm, tiles_per_batch):
    @pl.when(pl.program_id(0) % tiles_per_batch == 0)
    def _():
        state_ref[...] = s0_ref[...]

    shared = (wa2_ref, wa2T_ref, ba2_ref, ba2T_ref, lc_ref, mcat_ref, state_ref)
    scans = [
        _gla_direction(0, qf_ref, kTf_ref, vf_ref, af_ref, aTf_ref, *shared, of_ref, tm // GLA_C),
        _gla_direction(1, qb_ref, kTb_ref, vb_ref, ab_ref, aTb_ref, *shared, ob_ref, tm // GLA_C),
    ]
    live = list(scans)
    while live:
        live = [scan for scan in live if next(scan, "done") != "done"]
    sfin_ref[...] = state_ref[...]


def _gla(q, kT, v, a, aT, s0, w, *, tm, tiles_per_batch, casts=()):
    n = q.shape[0]
    nt = n // tm
    batch = nt // tiles_per_batch
    lc, mcat = _gla_consts()
    tpb = tiles_per_batch
    fwd = lambda i: i
    bwd = lambda i: (i // tpb) * tpb + (tpb - 1 - i % tpb)

    def tile_specs(idx):
        return [
            pl.BlockSpec((tm, HK), lambda i: (idx(i), 0)),
            pl.BlockSpec((HK, tm), lambda i: (0, idx(i))),
            pl.BlockSpec((tm, BW), lambda i: (idx(i), 0)),
            pl.BlockSpec((tm, LANES), lambda i: (idx(i), 0)),
            pl.BlockSpec((2 * RANK, tm), lambda i: (0, idx(i))),
        ]

    state_block = (None, 2, 2, 2 * DK, 2 * DV)
    in_specs = tile_specs(fwd) + tile_specs(bwd) + [
        pl.BlockSpec(state_block, lambda i: (i // tpb, 0, 0, 0, 0)),
        _const_spec((2, LANES, HK)), _const_spec((2, HK, 2 * RANK)),
        _const_spec((2, 1, HK)), _const_spec((2, HK, 1)),
        _const_spec((2, GLA_C, GLA_C)), _const_spec((2, GLA_C, 3 * GLA_C)),
    ]
    out_specs = [
        pl.BlockSpec((tm, BW), lambda i: (fwd(i), 0)),
        pl.BlockSpec((tm, BW), lambda i: (bwd(i), 0)),
        pl.BlockSpec(state_block, lambda i: (i // tpb, 0, 0, 0, 0)),
    ]
    out_shape = [
        jax.ShapeDtypeStruct((n, BW), BF), jax.ShapeDtypeStruct((n, BW), BF),
        jax.ShapeDtypeStruct((batch, 2, 2, 2 * DK, 2 * DV), F32),
    ]
    tiles = (q, kT, v, a, aT)
    args = [*tiles, *tiles, s0, w["a2"], w["a2T"], w["ba2"], w["ba2T"], lc, mcat]
    body = functools.partial(_gla_kernel, tm=tm, tiles_per_batch=tpb)
    if casts:
        body, c_in, c_args, c_out, c_shape = _with_casts(body, len(args), 3, casts, nt)
        in_specs, args = in_specs + c_in, args + c_args
        out_specs, out_shape = out_specs + c_out, out_shape + c_shape
    res = pl.pallas_call(
        body,
        grid=(nt,),
        in_specs=in_specs,
        out_specs=out_specs,
        out_shape=out_shape,
        scratch_shapes=[pltpu.VMEM((2, 2, 2 * DK, 2 * DV), F32)],
        compiler_params=_params(1),
        name="gla",
    )(*args)
    return (*res[:3], list(res[3:])) if casts else res


def _dft_cos_sin(n):
    k = np.arange(n)
    ang = 2.0 * np.pi * ((k[:, None] * k[None, :]) % n) / n
    return np.cos(ang), np.sin(ang)


def _fnet1_kernel(x_ref, g_ref, sh_ref, sc_ref, wfn_ref, ccs_ref, f_ref, tc_ref, ts_ref,
                  zr_ref, zi_ref):
    rows = FFT_N1 * FFT_SUB
    x = x_ref[...].reshape(rows, D)
    h = _norm_mod(x, g_ref[...], sh_ref[...], sc_ref[...]).astype(BF)
    fn = _dot(h, wfn_ref[...]).astype(BF)
    re, im = [], []
    for g in range(NGROUP):
        pq = _dot(fn[:, g * GC:(g + 1) * GC], ccs_ref[...])
        re.append(pq[:, :GC].astype(BF))
        im.append(pq[:, GC:].astype(BF))
    w = jnp.concatenate([jnp.concatenate(re, axis=1), jnp.concatenate(im, axis=1)], axis=0)
    y = _dot(f_ref[...], w)
    yr, yi = y[:rows], y[rows:]
    tc = jnp.concatenate([tc_ref[...]] * (BW // LANES), axis=1)
    ts = jnp.concatenate([ts_ref[...]] * (BW // LANES), axis=1)
    zr_ref[...] = (yr * tc + yi * ts).reshape(FFT_N1, FFT_SUB, BW)
    zi_ref[...] = (yi * tc - yr * ts).reshape(FFT_N1, FFT_SUB, BW)


def _fft2_kernel(zr_ref, zi_ref, f_ref, o_ref):
    for k in range(zr_ref.shape[0]):
        z = jnp.concatenate([zr_ref[k].astype(BF), zi_ref[k].astype(BF)], axis=0)
        o_ref[k] = _dot(f_ref[...], z)


def _fnet_latent(x, g, shift, scale, w, batch):
    n1, n2, sub = FFT_N1, FFT_N2, FFT_SUB
    rows = n1 * sub
    c1, s1 = _dft_cos_sin(n1)
    eye = np.eye(sub)
    f1 = np.block([[np.kron(c1, eye), np.kron(s1, eye)],
                   [np.kron(-s1, eye), np.kron(c1, eye)]]) / math.sqrt(n1)
    k1 = np.arange(n1)[None, :, None]
    m2 = np.arange(n2).reshape(n2 // sub, 1, sub)
    ang = (2.0 * np.pi * (k1 * m2) / (n1 * n2)).reshape(n2 // sub, rows, 1)
    tc = np.broadcast_to(np.cos(ang), (n2 // sub, rows, LANES)).astype(np.float32)
    ts = np.broadcast_to(np.sin(ang), (n2 // sub, rows, LANES)).astype(np.float32)
    mod = pl.BlockSpec((None, 1, D), lambda b, j: (b, 0, 0))
    zblk = pl.BlockSpec((None, n1, sub, BW), lambda b, j: (b, 0, j, 0))
    tw = pl.BlockSpec((None, rows, LANES), lambda b, j: (j, 0, 0))
    zr, zi = pl.pallas_call(
        _fnet1_kernel,
        grid=(batch, n2 // sub),
        in_specs=[pl.BlockSpec((None, n1, sub, D), lambda b, j: (b, 0, j, 0)),
                  _const_spec((1, D)), mod, mod, _const_spec((D, BW)), _const_spec((GC, 2 * GC)),
                  _const_spec((2 * rows, 2 * rows)), tw, tw],
        out_specs=[zblk, zblk],
        out_shape=[jax.ShapeDtypeStruct((batch, n1, n2, BW), F32)] * 2,
        compiler_params=_params(2),
        name="fnet1",
    )(x.reshape(batch, n1, n2, D), g, shift, scale, w["fn"], w["ccs"], _bf16_const(f1),
      jnp.asarray(tc), jnp.asarray(ts))
    c2, s2 = _dft_cos_sin(n2)
    f2 = np.concatenate([c2, s2], axis=1) / math.sqrt(n2 * GC)
    kblk = pl.BlockSpec((None, FFT_K1_BLOCK, n2, BW), lambda b, k: (b, k, 0, 0))
    return pl.pallas_call(
        _fft2_kernel,
        grid=(batch, n1 // FFT_K1_BLOCK),
        in_specs=[kblk, kblk, _const_spec((n2, 2 * n2))],
        out_specs=kblk,
        out_shape=jax.ShapeDtypeStruct((batch, n1, n2, BW), F32),
        compiler_params=_params(2),
        name="fft2",
    )(zr, zi, _bf16_const(f2))


def _fft_direct(wr, wi, batch, seq):
    c2, s2 = _dft_cos_sin(seq)
    f2 = np.concatenate([c2, s2], axis=1) / math.sqrt(seq * GC)
    zblk = pl.BlockSpec((1, seq, BW), lambda b: (b, 0, 0))
    y = pl.pallas_call(
        _fft2_kernel,
        grid=(batch,),
        in_specs=[zblk, zblk, _const_spec((seq, 2 * seq))],
        out_specs=zblk,
        out_shape=jax.ShapeDtypeStruct((batch, seq, BW), F32),
        compiler_params=_params(1),
        name="fft_direct",
    )(wr.reshape(batch, seq, BW), wi.reshape(batch, seq, BW), _bf16_const(f2))
    return y.reshape(batch * seq, BW)


def _mix_kernel(*refs, tm, row_len, k1_major):
    if k1_major:
        perm_ref, refs = refs[0], refs[1:]
    (x_ref, h_ref, of_ref, ob_ref, yf_ref, gt_ref, wr_ref, go_ref, wsu_ref, wsv_ref,
     wcb_ref, wcc_ref, wcx_ref, wsgu_ref, bsgu_ref, wconv_ref, wbr_ref, wgate_ref,
     bgate_ref, wout_ref, o_ref, br_ref) = refs
    h = h_ref[...]
    if k1_major:
        yf = _dot(perm_ref[...], yf_ref[...].reshape(tm, BW).astype(BF)).astype(BF)
    else:
        yf = yf_ref[...].astype(BF)

    def gated(k, branch):
        gate = _sigmoid(_dot(h, wgate_ref[k]) + bgate_ref[k])
        return gate * _dot(branch, wbr_ref[k])

    r = _dot(h, wr_ref[...])
    for hd in range(HEADS):
        lanes = slice(hd * DV, (hd + 1) * DV)
        oh = of_ref[:, lanes].astype(F32) + ob_ref[:, lanes].astype(F32)
        on = oh * lax.rsqrt(jnp.mean(oh * oh, axis=-1, keepdims=True) + EPS) * go_ref[:, lanes]
        br_ref[:, lanes] = (on * _silu(r[:, lanes])).astype(BF)
    m = gated(0, br_ref[...])
    su = _dot(h, wsu_ref[...])
    sv = _dot(h, wsv_ref[...])
    for g in range(NGROUP):
        lanes = slice(g * GC, (g + 1) * GC)
        vg = sv[:, lanes]
        vc = vg - jnp.mean(vg, axis=-1, keepdims=True)
        z = (vc * lax.rsqrt(jnp.mean(vc * vc, axis=-1, keepdims=True) + EPS)).astype(BF)
        for n in range(tm // SGU_CHUNK):
            rows = slice(n * SGU_CHUNK, (n + 1) * SGU_CHUNK)
            s = _dot(wsgu_ref[g], z[rows]) + bsgu_ref[:, lanes]
            br_ref[rows, lanes] = (su[rows, lanes] * s).astype(BF)
    m = m + gated(1, br_ref[...])
    m = m + gated(2, yf)
    zc = _dot(h, wcc_ref[...]) * _dot(h, wcx_ref[...])
    pos = lax.broadcasted_iota(jnp.int32, (tm, BW), 0) % row_len
    left = jnp.where(pos == 0, 0.0, pltpu.roll(zc, 1, 0))
    right = jnp.where(pos == row_len - 1, 0.0, pltpu.roll(zc, tm - 1, 0))
    y = left * wconv_ref[0:1, :] + zc * wconv_ref[1:2, :] + right * wconv_ref[2:3, :]
    conv = (_dot(h, wcb_ref[...]) * y).astype(BF)
    m = m + gated(3, conv)
    o_ref[...] = x_ref[...] + gt_ref[...] * _dot(m.astype(BF), wout_ref[...])


def _mix(x, h, of, ob, yf, gate, w, *, tm, tiles_per_batch, per_batch, row_len):
    n = x.shape[0]
    layer = (w["layer"],)
    ms = _mod_spec(tiles_per_batch, per_batch)
    tile = lambda width: pl.BlockSpec((tm, width), lambda i: (i, 0))
    k1_major = yf.ndim == 4
    pre_specs, pre_args = [], []
    if k1_major:
        k2_tile = tm // FFT_N1
        assert tm % FFT_N1 == 0 and k2_tile % 8 == 0
        tpb = tiles_per_batch
        yf_spec = pl.BlockSpec((None, FFT_N1, k2_tile, BW), lambda i: (i // tpb, 0, i % tpb, 0))
        r = np.arange(tm)
        src = (r % FFT_N1) * k2_tile + r // FFT_N1
        perm = np.zeros((tm, tm), np.float32)
        perm[r, src] = 1.0
        pre_specs, pre_args = [_const_spec((tm, tm))], [jnp.asarray(perm, BF)]
    else:
        yf_spec = tile(BW)
    in_specs = pre_specs + [
        tile(D), tile(D), tile(BW), tile(BW), yf_spec, ms,
        _const_spec((D, BW)), _const_spec((1, BW)),
        _const_spec((D, BW)), _const_spec((D, BW)), _const_spec((D, BW)), _const_spec((D, BW)),
        _const_spec((D, BW)),
        _pick_spec(layer, (NGROUP, SGU_CHUNK, SGU_CHUNK)), _const_spec((SGU_CHUNK, BW)),
        _const_spec((3, BW)), _const_spec((4, BW, D)), _const_spec((4, D, D)),
        _const_spec((4, 1, D)), _const_spec((D, D)),
    ]
    return pl.pallas_call(
        functools.partial(_mix_kernel, tm=tm, row_len=row_len, k1_major=k1_major),
        grid=(n // tm,),
        in_specs=in_specs,
        out_specs=tile(D),
        out_shape=jax.ShapeDtypeStruct((n, D), F32),
        scratch_shapes=[pltpu.VMEM((tm, BW), BF)],
        compiler_params=_params(1),
        name="mix",
    )(*pre_args, x, h, of, ob, yf, gate, w["r"], w["go"], w["su"], w["sv"], w["cb"], w["cc"], w["cx"],
      w["sgu"], w["bsgu"], w["conv"], w["branch"], w["gate"], w["bgate"], w["out"])


def _layer_weights(i, w_in, w_gla_a2, b_gla_a2, g_gla_norm, w_sgu, b_sgu, w_conv, w_branch,
                   w_gate, b_gate, w_out):
    edges = np.cumsum([0, HK, HK, BW, BW, 2 * RANK, BW, BW, BW, BW, BW, BW])
    q, k, v, r, a, su, sv, fn, cb, cc, cx = [
        w_in[i, :, edges[j]:edges[j + 1]].astype(BF) for j in range(11)]
    a2 = w_gla_a2[i].astype(BF)
    a2_tok = jnp.zeros((2, LANES, HK), BF)
    a2_tok = a2_tok.at[0, :RANK].set(a2[0]).at[1, RANK:2 * RANK].set(a2[1])
    cc_, sc_ = _dft_cos_sin(GC)
    return {
        "q": q, "kT": k.T, "v": v, "r": r,
        "a": jnp.pad(a, ((0, 0), (0, LANES - 2 * RANK))), "aT": a.T,
        "a2": a2_tok, "a2T": jnp.transpose(a2_tok[:, :2 * RANK, :], (0, 2, 1)),
        "ba2": b_gla_a2[i][:, None, :], "ba2T": b_gla_a2[i][:, :, None],
        "go": g_gla_norm[i].reshape(1, BW),
        "su": su, "sv": sv, "fn": fn, "cb": cb, "cc": cc, "cx": cx,
        "ccs": _bf16_const(np.concatenate([cc_, -sc_], axis=1)),
        "layer": i,
        "sgu": w_sgu.astype(BF),
        "merge_casts": (_cast_job(w_gate.reshape(DEPTH, 4 * D, D), (i,), 4 * D, D),
                        _cast_job(w_branch.reshape(DEPTH, 4 * BW, D), (i,), 4 * BW, D),
                        _cast_job(w_out, (i,), D, D)),
        "bsgu": jnp.repeat(jnp.transpose(b_sgu[i]), GC, axis=1),
        "conv": w_conv[i],
        "bgate": b_gate[i][:, None, :],
    }


def kernel(x, c, ctx, c_ctx, w_ada, b_ada, g_norm, w_ff1, w_ff3, w_ff2, w_in, w_gla_a2, b_gla_a2,
           g_gla_norm, w_sgu, b_sgu, w_conv, w_branch, w_gate, b_gate, w_out, g_final):
    batch, seq, _ = x.shape
    ctx_len = ctx.shape[1]
    assert seq % TM_LAT == 0 and TM_LAT % GRID_W == 0 and seq == FFT_N1 * FFT_N2
    assert seq % TM_FFN == 0
    assert ctx_len % GLA_C == 0
    tpb = seq // TM_LAT

    cvec = jnp.concatenate([c, c_ctx[None, :]], axis=0)
    cb = jnp.broadcast_to(cvec[:, :, None], (3, D, LANES))
    mods = _adaln(cb, w_ada, b_ada).reshape(DEPTH, 3, N_MOD, 1, D)

    xs = x.reshape(batch * seq, D)
    cs = ctx.reshape(batch * ctx_len, D)
    lat = dict(tm=TM_LAT, tiles_per_batch=tpb, per_batch=True)
    lat_ffn = dict(tm=TM_FFN, tiles_per_batch=seq // TM_FFN, per_batch=True)
    cx_ = dict(tm=batch * ctx_len, tiles_per_batch=1, per_batch=False)
    s_zero = jnp.zeros((batch, 2, 2, 2 * DK, 2 * DV), F32)

    def ff_casts(layer, half):
        lead = (layer, half)
        return (_cast_job(w_ff1, lead, D, D_FF), _cast_job(w_ff3, lead, D, D_FF),
                _cast_job(w_ff2, lead, D_FF, D))

    ff = [w_ff1[0, 0].astype(BF), w_ff3[0, 0].astype(BF), w_ff2[0, 0].astype(BF)]

    for i in range(DEPTH):
        last = i == DEPTH - 1
        ml = [mods[i, 0:2, j] for j in range(N_MOD)]
        mc = [mods[i, 2:3, j] for j in range(N_MOD)]
        gn = g_norm[i][:, None, :]
        w = _layer_weights(i, w_in, w_gla_a2, b_gla_a2, g_gla_norm, w_sgu, b_sgu, w_conv,
                           w_branch, w_gate, b_gate, w_out)

        xs, hl, *gla_l, merge_w = _ffn(xs, gn[0], ml[0], ml[1], ml[2], *ff,
                                       prep=(gn[1], ml[3], ml[4], w, False),
                                       casts=w["merge_casts"], **lat_ffn)
        w["gate"] = merge_w[0].reshape(4, D, D)
        w["branch"] = merge_w[1].reshape(4, BW, D)
        w["out"] = merge_w[2]
        cs, hc, *gla_c, wrc, wic = _ffn(cs, gn[0], mc[0], mc[1], mc[2], *ff,
                                        prep=(gn[1], mc[3], mc[4], w, True), **cx_)

        ofc, obc, s_ctx = _gla(*gla_c, s_zero, w, tm=ctx_len, tiles_per_batch=1)
        ofl, obl, _, ff = _gla(*gla_l, s_ctx, w, tm=TM_LAT, tiles_per_batch=tpb,
                               casts=ff_casts(i, 1))
        yf = _fnet_latent(xs.reshape(batch, seq, D), gn[1], ml[3], ml[4], w, batch)
        xs = _mix(xs, hl, ofl, obl, yf, ml[5], w, row_len=GRID_W, **lat)
        if last:
            xs = _ffn(xs, gn[2], ml[6], ml[7], ml[8], *ff, g_final=g_final[None, :], **lat_ffn)
        else:
            yfc = _fft_direct(wrc, wic, batch, ctx_len)
            cs = _mix(cs, hc, ofc, obc, yfc, mc[5], w, row_len=ctx_len, **cx_)
            cs = _ffn(cs, gn[2], mc[6], mc[7], mc[8], *ff, **cx_)
            xs, ff = _ffn(xs, gn[2], ml[6], ml[7], ml[8], *ff, casts=ff_casts(i + 1, 0), **lat_ffn)
    return xs.reshape(batch, seq, D)
```

```python
import functools
import math

import numpy as np
import jax
import jax.numpy as jnp
from jax import lax
from jax.experimental import pallas as pl
from jax.experimental.pallas import tpu as pltpu

D = 1024
DEPTH = 2
GRID_W = 64
N_MOD = 9
D_FF = 2816
BW = 512
HEADS = 4
DV = 128
DK = 64
HK = HEADS * DK
RANK = 16
GLA_NORMALIZER = 16.0
SGU_CHUNK = 128
GC = 128
NGROUP = 4
EPS = 1e-6
LANES = 128

TM_LAT = 512
TM_FFN = 1024
FF_CHUNK = 256
GLA_C = 128
GLA_SUB = 32
GLA_NSUB = GLA_C // GLA_SUB
FFT_N1 = 64
FFT_N2 = 128
FFT_SUB = 8
FFT_K1_BLOCK = 8
ADA_TN = 1152
VMEM_LIMIT = 56 * 1024 * 1024
FNET_VMEM_LIMIT = 60 * 1024 * 1024

BF = jnp.bfloat16
F32 = jnp.float32

_NT = (((1,), (1,)), ((), ()))


def _dot(a, b):
    return jnp.dot(a, b, preferred_element_type=F32)


def _dot_nt(a, b):
    return lax.dot_general(a, b, _NT, preferred_element_type=F32)


def _sigmoid(x):
    return 1.0 / (1.0 + jnp.exp(-x))


def _silu(x):
    return x * _sigmoid(x)


def _log_sigmoid(x):
    return jnp.minimum(x, 0.0) - jnp.log(1.0 + jnp.exp(-jnp.abs(x)))


def _bf16_const(a):
    return jnp.asarray(a, F32).astype(BF)


def _const_spec(shape):
    nd = len(shape)
    return pl.BlockSpec(shape, lambda *_: (0,) * nd, pipeline_mode=pl.Buffered(1))


def _pick_spec(lead, shape):
    nd = len(shape)
    return pl.BlockSpec((None,) * len(lead) + tuple(shape), lambda *_: tuple(lead) + (0,) * nd,
                        pipeline_mode=pl.Buffered(1))


def _cast_job(src, lead, rows, cols):
    return (src, tuple(lead), rows, cols)


def _with_casts(kernel_fn, n_in, n_out, jobs, n_steps):
    bf16_rows = 16
    chunks = [max(k for k in range(1, n_steps + 1) if rows % (k * bf16_rows) == 0)
              for _, _, rows, _ in jobs]
    in_specs, args, out_specs, out_shape = [], [], [], []
    for (src, lead, rows, cols), ch in zip(jobs, chunks):
        r = rows // ch
        in_specs.append(pl.BlockSpec(
            (None,) * len(lead) + (r, cols),
            lambda i, lead=lead, ch=ch: lead + (jnp.minimum(i, ch - 1), 0)))
        out_specs.append(pl.BlockSpec((r, cols), lambda i, ch=ch: (jnp.minimum(i, ch - 1), 0)))
        args.append(src)
        out_shape.append(jax.ShapeDtypeStruct((rows, cols), BF))
    k = len(jobs)

    def kernel(*refs):
        ins, cast_in = refs[:n_in], refs[n_in:n_in + k]
        outs = refs[n_in + k:n_in + k + n_out]
        cast_out = refs[n_in + k + n_out:n_in + 2 * k + n_out]
        kernel_fn(*ins, *outs, *refs[n_in + 2 * k + n_out:])
        step = pl.program_id(0)
        for src_ref, dst_ref, ch in zip(cast_in, cast_out, chunks):
            if ch == n_steps:
                dst_ref[...] = src_ref[...].astype(BF)
            else:
                @pl.when(step < ch)
                def _(src_ref=src_ref, dst_ref=dst_ref):
                    dst_ref[...] = src_ref[...].astype(BF)

    return kernel, in_specs, args, out_specs, out_shape


def _params(n_grid, vmem_limit=None):
    return pltpu.CompilerParams(dimension_semantics=("arbitrary",) * n_grid,
                                vmem_limit_bytes=vmem_limit or VMEM_LIMIT)


def _norm_mod(x, g, shift, scale):
    hn = x * lax.rsqrt(jnp.mean(x * x, axis=-1, keepdims=True) + EPS) * g
    return hn * (1.0 + scale) + shift


def _adaln_kernel(cb_ref, w_ref, b_ref, o_ref):
    for r in range(3):
        cv = cb_ref[r]
        s = _silu(cv)
        for j in range(ADA_TN // LANES):
            w = w_ref[:, j * LANES:(j + 1) * LANES]
            o_ref[r:r + 1, j * LANES:(j + 1) * LANES] = (
                jnp.sum(w * s, axis=0, keepdims=True) + b_ref[:, j * LANES:(j + 1) * LANES])


def _adaln(cb, w_ada, b_ada):
    nmod = N_MOD * D
    return pl.pallas_call(
        _adaln_kernel,
        grid=(DEPTH, nmod // ADA_TN),
        in_specs=[
            pl.BlockSpec((3, D, LANES), lambda l, j: (0, 0, 0)),
            pl.BlockSpec((None, D, ADA_TN), lambda l, j: (l, 0, j)),
            pl.BlockSpec((None, 1, ADA_TN), lambda l, j: (l, 0, j)),
        ],
        out_specs=pl.BlockSpec((None, 3, ADA_TN), lambda l, j: (l, 0, j)),
        out_shape=jax.ShapeDtypeStruct((DEPTH, 3, nmod), F32),
        compiler_params=_params(2),
        name="adaln",
    )(cb, w_ada, b_ada.reshape(DEPTH, 1, nmod))


def _ffn_kernel(*refs, tail):
    s_ref, g_ref, sh_ref, sc_ref, gt_ref, w1_ref, w3_ref, w2_ref = refs[:8]
    rest = refs[8:]
    s = s_ref[...]
    h = _norm_mod(s, g_ref[...], sh_ref[...], sc_ref[...]).astype(BF)
    acc = jnp.zeros(s.shape, F32)
    for j in range(D_FF // FF_CHUNK):
        cols = slice(j * FF_CHUNK, (j + 1) * FF_CHUNK)
        a = _dot(h, w1_ref[:, cols])
        b = _dot(h, w3_ref[:, cols])
        u = (_silu(a) * b).astype(BF)
        acc = acc + _dot(u, w2_ref[cols, :])
    out = s + 0.5 * gt_ref[...] * acc
    if tail is None:
        rest[0][...] = out
    elif tail == "final":
        gf_ref, o_ref = rest
        o_ref[...] = out * lax.rsqrt(jnp.mean(out * out, axis=-1, keepdims=True) + EPS) * gf_ref[...]
    else:
        (g2_ref, sh2_ref, sc2_ref, wq_ref, wkT_ref, wv_ref, wa_ref, waT_ref) = rest[:8]
        rest = rest[8:]
        if tail == "prep_fnet":
            wfn_ref, ccs_ref = rest[:2]
            rest = rest[2:]
        o_ref, h_ref, q_ref, kT_ref, v_ref, a_ref, aT_ref = rest[:7]
        o_ref[...] = out
        h2 = _norm_mod(out, g2_ref[...], sh2_ref[...], sc2_ref[...]).astype(BF)
        h_ref[...] = h2
        q_ref[...] = (_dot(h2, wq_ref[...]) * (DK ** -0.5)).astype(BF)
        kT_ref[...] = _dot_nt(wkT_ref[...], h2).astype(BF)
        v_ref[...] = _dot(h2, wv_ref[...]).astype(BF)
        a_ref[...] = _dot(h2, wa_ref[...]).astype(BF)
        aT_ref[...] = _dot_nt(waT_ref[...], h2).astype(BF)
        if tail == "prep_fnet":
            wr_ref, wi_ref = rest[7:]
            fn = _dot(h2, wfn_ref[...]).astype(BF)
            for g in range(NGROUP):
                pq = _dot(fn[:, g * GC:(g + 1) * GC], ccs_ref[...])
                wr_ref[:, g * GC:(g + 1) * GC] = pq[:, :GC]
                wi_ref[:, g * GC:(g + 1) * GC] = pq[:, GC:]


def _mod_spec(tiles_per_batch, per_batch):
    if per_batch:
        return pl.BlockSpec((None, 1, D), lambda i: (i // tiles_per_batch, 0, 0))
    return pl.BlockSpec((None, 1, D), lambda i: (0, 0, 0))


def _ffn(s, g, shift, scale, gate, w1, w3, w2, *, tm, tiles_per_batch, per_batch,
         g_final=None, prep=None, casts=()):
    n = s.shape[0]
    ms = _mod_spec(tiles_per_batch, per_batch)
    rows = lambda width: pl.BlockSpec((tm, width), lambda i: (i, 0))
    cols = lambda height: pl.BlockSpec((height, tm), lambda i: (0, i))
    in_specs = [
        rows(D), _const_spec((1, D)), ms, ms, ms,
        _const_spec((D, D_FF)), _const_spec((D, D_FF)), _const_spec((D_FF, D)),
    ]
    args = [s, g, shift, scale, gate, w1, w3, w2]
    out_specs = [rows(D)]
    out_shape = [jax.ShapeDtypeStruct((n, D), F32)]
    tail = None
    if g_final is not None:
        tail = "final"
        in_specs.append(_const_spec((1, D)))
        args.append(g_final)
    elif prep is not None:
        g2, shift2, scale2, w, with_fnet = prep
        tail = "prep_fnet" if with_fnet else "prep"
        in_specs += [_const_spec((1, D)), ms, ms,
                     _const_spec((D, HK)), _const_spec((HK, D)), _const_spec((D, BW)),
                     _const_spec((D, LANES)), _const_spec((2 * RANK, D))]
        args += [g2, shift2, scale2, w["q"], w["kT"], w["v"], w["a"], w["aT"]]
        out_specs += [rows(D), rows(HK), cols(HK), rows(BW), rows(LANES), cols(2 * RANK)]
        out_shape += [
            jax.ShapeDtypeStruct((n, D), BF), jax.ShapeDtypeStruct((n, HK), BF),
            jax.ShapeDtypeStruct((HK, n), BF), jax.ShapeDtypeStruct((n, BW), BF),
            jax.ShapeDtypeStruct((n, LANES), BF), jax.ShapeDtypeStruct((2 * RANK, n), BF),
        ]
        if with_fnet:
            in_specs += [_const_spec((D, BW)), _const_spec((GC, 2 * GC))]
            args += [w["fn"], w["ccs"]]
            out_specs += [rows(BW), rows(BW)]
            out_shape += [jax.ShapeDtypeStruct((n, BW), F32)] * 2
    body = functools.partial(_ffn_kernel, tail=tail)
    n_main = len(out_shape)
    if casts:
        body, c_in, c_args, c_out, c_shape = _with_casts(body, len(args), n_main, casts, n // tm)
        in_specs, args = in_specs + c_in, args + c_args
        out_specs, out_shape = out_specs + c_out, out_shape + c_shape
    res = pl.pallas_call(
        body,
        grid=(n // tm,),
        in_specs=in_specs,
        out_specs=out_specs,
        out_shape=out_shape,
        compiler_params=_params(1),
        name="ffn" if tail is None else "ffn_" + tail,
    )(*args)
    main = list(res[:n_main]) if prep is not None else [res[0]]
    if casts:
        main.append(list(res[n_main:]))
    return main if len(main) > 1 else main[0]


def _bd_mask():
    r = lax.broadcasted_iota(jnp.int32, (2 * DK, 2 * DV), 0) // DK
    c = lax.broadcasted_iota(jnp.int32, (2 * DK, 2 * DV), 1) // DV
    return r == c


def _gla_consts():
    c = GLA_C
    t = np.arange(c)
    sub = t // GLA_SUB
    lc = np.stack([t[None, :] <= t[:, None], t[None, :] >= t[:, None]]).astype(np.float32)
    tt, jj = t[:, None], t[None, :]
    sub_end = (sub * GLA_SUB + GLA_SUB - 1)[None, :]
    sub_start = (sub * GLA_SUB)[None, :]
    mk_f = (tt > jj) & (tt <= sub_end)
    mk_b = (tt < jj) & (tt >= sub_start)
    me_f = tt > jj
    me_b = tt < jj
    ones = np.ones((c, c), bool)
    mcat = np.stack([np.concatenate([mk_f, me_f, ones], axis=1),
                     np.concatenate([mk_b, me_b, ones], axis=1)]).astype(np.float32)
    return jnp.asarray(lc, BF), jnp.asarray(mcat, BF)


def _gla_direction(dr, q_ref, kT_ref, v_ref, a_ref, aT_ref, wa2_ref, wa2T_ref, ba2_ref, ba2T_ref,
                   lc_ref, mcat_ref, state_ref, o_ref, nchunk):
    c = GLA_C
    inv = 1.0 / GLA_NORMALIZER
    g = (_log_sigmoid(_dot(a_ref[...], wa2_ref[dr]) + ba2_ref[dr]) * inv).astype(BF)
    gT = (_log_sigmoid(_dot(wa2T_ref[dr], aT_ref[...]) + ba2T_ref[dr]) * inv).astype(BF)
    col_sub = lax.broadcasted_iota(jnp.int32, (2 * DK, c), 1) // GLA_SUB
    ii = lax.broadcasted_iota(jnp.int32, (c, c), 0)
    jj = lax.broadcasted_iota(jnp.int32, (c, c), 1)
    tri = (jj <= ii) if dr == 0 else (jj >= ii)
    low_half = lax.broadcasted_iota(jnp.int32, (c, LANES), 1) < DK
    bd = _bd_mask()
    zero_v = jnp.zeros((c, DV), BF)
    state = [state_ref[dr, 0], state_ref[dr, 1]]
    order = range(nchunk) if dr == 0 else range(nchunk - 1, -1, -1)
    sums = []
    for ci in order:
        rows = slice(ci * c, (ci + 1) * c)
        sums.append((_dot(lc_ref[dr], g[rows]),
                     _dot(gT[:, rows], mcat_ref[dr])))
    yield
    operands = []
    for ci, (b, fm) in zip(order, sums):
        rows = slice(ci * c, (ci + 1) * c)
        kc = kT_ref[:, rows].astype(F32)
        ksub = (kc * jnp.exp(fm[:, 0:c])).astype(BF)
        kend = (kc * jnp.exp(fm[:, c:2 * c])).astype(BF)
        dec = jnp.exp(fm[:, 2 * c:3 * c])
        qc = q_ref[rows, :].astype(F32)
        qb = (qc * jnp.exp(b)).astype(BF)
        for p in range(2):
            lanes = slice(p * LANES, (p + 1) * LANES)
            qp = qc[:, lanes]
            bp = b[:, lanes]
            kp = ksub[lanes]
            qblocks = []
            kblocks = []
            for j in range(GLA_NSUB):
                if dr == 0:
                    ref_row, lo, hi = j * GLA_SUB + GLA_SUB - 1, j * GLA_SUB, c
                else:
                    ref_row, lo, hi = j * GLA_SUB, 0, (j + 1) * GLA_SUB
                live = qp[lo:hi] * jnp.exp(bp[lo:hi] - bp[ref_row:ref_row + 1, :])
                parts = []
                if lo > 0:
                    parts.append(jnp.zeros((lo, LANES), F32))
                parts.append(live)
                if hi < c:
                    parts.append(jnp.zeros((c - hi, LANES), F32))
                qblocks.append(jnp.concatenate(parts, axis=0) if len(parts) > 1 else live)
                kblocks.append(jnp.where(col_sub == j, kp, jnp.zeros_like(kp)))
            hats = []
            for hd in range(2):
                qh = []
                for m in range(GLA_NSUB // 2):
                    even, odd = qblocks[2 * m], qblocks[2 * m + 1]
                    if hd == 0:
                        qh.append(jnp.where(low_half, even, pltpu.roll(odd, DK, 1)))
                    else:
                        qh.append(jnp.where(low_half, pltpu.roll(even, DK, 1), odd))
                hats.append((jnp.concatenate(qh, axis=1).astype(BF),
                             jnp.concatenate([kj[hd * DK:(hd + 1) * DK] for kj in kblocks], axis=0)))
            dp = dec[lanes]
            operands.append((ci, p, hats, qb[:, lanes], kend[lanes],
                             jnp.concatenate([dp, dp], axis=1)))
    yield
    scores = []
    for ci, p, hats, qbp, kendp, decay in operands:
        heads = [jnp.where(tri, _dot(qhat, khat), 0.0).astype(BF) for qhat, khat in hats]
        scores.append(jnp.concatenate(heads, axis=1))
    yield
    pending = []
    for (ci, p, hats, qbp, kendp, decay), att in zip(operands, scores):
        vp = v_ref[ci * c:(ci + 1) * c, p * 2 * DV:(p + 1) * 2 * DV]
        vbd = jnp.concatenate([
            jnp.concatenate([vp[:, :DV], zero_v], axis=1),
            jnp.concatenate([zero_v, vp[:, DV:]], axis=1)], axis=0)
        pending.append((ci, p, _dot(att, vbd), qbp, decay, jnp.where(bd, _dot(kendp, vp), 0.0)))
    yield
    for ci, p, intra, qbp, decay, upd in pending:
        o_p = intra + _dot(qbp, state[p].astype(BF))
        o_ref[ci * c:(ci + 1) * c, p * 2 * DV:(p + 1) * 2 * DV] = o_p.astype(BF)
        state[p] = decay * state[p] + upd
    state_ref[dr, 0] = state[0]
    state_ref[dr, 1] = state[1]


def _gla_kernel(qf_ref, kTf_ref, vf_ref, af_ref, aTf_ref, qb_ref, kTb_ref, vb_ref, ab_ref, aTb_ref,
                s0_ref, wa2_ref, wa2T_ref, ba2_ref, ba2T_ref, lc_ref, mcat_ref,
                of_ref, ob_ref, sfin_ref, state_ref, *, tm, tiles_per_batch):
    @pl.when(pl.program_id(0) % tiles_per_batch == 0)
    def _():
        state_ref[...] = s0_ref[...]

    shared = (wa2_ref, wa2T_ref, ba2_ref, ba2T_ref, lc_ref, mcat_ref, state_ref)
    scans = [
        _gla_direction(0, qf_ref, kTf_ref, vf_ref, af_ref, aTf_ref, *shared, of_ref, tm // GLA_C),
        _gla_direction(1, qb_ref, kTb_ref, vb_ref, ab_ref, aTb_ref, *shared, ob_ref, tm // GLA_C),
    ]
    live = list(scans)
    while live:
        live = [scan for scan in live if next(scan, "done") != "done"]
    sfin_ref[...] = state_ref[...]


def _gla(q, kT, v, a, aT, s0, w, *, tm, tiles_per_batch, casts=()):
    n = q.shape[0]
    nt = n // tm
    batch = nt // tiles_per_batch
    lc, mcat = _gla_consts()
    tpb = tiles_per_batch
    fwd = lambda i: i
    bwd = lambda i: (i // tpb) * tpb + (tpb - 1 - i % tpb)

    def tile_specs(idx):
        return [
            pl.BlockSpec((tm, HK), lambda i: (idx(i), 0)),
            pl.BlockSpec((HK, tm), lambda i: (0, idx(i))),
            pl.BlockSpec((tm, BW), lambda i: (idx(i), 0)),
            pl.BlockSpec((tm, LANES), lambda i: (idx(i), 0)),
            pl.BlockSpec((2 * RANK, tm), lambda i: (0, idx(i))),
        ]

    state_block = (None, 2, 2, 2 * DK, 2 * DV)
    in_specs = tile_specs(fwd) + tile_specs(bwd) + [
        pl.BlockSpec(state_block, lambda i: (i // tpb, 0, 0, 0, 0)),
        _const_spec((2, LANES, HK)), _const_spec((2, HK, 2 * RANK)),
        _const_spec((2, 1, HK)), _const_spec((2, HK, 1)),
        _const_spec((2, GLA_C, GLA_C)), _const_spec((2, GLA_C, 3 * GLA_C)),
    ]
    out_specs = [
        pl.BlockSpec((tm, BW), lambda i: (fwd(i), 0)),
        pl.BlockSpec((tm, BW), lambda i: (bwd(i), 0)),
        pl.BlockSpec(state_block, lambda i: (i // tpb, 0, 0, 0, 0)),
    ]
    out_shape = [
        jax.ShapeDtypeStruct((n, BW), BF), jax.ShapeDtypeStruct((n, BW), BF),
        jax.ShapeDtypeStruct((batch, 2, 2, 2 * DK, 2 * DV), F32),
    ]
    tiles = (q, kT, v, a, aT)
    args = [*tiles, *tiles, s0, w["a2"], w["a2T"], w["ba2"], w["ba2T"], lc, mcat]
    body = functools.partial(_gla_kernel, tm=tm, tiles_per_batch=tpb)
    if casts:
        body, c_in, c_args, c_out, c_shape = _with_casts(body, len(args), 3, casts, nt)
        in_specs, args = in_specs + c_in, args + c_args
        out_specs, out_shape = out_specs + c_out, out_shape + c_shape
    res = pl.pallas_call(
        body,
        grid=(nt,),
        in_specs=in_specs,
        out_specs=out_specs,
        out_shape=out_shape,
        scratch_shapes=[pltpu.VMEM((2, 2, 2 * DK, 2 * DV), F32)],
        compiler_params=_params(1),
        name="gla",
    )(*args)
    return (*res[:3], list(res[3:])) if casts else res


def _dft_cos_sin(n):
    k = np.arange(n)
    ang = 2.0 * np.pi * ((k[:, None] * k[None, :]) % n) / n
    return np.cos(ang), np.sin(ang)


def _fnet_kernel(x_ref, g_ref, sh_ref, sc_ref, wfn_ref, ccs_ref, f1_ref, tc_ref, ts_ref, f2_ref,
                 o_ref, zr_scr, zi_scr):
    j = pl.program_id(1)
    n_stage1 = FFT_N2 // FFT_SUB

    @pl.when(j < n_stage1)
    def _():
        rows = FFT_N1 * FFT_SUB
        x = x_ref[...].reshape(rows, D)
        h = _norm_mod(x, g_ref[...], sh_ref[...], sc_ref[...]).astype(BF)
        fn = _dot(h, wfn_ref[...]).astype(BF)
        re, im = [], []
        for g in range(NGROUP):
            pq = _dot(fn[:, g * GC:(g + 1) * GC], ccs_ref[...])
            re.append(pq[:, :GC].astype(BF))
            im.append(pq[:, GC:].astype(BF))
        w = jnp.concatenate([jnp.concatenate(re, axis=1), jnp.concatenate(im, axis=1)], axis=0)
        y = _dot(f1_ref[...], w)
        yr, yi = y[:rows], y[rows:]
        tc = jnp.concatenate([tc_ref[...]] * (BW // LANES), axis=1)
        ts = jnp.concatenate([ts_ref[...]] * (BW // LANES), axis=1)
        group = pl.ds(pl.multiple_of(j * FFT_SUB, FFT_SUB), FFT_SUB)
        zr_scr[:, group, :] = (yr * tc + yi * ts).reshape(FFT_N1, FFT_SUB, BW)
        zi_scr[:, group, :] = (yi * tc - yr * ts).reshape(FFT_N1, FFT_SUB, BW)

    @pl.when(j >= n_stage1)
    def _():
        base = (j - n_stage1) * FFT_K1_BLOCK
        for k in range(FFT_K1_BLOCK):
            z = jnp.concatenate([zr_scr[base + k].astype(BF), zi_scr[base + k].astype(BF)], axis=0)
            o_ref[k] = _dot(f2_ref[...], z)


def _fft2_kernel(zr_ref, zi_ref, f_ref, o_ref):
    for k in range(zr_ref.shape[0]):
        z = jnp.concatenate([zr_ref[k].astype(BF), zi_ref[k].astype(BF)], axis=0)
        o_ref[k] = _dot(f_ref[...], z)


def _fnet_latent(x, g, shift, scale, w, batch):
    n1, n2, sub = FFT_N1, FFT_N2, FFT_SUB
    rows = n1 * sub
    c1, s1 = _dft_cos_sin(n1)
    eye = np.eye(sub)
    f1 = np.block([[np.kron(c1, eye), np.kron(s1, eye)],
                   [np.kron(-s1, eye), np.kron(c1, eye)]]) / math.sqrt(n1)
    k1 = np.arange(n1)[None, :, None]
    m2 = np.arange(n2).reshape(n2 // sub, 1, sub)
    ang = (2.0 * np.pi * (k1 * m2) / (n1 * n2)).reshape(n2 // sub, rows, 1)
    tc = np.broadcast_to(np.cos(ang), (n2 // sub, rows, LANES)).astype(np.float32)
    ts = np.broadcast_to(np.sin(ang), (n2 // sub, rows, LANES)).astype(np.float32)
    c2, s2 = _dft_cos_sin(n2)
    f2 = np.concatenate([c2, s2], axis=1) / math.sqrt(n2 * GC)
    s1_steps, s2_steps = n2 // sub, n1 // FFT_K1_BLOCK
    group = lambda j: jnp.minimum(j, s1_steps - 1)
    mod = pl.BlockSpec((None, 1, D), lambda b, j: (b, 0, 0))
    tw = pl.BlockSpec((None, rows, LANES), lambda b, j: (group(j), 0, 0))
    return pl.pallas_call(
        _fnet_kernel,
        grid=(batch, s1_steps + s2_steps),
        in_specs=[pl.BlockSpec((None, n1, sub, D), lambda b, j: (b, 0, group(j), 0)),
                  _const_spec((1, D)), mod, mod, _const_spec((D, BW)), _const_spec((GC, 2 * GC)),
                  _const_spec((2 * rows, 2 * rows)), tw, tw, _const_spec((n2, 2 * n2))],
        out_specs=pl.BlockSpec((None, FFT_K1_BLOCK, n2, BW),
                               lambda b, j: (b, jnp.maximum(j - s1_steps, 0), 0, 0)),
        out_shape=jax.ShapeDtypeStruct((batch, n1, n2, BW), F32),
        scratch_shapes=[pltpu.VMEM((n1, n2, BW), F32), pltpu.VMEM((n1, n2, BW), F32)],
        compiler_params=_params(2, FNET_VMEM_LIMIT),
        name="fnet",
    )(x.reshape(batch, n1, n2, D), g, shift, scale, w["fn"], w["ccs"], _bf16_const(f1),
      jnp.asarray(tc), jnp.asarray(ts), _bf16_const(f2))


def _fft_direct(wr, wi, batch, seq):
    c2, s2 = _dft_cos_sin(seq)
    f2 = np.concatenate([c2, s2], axis=1) / math.sqrt(seq * GC)
    zblk = pl.BlockSpec((1, seq, BW), lambda b: (b, 0, 0))
    y = pl.pallas_call(
        _fft2_kernel,
        grid=(batch,),
        in_specs=[zblk, zblk, _const_spec((seq, 2 * seq))],
        out_specs=zblk,
        out_shape=jax.ShapeDtypeStruct((batch, seq, BW), F32),
        compiler_params=_params(1),
        name="fft_direct",
    )(wr.reshape(batch, seq, BW), wi.reshape(batch, seq, BW), _bf16_const(f2))
    return y.reshape(batch * seq, BW)


def _mix_kernel(*refs, tm, row_len, k1_major):
    if k1_major:
        perm_ref, refs = refs[0], refs[1:]
    (x_ref, h_ref, of_ref, ob_ref, yf_ref, gt_ref, wr_ref, go_ref, wsu_ref, wsv_ref,
     wcb_ref, wcc_ref, wcx_ref, wsgu_ref, bsgu_ref, wconv_ref, wbr_ref, wgate_ref,
     bgate_ref, wout_ref, o_ref, br_ref) = refs
    h = h_ref[...]
    if k1_major:
        yf = _dot(perm_ref[...], yf_ref[...].reshape(tm, BW).astype(BF)).astype(BF)
    else:
        yf = yf_ref[...].astype(BF)

    def gated(k, branch):
        gate = _sigmoid(_dot(h, wgate_ref[k]) + bgate_ref[k])
        return gate * _dot(branch, wbr_ref[k])

    r = _dot(h, wr_ref[...])
    for hd in range(HEADS):
        lanes = slice(hd * DV, (hd + 1) * DV)
        oh = of_ref[:, lanes].astype(F32) + ob_ref[:, lanes].astype(F32)
        on = oh * lax.rsqrt(jnp.mean(oh * oh, axis=-1, keepdims=True) + EPS) * go_ref[:, lanes]
        br_ref[:, lanes] = (on * _silu(r[:, lanes])).astype(BF)
    m = gated(0, br_ref[...])
    su = _dot(h, wsu_ref[...])
    sv = _dot(h, wsv_ref[...])
    for g in range(NGROUP):
        lanes = slice(g * GC, (g + 1) * GC)
        vg = sv[:, lanes]
        vc = vg - jnp.mean(vg, axis=-1, keepdims=True)
        z = (vc * lax.rsqrt(jnp.mean(vc * vc, axis=-1, keepdims=True) + EPS)).astype(BF)
        for n in range(tm // SGU_CHUNK):
            rows = slice(n * SGU_CHUNK, (n + 1) * SGU_CHUNK)
            s = _dot(wsgu_ref[g], z[rows]) + bsgu_ref[:, lanes]
            br_ref[rows, lanes] = (su[rows, lanes] * s).astype(BF)
    m = m + gated(1, br_ref[...])
    m = m + gated(2, yf)
    zc = _dot(h, wcc_ref[...]) * _dot(h, wcx_ref[...])
    pos = lax.broadcasted_iota(jnp.int32, (tm, BW), 0) % row_len
    left = jnp.where(pos == 0, 0.0, pltpu.roll(zc, 1, 0))
    right = jnp.where(pos == row_len - 1, 0.0, pltpu.roll(zc, tm - 1, 0))
    y = left * wconv_ref[0:1, :] + zc * wconv_ref[1:2, :] + right * wconv_ref[2:3, :]
    conv = (_dot(h, wcb_ref[...]) * y).astype(BF)
    m = m + gated(3, conv)
    o_ref[...] = x_ref[...] + gt_ref[...] * _dot(m.astype(BF), wout_ref[...])


def _mix(x, h, of, ob, yf, gate, w, *, tm, tiles_per_batch, per_batch, row_len):
    n = x.shape[0]
    layer = (w["layer"],)
    ms = _mod_spec(tiles_per_batch, per_batch)
    tile = lambda width: pl.BlockSpec((tm, width), lambda i: (i, 0))
    k1_major = yf.ndim == 4
    pre_specs, pre_args = [], []
    if k1_major:
        k2_tile = tm // FFT_N1
        assert tm % FFT_N1 == 0 and k2_tile % 8 == 0
        tpb = tiles_per_batch
        yf_spec = pl.BlockSpec((None, FFT_N1, k2_tile, BW), lambda i: (i // tpb, 0, i % tpb, 0))
        r = np.arange(tm)
        src = (r % FFT_N1) * k2_tile + r // FFT_N1
        perm = np.zeros((tm, tm), np.float32)
        perm[r, src] = 1.0
        pre_specs, pre_args = [_const_spec((tm, tm))], [jnp.asarray(perm, BF)]
    else:
        yf_spec = tile(BW)
    in_specs = pre_specs + [
        tile(D), tile(D), tile(BW), tile(BW), yf_spec, ms,
        _const_spec((D, BW)), _const_spec((1, BW)),
        _const_spec((D, BW)), _const_spec((D, BW)), _const_spec((D, BW)), _const_spec((D, BW)),
        _const_spec((D, BW)),
        _pick_spec(layer, (NGROUP, SGU_CHUNK, SGU_CHUNK)), _const_spec((SGU_CHUNK, BW)),
        _const_spec((3, BW)), _const_spec((4, BW, D)), _const_spec((4, D, D)),
        _const_spec((4, 1, D)), _const_spec((D, D)),
    ]
    return pl.pallas_call(
        functools.partial(_mix_kernel, tm=tm, row_len=row_len, k1_major=k1_major),
        grid=(n // tm,),
        in_specs=in_specs,
        out_specs=tile(D),
        out_shape=jax.ShapeDtypeStruct((n, D), F32),
        scratch_shapes=[pltpu.VMEM((tm, BW), BF)],
        compiler_params=_params(1),
        name="mix",
    )(*pre_args, x, h, of, ob, yf, gate, w["r"], w["go"], w["su"], w["sv"], w["cb"], w["cc"], w["cx"],
      w["sgu"], w["bsgu"], w["conv"], w["branch"], w["gate"], w["bgate"], w["out"])


_IN_EDGES = np.cumsum([0, HK, HK, BW, BW, 2 * RANK, BW, BW, BW, BW, BW, BW])
_IN_NAMES = ("q", "k", "v", "r", "a", "su", "sv", "fn", "cb", "cc", "cx")


def _in_columns(w_in_layer):
    return {name: w_in_layer[:, _IN_EDGES[j]:_IN_EDGES[j + 1]].astype(BF)
            for j, name in enumerate(_IN_NAMES)}


def _layer_weights(i, w_in_layer, w_gla_a2, b_gla_a2, g_gla_norm, w_sgu, b_sgu, w_conv, w_branch,
                   w_gate, b_gate, w_out):
    cols = _in_columns(w_in_layer)
    q, k, v, r, a, su, sv, fn, cb, cc, cx = [cols[name] for name in _IN_NAMES]
    a2 = w_gla_a2[i].astype(BF)
    a2_tok = jnp.zeros((2, LANES, HK), BF)
    a2_tok = a2_tok.at[0, :RANK].set(a2[0]).at[1, RANK:2 * RANK].set(a2[1])
    cc_, sc_ = _dft_cos_sin(GC)
    return {
        "q": q, "kT": k.T, "v": v, "r": r,
        "a": jnp.pad(a, ((0, 0), (0, LANES - 2 * RANK))), "aT": a.T,
        "a2": a2_tok, "a2T": jnp.transpose(a2_tok[:, :2 * RANK, :], (0, 2, 1)),
        "ba2": b_gla_a2[i][:, None, :], "ba2T": b_gla_a2[i][:, :, None],
        "go": g_gla_norm[i].reshape(1, BW),
        "su": su, "sv": sv, "fn": fn, "cb": cb, "cc": cc, "cx": cx,
        "ccs": _bf16_const(np.concatenate([cc_, -sc_], axis=1)),
        "layer": i,
        "sgu": w_sgu.astype(BF),
        "merge_casts": (_cast_job(w_gate.reshape(DEPTH, 4 * D, D), (i,), 4 * D, D),
                        _cast_job(w_branch.reshape(DEPTH, 4 * BW, D), (i,), 4 * BW, D),
                        _cast_job(w_out, (i,), D, D)),
        "bsgu": jnp.repeat(jnp.transpose(b_sgu[i]), GC, axis=1),
        "conv": w_conv[i],
        "bgate": b_gate[i][:, None, :],
    }


def kernel(x, c, ctx, c_ctx, w_ada, b_ada, g_norm, w_ff1, w_ff3, w_ff2, w_in, w_gla_a2, b_gla_a2,
           g_gla_norm, w_sgu, b_sgu, w_conv, w_branch, w_gate, b_gate, w_out, g_final):
    batch, seq, _ = x.shape
    ctx_len = ctx.shape[1]
    assert seq % TM_LAT == 0 and TM_LAT % GRID_W == 0 and seq == FFT_N1 * FFT_N2
    assert seq % TM_FFN == 0
    assert ctx_len % GLA_C == 0
    tpb = seq // TM_LAT

    cvec = jnp.concatenate([c, c_ctx[None, :]], axis=0)
    cb = jnp.broadcast_to(cvec[:, :, None], (3, D, LANES))
    mods = _adaln(cb, w_ada, b_ada).reshape(DEPTH, 3, N_MOD, 1, D)

    xs = x.reshape(batch * seq, D)
    cs = ctx.reshape(batch * ctx_len, D)
    lat = dict(tm=TM_LAT, tiles_per_batch=tpb, per_batch=True)
    lat_ffn = dict(tm=TM_FFN, tiles_per_batch=seq // TM_FFN, per_batch=True)
    cx_ = dict(tm=batch * ctx_len, tiles_per_batch=1, per_batch=False)
    s_zero = jnp.zeros((batch, 2, 2, 2 * DK, 2 * DV), F32)

    def ff_casts(layer, half):
        lead = (layer, half)
        return (_cast_job(w_ff1, lead, D, D_FF), _cast_job(w_ff3, lead, D, D_FF),
                _cast_job(w_ff2, lead, D_FF, D))

    ff = [w_ff1[0, 0].astype(BF), w_ff3[0, 0].astype(BF), w_ff2[0, 0].astype(BF)]

    for i in range(DEPTH):
        last = i == DEPTH - 1
        ml = [mods[i, 0:2, j] for j in range(N_MOD)]
        mc = [mods[i, 2:3, j] for j in range(N_MOD)]
        gn = g_norm[i][:, None, :]
        w = _layer_weights(i, w_in[0] if i == 0 else w_in_next, w_gla_a2, b_gla_a2, g_gla_norm,
                           w_sgu, b_sgu, w_conv, w_branch, w_gate, b_gate, w_out)

        xs, hl, *gla_l, merge_w = _ffn(xs, gn[0], ml[0], ml[1], ml[2], *ff,
                                       prep=(gn[1], ml[3], ml[4], w, False),
                                       casts=w["merge_casts"], **lat_ffn)
        w["gate"] = merge_w[0].reshape(4, D, D)
        w["branch"] = merge_w[1].reshape(4, BW, D)
        w["out"] = merge_w[2]
        cs, hc, *gla_c, wrc, wic = _ffn(cs, gn[0], mc[0], mc[1], mc[2], *ff,
                                        prep=(gn[1], mc[3], mc[4], w, True), **cx_)

        ofc, obc, s_ctx = _gla(*gla_c, s_zero, w, tm=ctx_len, tiles_per_batch=1)
        in_jobs = tuple(_cast_job(w_in, (l,), D, w_in.shape[2]) for l in range(DEPTH)) if i == 0 else ()
        ofl, obl, _, narrowed = _gla(*gla_l, s_ctx, w, tm=TM_LAT, tiles_per_batch=tpb,
                                     casts=ff_casts(i, 1) + in_jobs)
        ff = narrowed[:3]
        if i == 0:
            late = _in_columns(narrowed[3])
            w.update({name: late[name] for name in ("r", "su", "sv", "cb", "cc", "cx")})
            w_in_next = narrowed[4]
        yf = _fnet_latent(xs.reshape(batch, seq, D), gn[1], ml[3], ml[4], w, batch)
        xs = _mix(xs, hl, ofl, obl, yf, ml[5], w, row_len=GRID_W, **lat)
        if last:
            xs = _ffn(xs, gn[2], ml[6], ml[7], ml[8], *ff, g_final=g_final[None, :], **lat_ffn)
        else:
            yfc = _fft_direct(wrc, wic, batch, ctx_len)
            cs = _mix(cs, hc, ofc, obc, yfc, mc[5], w, row_len=ctx_len, **cx_)
            cs = _ffn(cs, gn[2], mc[6], mc[7], mc[8], *ff, **cx_)
            xs, ff = _ffn(xs, gn[2], ml[6], ml[7], ml[8], *ff, casts=ff_casts(i + 1, 0), **lat_ffn)
    return xs.reshape(batch, seq, D)
```

```python
import functools
import math

import numpy as np
import jax
import jax.numpy as jnp
from jax import lax
from jax.experimental import pallas as pl
from jax.experimental.pallas import tpu as pltpu

D = 1024
DEPTH = 2
GRID_W = 64
N_MOD = 9
D_FF = 2816
BW = 512
HEADS = 4
DV = 128
DK = 64
HK = HEADS * DK
RANK = 16
GLA_NORMALIZER = 16.0
SGU_CHUNK = 128
GC = 128
NGROUP = 4
EPS = 1e-6
LANES = 128

TM_LAT = 512
TM_FFN = 1024
FF_CHUNK = 256
GLA_C = 128
GLA_SUB = 32
GLA_NSUB = GLA_C // GLA_SUB
FFT_N1 = 64
FFT_N2 = 128
FFT_SUB = 8
FFT_K1_BLOCK = 8
ADA_TN = 1152
VMEM_LIMIT = 56 * 1024 * 1024
FNET_VMEM_LIMIT = 60 * 1024 * 1024

BF = jnp.bfloat16
F32 = jnp.float32

_NT = (((1,), (1,)), ((), ()))


def _dot(a, b):
    return jnp.dot(a, b, preferred_element_type=F32)


def _dot_nt(a, b):
    return lax.dot_general(a, b, _NT, preferred_element_type=F32)


def _sigmoid(x):
    return 1.0 / (1.0 + jnp.exp(-x))


def _silu(x):
    return x * _sigmoid(x)


def _log_sigmoid(x):
    return jnp.minimum(x, 0.0) - jnp.log(1.0 + jnp.exp(-jnp.abs(x)))


def _bf16_const(a):
    return jnp.asarray(a, F32).astype(BF)


def _const_spec(shape):
    nd = len(shape)
    return pl.BlockSpec(shape, lambda *_: (0,) * nd, pipeline_mode=pl.Buffered(1))


def _pick_spec(lead, shape):
    nd = len(shape)
    return pl.BlockSpec((None,) * len(lead) + tuple(shape), lambda *_: tuple(lead) + (0,) * nd,
                        pipeline_mode=pl.Buffered(1))


def _cast_job(src, lead, rows, cols):
    return (src, tuple(lead), rows, cols)


def _with_casts(kernel_fn, n_in, n_out, jobs, n_steps):
    bf16_rows = 16
    chunks = [max(k for k in range(1, n_steps + 1) if rows % (k * bf16_rows) == 0)
              for _, _, rows, _ in jobs]
    in_specs, args, out_specs, out_shape = [], [], [], []
    for (src, lead, rows, cols), ch in zip(jobs, chunks):
        r = rows // ch
        in_specs.append(pl.BlockSpec(
            (None,) * len(lead) + (r, cols),
            lambda i, lead=lead, ch=ch: lead + (jnp.minimum(i, ch - 1), 0)))
        out_specs.append(pl.BlockSpec((r, cols), lambda i, ch=ch: (jnp.minimum(i, ch - 1), 0)))
        args.append(src)
        out_shape.append(jax.ShapeDtypeStruct((rows, cols), BF))
    k = len(jobs)

    def kernel(*refs):
        ins, cast_in = refs[:n_in], refs[n_in:n_in + k]
        outs = refs[n_in + k:n_in + k + n_out]
        cast_out = refs[n_in + k + n_out:n_in + 2 * k + n_out]
        kernel_fn(*ins, *outs, *refs[n_in + 2 * k + n_out:])
        step = pl.program_id(0)
        for src_ref, dst_ref, ch in zip(cast_in, cast_out, chunks):
            if ch == n_steps:
                dst_ref[...] = src_ref[...].astype(BF)
            else:
                @pl.when(step < ch)
                def _(src_ref=src_ref, dst_ref=dst_ref):
                    dst_ref[...] = src_ref[...].astype(BF)

    return kernel, in_specs, args, out_specs, out_shape


def _params(n_grid, vmem_limit=None):
    return pltpu.CompilerParams(dimension_semantics=("arbitrary",) * n_grid,
                                vmem_limit_bytes=vmem_limit or VMEM_LIMIT)


def _norm_mod(x, g, shift, scale):
    hn = x * lax.rsqrt(jnp.mean(x * x, axis=-1, keepdims=True) + EPS) * g
    return hn * (1.0 + scale) + shift


def _adaln_kernel(cb_ref, w_ref, b_ref, o_ref):
    for r in range(3):
        cv = cb_ref[r]
        s = _silu(cv)
        for j in range(ADA_TN // LANES):
            w = w_ref[:, j * LANES:(j + 1) * LANES]
            o_ref[r:r + 1, j * LANES:(j + 1) * LANES] = (
                jnp.sum(w * s, axis=0, keepdims=True) + b_ref[:, j * LANES:(j + 1) * LANES])


def _adaln(cb, w_ada, b_ada):
    nmod = N_MOD * D
    return pl.pallas_call(
        _adaln_kernel,
        grid=(DEPTH, nmod // ADA_TN),
        in_specs=[
            pl.BlockSpec((3, D, LANES), lambda l, j: (0, 0, 0)),
            pl.BlockSpec((None, D, ADA_TN), lambda l, j: (l, 0, j)),
            pl.BlockSpec((None, 1, ADA_TN), lambda l, j: (l, 0, j)),
        ],
        out_specs=pl.BlockSpec((None, 3, ADA_TN), lambda l, j: (l, 0, j)),
        out_shape=jax.ShapeDtypeStruct((DEPTH, 3, nmod), F32),
        compiler_params=_params(2),
        name="adaln",
    )(cb, w_ada, b_ada.reshape(DEPTH, 1, nmod))


def _ffn_kernel(*refs, tail):
    s_ref, g_ref, sh_ref, sc_ref, gt_ref, w1_ref, w3_ref, w2_ref = refs[:8]
    rest = refs[8:]
    s = s_ref[...]
    h = _norm_mod(s, g_ref[...], sh_ref[...], sc_ref[...]).astype(BF)
    acc = jnp.zeros(s.shape, F32)
    for j in range(D_FF // FF_CHUNK):
        cols = slice(j * FF_CHUNK, (j + 1) * FF_CHUNK)
        a = _dot(h, w1_ref[:, cols])
        b = _dot(h, w3_ref[:, cols])
        u = (_silu(a) * b).astype(BF)
        acc = acc + _dot(u, w2_ref[cols, :])
    out = s + 0.5 * gt_ref[...] * acc
    if tail is None:
        rest[0][...] = out
    elif tail == "final":
        gf_ref, o_ref = rest
        o_ref[...] = out * lax.rsqrt(jnp.mean(out * out, axis=-1, keepdims=True) + EPS) * gf_ref[...]
    else:
        (g2_ref, sh2_ref, sc2_ref, wq_ref, wkT_ref, wv_ref, wa_ref, waT_ref) = rest[:8]
        rest = rest[8:]
        if tail == "prep_fnet":
            wfn_ref, ccs_ref = rest[:2]
            rest = rest[2:]
        o_ref, h_ref, q_ref, kT_ref, v_ref, a_ref, aT_ref = rest[:7]
        o_ref[...] = out
        h2 = _norm_mod(out, g2_ref[...], sh2_ref[...], sc2_ref[...]).astype(BF)
        h_ref[...] = h2
        q_ref[...] = (_dot(h2, wq_ref[...]) * (DK ** -0.5)).astype(BF)
        kT_ref[...] = _dot_nt(wkT_ref[...], h2).astype(BF)
        v_ref[...] = _dot(h2, wv_ref[...]).astype(BF)
        a_ref[...] = _dot(h2, wa_ref[...]).astype(BF)
        aT_ref[...] = _dot_nt(waT_ref[...], h2).astype(BF)
        if tail == "prep_fnet":
            wr_ref, wi_ref = rest[7:]
            fn = _dot(h2, wfn_ref[...]).astype(BF)
            for g in range(NGROUP):
                pq = _dot(fn[:, g * GC:(g + 1) * GC], ccs_ref[...])
                wr_ref[:, g * GC:(g + 1) * GC] = pq[:, :GC]
                wi_ref[:, g * GC:(g + 1) * GC] = pq[:, GC:]


def _mod_spec(tiles_per_batch, per_batch):
    if per_batch:
        return pl.BlockSpec((None, 1, D), lambda i: (i // tiles_per_batch, 0, 0))
    return pl.BlockSpec((None, 1, D), lambda i: (0, 0, 0))


def _ffn(s, g, shift, scale, gate, w1, w3, w2, *, tm, tiles_per_batch, per_batch,
         g_final=None, prep=None, casts=()):
    n = s.shape[0]
    ms = _mod_spec(tiles_per_batch, per_batch)
    rows = lambda width: pl.BlockSpec((tm, width), lambda i: (i, 0))
    cols = lambda height: pl.BlockSpec((height, tm), lambda i: (0, i))
    in_specs = [
        rows(D), _const_spec((1, D)), ms, ms, ms,
        _const_spec((D, D_FF)), _const_spec((D, D_FF)), _const_spec((D_FF, D)),
    ]
    args = [s, g, shift, scale, gate, w1, w3, w2]
    out_specs = [rows(D)]
    out_shape = [jax.ShapeDtypeStruct((n, D), F32)]
    tail = None
    if g_final is not None:
        tail = "final"
        in_specs.append(_const_spec((1, D)))
        args.append(g_final)
    elif prep is not None:
        g2, shift2, scale2, w, with_fnet = prep
        tail = "prep_fnet" if with_fnet else "prep"
        in_specs += [_const_spec((1, D)), ms, ms,
                     _const_spec((D, HK)), _const_spec((HK, D)), _const_spec((D, BW)),
                     _const_spec((D, LANES)), _const_spec((2 * RANK, D))]
        args += [g2, shift2, scale2, w["q"], w["kT"], w["v"], w["a"], w["aT"]]
        out_specs += [rows(D), rows(HK), cols(HK), rows(BW), rows(LANES), cols(2 * RANK)]
        out_shape += [
            jax.ShapeDtypeStruct((n, D), BF), jax.ShapeDtypeStruct((n, HK), BF),
            jax.ShapeDtypeStruct((HK, n), BF), jax.ShapeDtypeStruct((n, BW), BF),
            jax.ShapeDtypeStruct((n, LANES), BF), jax.ShapeDtypeStruct((2 * RANK, n), BF),
        ]
        if with_fnet:
            in_specs += [_const_spec((D, BW)), _const_spec((GC, 2 * GC))]
            args += [w["fn"], w["ccs"]]
            out_specs += [rows(BW), rows(BW)]
            out_shape += [jax.ShapeDtypeStruct((n, BW), F32)] * 2
    body = functools.partial(_ffn_kernel, tail=tail)
    n_main = len(out_shape)
    if casts:
        body, c_in, c_args, c_out, c_shape = _with_casts(body, len(args), n_main, casts, n // tm)
        in_specs, args = in_specs + c_in, args + c_args
        out_specs, out_shape = out_specs + c_out, out_shape + c_shape
    res = pl.pallas_call(
        body,
        grid=(n // tm,),
        in_specs=in_specs,
        out_specs=out_specs,
        out_shape=out_shape,
        compiler_params=_params(1),
        name="ffn" if tail is None else "ffn_" + tail,
    )(*args)
    main = list(res[:n_main]) if prep is not None else [res[0]]
    if casts:
        main.append(list(res[n_main:]))
    return main if len(main) > 1 else main[0]


def _bd_mask():
    r = lax.broadcasted_iota(jnp.int32, (2 * DK, 2 * DV), 0) // DK
    c = lax.broadcasted_iota(jnp.int32, (2 * DK, 2 * DV), 1) // DV
    return r == c


def _gla_consts():
    c = GLA_C
    t = np.arange(c)
    sub = t // GLA_SUB
    lc = np.stack([t[None, :] <= t[:, None], t[None, :] >= t[:, None]]).astype(np.float32)
    tt, jj = t[:, None], t[None, :]
    sub_end = (sub * GLA_SUB + GLA_SUB - 1)[None, :]
    sub_start = (sub * GLA_SUB)[None, :]
    mk_f = (tt > jj) & (tt <= sub_end)
    mk_b = (tt < jj) & (tt >= sub_start)
    me_f = tt > jj
    me_b = tt < jj
    ones = np.ones((c, c), bool)
    mcat = np.stack([np.concatenate([mk_f, me_f, ones], axis=1),
                     np.concatenate([mk_b, me_b, ones], axis=1)]).astype(np.float32)
    return jnp.asarray(lc, BF), jnp.asarray(mcat, BF)


def _gla_direction(dr, q_ref, kT_ref, v_ref, a_ref, aT_ref, wa2_ref, wa2T_ref, ba2_ref, ba2T_ref,
                   lc_ref, mcat_ref, state_ref, o_ref, nchunk):
    c = GLA_C
    inv = 1.0 / GLA_NORMALIZER
    g = (_log_sigmoid(_dot(a_ref[...], wa2_ref[dr]) + ba2_ref[dr]) * inv).astype(BF)
    gT = (_log_sigmoid(_dot(wa2T_ref[dr], aT_ref[...]) + ba2T_ref[dr]) * inv).astype(BF)
    col_sub = lax.broadcasted_iota(jnp.int32, (2 * DK, c), 1) // GLA_SUB
    ii = lax.broadcasted_iota(jnp.int32, (c, c), 0)
    jj = lax.broadcasted_iota(jnp.int32, (c, c), 1)
    tri = (jj <= ii) if dr == 0 else (jj >= ii)
    low_half = lax.broadcasted_iota(jnp.int32, (c, LANES), 1) < DK
    bd = _bd_mask()
    zero_v = jnp.zeros((c, DV), BF)
    state = [state_ref[dr, 0], state_ref[dr, 1]]
    order = range(nchunk) if dr == 0 else range(nchunk - 1, -1, -1)
    sums = []
    for ci in order:
        rows = slice(ci * c, (ci + 1) * c)
        sums.append((_dot(lc_ref[dr], g[rows]),
                     _dot(gT[:, rows], mcat_ref[dr])))
    yield
    operands = []
    for ci, (b, fm) in zip(order, sums):
        rows = slice(ci * c, (ci + 1) * c)
        kc = kT_ref[:, rows].astype(F32)
        ksub = (kc * jnp.exp(fm[:, 0:c])).astype(BF)
        kend = (kc * jnp.exp(fm[:, c:2 * c])).astype(BF)
        dec = jnp.exp(fm[:, 2 * c:3 * c])
        qc = q_ref[rows, :].astype(F32)
        qb = (qc * jnp.exp(b)).astype(BF)
        for p in range(2):
            lanes = slice(p * LANES, (p + 1) * LANES)
            qp = qc[:, lanes]
            bp = b[:, lanes]
            kp = ksub[lanes]
            qblocks = []
            kblocks = []
            for j in range(GLA_NSUB):
                if dr == 0:
                    ref_row, lo, hi = j * GLA_SUB + GLA_SUB - 1, j * GLA_SUB, c
                else:
                    ref_row, lo, hi = j * GLA_SUB, 0, (j + 1) * GLA_SUB
                live = qp[lo:hi] * jnp.exp(bp[lo:hi] - bp[ref_row:ref_row + 1, :])
                parts = []
                if lo > 0:
                    parts.append(jnp.zeros((lo, LANES), F32))
                parts.append(live)
                if hi < c:
                    parts.append(jnp.zeros((c - hi, LANES), F32))
                qblocks.append(jnp.concatenate(parts, axis=0) if len(parts) > 1 else live)
                kblocks.append(jnp.where(col_sub == j, kp, jnp.zeros_like(kp)))
            hats = []
            for hd in range(2):
                qh = []
                for m in range(GLA_NSUB // 2):
                    even, odd = qblocks[2 * m], qblocks[2 * m + 1]
                    if hd == 0:
                        qh.append(jnp.where(low_half, even, pltpu.roll(odd, DK, 1)))
                    else:
                        qh.append(jnp.where(low_half, pltpu.roll(even, DK, 1), odd))
                hats.append((jnp.concatenate(qh, axis=1).astype(BF),
                             jnp.concatenate([kj[hd * DK:(hd + 1) * DK] for kj in kblocks], axis=0)))
            dp = dec[lanes]
            operands.append((ci, p, hats, qb[:, lanes], kend[lanes],
                             jnp.concatenate([dp, dp], axis=1)))
    yield
    scores = []
    for ci, p, hats, qbp, kendp, decay in operands:
        heads = [jnp.where(tri, _dot(qhat, khat), 0.0).astype(BF) for qhat, khat in hats]
        scores.append(jnp.concatenate(heads, axis=1))
    yield
    pending = []
    for (ci, p, hats, qbp, kendp, decay), att in zip(operands, scores):
        vp = v_ref[ci * c:(ci + 1) * c, p * 2 * DV:(p + 1) * 2 * DV]
        vbd = jnp.concatenate([
            jnp.concatenate([vp[:, :DV], zero_v], axis=1),
            jnp.concatenate([zero_v, vp[:, DV:]], axis=1)], axis=0)
        pending.append((ci, p, _dot(att, vbd), qbp, decay, jnp.where(bd, _dot(kendp, vp), 0.0)))
    yield
    for ci, p, intra, qbp, decay, upd in pending:
        o_p = intra + _dot(qbp, state[p].astype(BF))
        o_ref[ci * c:(ci + 1) * c, p * 2 * DV:(p + 1) * 2 * DV] = o_p.astype(BF)
        state[p] = decay * state[p] + upd
    state_ref[dr, 0] = state[0]
    state_ref[dr, 1] = state[1]


def _gla_kernel(qf_ref, kTf_ref, vf_ref, af_ref, aTf_ref, qb_ref, kTb_ref, vb_ref, ab_ref, aTb_ref,
                s0_ref, wa2_ref, wa2T_ref, ba2_ref, ba2T_ref, lc_ref, mcat_ref,
                of_ref, ob_ref, sfin_ref, state_ref, *, tm, tiles_per_batch):
    @pl.when(pl.program_id(0) % tiles_per_batch == 0)
    def _():
        state_ref[...] = s0_ref[...]

    shared = (wa2_ref, wa2T_ref, ba2_ref, ba2T_ref, lc_ref, mcat_ref, state_ref)
    scans = [
        _gla_direction(0, qf_ref, kTf_ref, vf_ref, af_ref, aTf_ref, *shared, of_ref, tm // GLA_C),
        _gla_direction(1, qb_ref, kTb_ref, vb_ref, ab_ref, aTb_ref, *shared, ob_ref, tm // GLA_C),
    ]
    live = list(scans)
    while live:
        live = [scan for scan in live if next(scan, "done") != "done"]
    sfin_ref[...] = state_ref[...]


def _gla(q, kT, v, a, aT, s0, w, *, tm, tiles_per_batch, casts=()):
    n = q.shape[0]
    nt = n // tm
    batch = nt // tiles_per_batch
    lc, mcat = _gla_consts()
    tpb = tiles_per_batch
    fwd = lambda i: i
    bwd = lambda i: (i // tpb) * tpb + (tpb - 1 - i % tpb)

    def tile_specs(idx):
        return [
            pl.BlockSpec((tm, HK), lambda i: (idx(i), 0)),
            pl.BlockSpec((HK, tm), lambda i: (0, idx(i))),
            pl.BlockSpec((tm, BW), lambda i: (idx(i), 0)),
            pl.BlockSpec((tm, LANES), lambda i: (idx(i), 0)),
            pl.BlockSpec((2 * RANK, tm), lambda i: (0, idx(i))),
        ]

    state_block = (None, 2, 2, 2 * DK, 2 * DV)
    in_specs = tile_specs(fwd) + tile_specs(bwd) + [
        pl.BlockSpec(state_block, lambda i: (i // tpb, 0, 0, 0, 0)),
        _const_spec((2, LANES, HK)), _const_spec((2, HK, 2 * RANK)),
        _const_spec((2, 1, HK)), _const_spec((2, HK, 1)),
        _const_spec((2, GLA_C, GLA_C)), _const_spec((2, GLA_C, 3 * GLA_C)),
    ]
    out_specs = [
        pl.BlockSpec((tm, BW), lambda i: (fwd(i), 0)),
        pl.BlockSpec((tm, BW), lambda i: (bwd(i), 0)),
        pl.BlockSpec(state_block, lambda i: (i // tpb, 0, 0, 0, 0)),
    ]
    out_shape = [
        jax.ShapeDtypeStruct((n, BW), BF), jax.ShapeDtypeStruct((n, BW), BF),
        jax.ShapeDtypeStruct((batch, 2, 2, 2 * DK, 2 * DV), F32),
    ]
    tiles = (q, kT, v, a, aT)
    args = [*tiles, *tiles, s0, w["a2"], w["a2T"], w["ba2"], w["ba2T"], lc, mcat]
    body = functools.partial(_gla_kernel, tm=tm, tiles_per_batch=tpb)
    if casts:
        body, c_in, c_args, c_out, c_shape = _with_casts(body, len(args), 3, casts, nt)
        in_specs, args = in_specs + c_in, args + c_args
        out_specs, out_shape = out_specs + c_out, out_shape + c_shape
    res = pl.pallas_call(
        body,
        grid=(nt,),
        in_specs=in_specs,
        out_specs=out_specs,
        out_shape=out_shape,
        scratch_shapes=[pltpu.VMEM((2, 2, 2 * DK, 2 * DV), F32)],
        compiler_params=_params(1),
        name="gla",
    )(*args)
    return (*res[:3], list(res[3:])) if casts else res


def _dft_cos_sin(n):
    k = np.arange(n)
    ang = 2.0 * np.pi * ((k[:, None] * k[None, :]) % n) / n
    return np.cos(ang), np.sin(ang)


def _fnet_kernel(x_ref, g_ref, sh_ref, sc_ref, wfn_ref, ccs_ref, f1_ref, tc_ref, ts_ref, f2_ref,
                 o_ref, zr_scr, zi_scr):
    j = pl.program_id(1)
    n_stage1 = FFT_N2 // FFT_SUB

    @pl.when(j < n_stage1)
    def _():
        rows = FFT_N1 * FFT_SUB
        x = x_ref[...].reshape(rows, D)
        h = _norm_mod(x, g_ref[...], sh_ref[...], sc_ref[...]).astype(BF)
        fn = _dot(h, wfn_ref[...]).astype(BF)
        re, im = [], []
        for g in range(NGROUP):
            pq = _dot(fn[:, g * GC:(g + 1) * GC], ccs_ref[...])
            re.append(pq[:, :GC].astype(BF))
            im.append(pq[:, GC:].astype(BF))
        w = jnp.concatenate([jnp.concatenate(re, axis=1), jnp.concatenate(im, axis=1)], axis=0)
        y = _dot(f1_ref[...], w)
        yr, yi = y[:rows], y[rows:]
        tc = jnp.concatenate([tc_ref[...]] * (BW // LANES), axis=1)
        ts = jnp.concatenate([ts_ref[...]] * (BW // LANES), axis=1)
        group = pl.ds(pl.multiple_of(j * FFT_SUB, FFT_SUB), FFT_SUB)
        zr_scr[:, group, :] = (yr * tc + yi * ts).reshape(FFT_N1, FFT_SUB, BW)
        zi_scr[:, group, :] = (yi * tc - yr * ts).reshape(FFT_N1, FFT_SUB, BW)

    @pl.when(j >= n_stage1)
    def _():
        base = (j - n_stage1) * FFT_K1_BLOCK
        for k in range(FFT_K1_BLOCK):
            z = jnp.concatenate([zr_scr[base + k].astype(BF), zi_scr[base + k].astype(BF)], axis=0)
            o_ref[k] = _dot(f2_ref[...], z)


def _fft2_kernel(zr_ref, zi_ref, f_ref, o_ref):
    for k in range(zr_ref.shape[0]):
        z = jnp.concatenate([zr_ref[k].astype(BF), zi_ref[k].astype(BF)], axis=0)
        o_ref[k] = _dot(f_ref[...], z)


def _fnet_latent(x, g, shift, scale, w, batch):
    n1, n2, sub = FFT_N1, FFT_N2, FFT_SUB
    rows = n1 * sub
    c1, s1 = _dft_cos_sin(n1)
    eye = np.eye(sub)
    f1 = np.block([[np.kron(c1, eye), np.kron(s1, eye)],
                   [np.kron(-s1, eye), np.kron(c1, eye)]]) / math.sqrt(n1)
    k1 = np.arange(n1)[None, :, None]
    m2 = np.arange(n2).reshape(n2 // sub, 1, sub)
    ang = (2.0 * np.pi * (k1 * m2) / (n1 * n2)).reshape(n2 // sub, rows, 1)
    tc = np.broadcast_to(np.cos(ang), (n2 // sub, rows, LANES)).astype(np.float32)
    ts = np.broadcast_to(np.sin(ang), (n2 // sub, rows, LANES)).astype(np.float32)
    c2, s2 = _dft_cos_sin(n2)
    f2 = np.concatenate([c2, s2], axis=1) / math.sqrt(n2 * GC)
    s1_steps, s2_steps = n2 // sub, n1 // FFT_K1_BLOCK
    group = lambda j: jnp.minimum(j, s1_steps - 1)
    mod = pl.BlockSpec((None, 1, D), lambda b, j: (b, 0, 0))
    tw = pl.BlockSpec((None, rows, LANES), lambda b, j: (group(j), 0, 0))
    return pl.pallas_call(
        _fnet_kernel,
        grid=(batch, s1_steps + s2_steps),
        in_specs=[pl.BlockSpec((None, n1, sub, D), lambda b, j: (b, 0, group(j), 0)),
                  _const_spec((1, D)), mod, mod, _const_spec((D, BW)), _const_spec((GC, 2 * GC)),
                  _const_spec((2 * rows, 2 * rows)), tw, tw, _const_spec((n2, 2 * n2))],
        out_specs=pl.BlockSpec((None, FFT_K1_BLOCK, n2, BW),
                               lambda b, j: (b, jnp.maximum(j - s1_steps, 0), 0, 0)),
        out_shape=jax.ShapeDtypeStruct((batch, n1, n2, BW), F32),
        scratch_shapes=[pltpu.VMEM((n1, n2, BW), F32), pltpu.VMEM((n1, n2, BW), F32)],
        compiler_params=_params(2, FNET_VMEM_LIMIT),
        name="fnet",
    )(x.reshape(batch, n1, n2, D), g, shift, scale, w["fn"], w["ccs"], _bf16_const(f1),
      jnp.asarray(tc), jnp.asarray(ts), _bf16_const(f2))


def _fft_direct(wr, wi, batch, seq):
    c2, s2 = _dft_cos_sin(seq)
    f2 = np.concatenate([c2, s2], axis=1) / math.sqrt(seq * GC)
    zblk = pl.BlockSpec((1, seq, BW), lambda b: (b, 0, 0))
    y = pl.pallas_call(
        _fft2_kernel,
        grid=(batch,),
        in_specs=[zblk, zblk, _const_spec((seq, 2 * seq))],
        out_specs=zblk,
        out_shape=jax.ShapeDtypeStruct((batch, seq, BW), F32),
        compiler_params=_params(1),
        name="fft_direct",
    )(wr.reshape(batch, seq, BW), wi.reshape(batch, seq, BW), _bf16_const(f2))
    return y.reshape(batch * seq, BW)


def _mix_kernel(*refs, tm, row_len, k1_major):
    if k1_major:
        perm_ref, refs = refs[0], refs[1:]
    (x_ref, h_ref, of_ref, ob_ref, yf_ref, gt_ref, wr_ref, go_ref, wsu_ref, wsv_ref,
     wcb_ref, wcc_ref, wcx_ref, wsgu_ref, bsgu_ref, wconv_ref, wbr_ref, wgate_ref,
     bgate_ref, wout_ref, o_ref, br_ref) = refs
    h = h_ref[...]
    if k1_major:
        yf = _dot(perm_ref[...], yf_ref[...].reshape(tm, BW).astype(BF)).astype(BF)
    else:
        yf = yf_ref[...].astype(BF)

    r = _dot(h, wr_ref[...])
    su = _dot(h, wsu_ref[...])
    sv = _dot(h, wsv_ref[...])
    zc = _dot(h, wcc_ref[...]) * _dot(h, wcx_ref[...])
    cb = _dot(h, wcb_ref[...])

    gla = []
    for hd in range(HEADS):
        lanes = slice(hd * DV, (hd + 1) * DV)
        oh = of_ref[:, lanes].astype(F32) + ob_ref[:, lanes].astype(F32)
        on = oh * lax.rsqrt(jnp.mean(oh * oh, axis=-1, keepdims=True) + EPS) * go_ref[:, lanes]
        gla.append((on * _silu(r[:, lanes])).astype(BF))
    for g in range(NGROUP):
        lanes = slice(g * GC, (g + 1) * GC)
        vg = sv[:, lanes]
        vc = vg - jnp.mean(vg, axis=-1, keepdims=True)
        z = (vc * lax.rsqrt(jnp.mean(vc * vc, axis=-1, keepdims=True) + EPS)).astype(BF)
        for n in range(tm // SGU_CHUNK):
            rows = slice(n * SGU_CHUNK, (n + 1) * SGU_CHUNK)
            s = _dot(wsgu_ref[g], z[rows]) + bsgu_ref[:, lanes]
            br_ref[rows, lanes] = (su[rows, lanes] * s).astype(BF)
    pos = lax.broadcasted_iota(jnp.int32, (tm, BW), 0) % row_len
    left = jnp.where(pos == 0, 0.0, pltpu.roll(zc, 1, 0))
    right = jnp.where(pos == row_len - 1, 0.0, pltpu.roll(zc, tm - 1, 0))
    y = left * wconv_ref[0:1, :] + zc * wconv_ref[1:2, :] + right * wconv_ref[2:3, :]
    branches = (jnp.concatenate(gla, axis=1), br_ref[...], yf, (cb * y).astype(BF))

    m = None
    for k, branch in enumerate(branches):
        gate = _sigmoid(_dot(h, wgate_ref[k]) + bgate_ref[k])
        term = gate * _dot(branch, wbr_ref[k])
        m = term if m is None else m + term
    o_ref[...] = x_ref[...] + gt_ref[...] * _dot(m.astype(BF), wout_ref[...])


def _mix(x, h, of, ob, yf, gate, w, *, tm, tiles_per_batch, per_batch, row_len):
    n = x.shape[0]
    layer = (w["layer"],)
    ms = _mod_spec(tiles_per_batch, per_batch)
    tile = lambda width: pl.BlockSpec((tm, width), lambda i: (i, 0))
    k1_major = yf.ndim == 4
    pre_specs, pre_args = [], []
    if k1_major:
        k2_tile = tm // FFT_N1
        assert tm % FFT_N1 == 0 and k2_tile % 8 == 0
        tpb = tiles_per_batch
        yf_spec = pl.BlockSpec((None, FFT_N1, k2_tile, BW), lambda i: (i // tpb, 0, i % tpb, 0))
        r = np.arange(tm)
        src = (r % FFT_N1) * k2_tile + r // FFT_N1
        perm = np.zeros((tm, tm), np.float32)
        perm[r, src] = 1.0
        pre_specs, pre_args = [_const_spec((tm, tm))], [jnp.asarray(perm, BF)]
    else:
        yf_spec = tile(BW)
    in_specs = pre_specs + [
        tile(D), tile(D), tile(BW), tile(BW), yf_spec, ms,
        _const_spec((D, BW)), _const_spec((1, BW)),
        _const_spec((D, BW)), _const_spec((D, BW)), _const_spec((D, BW)), _const_spec((D, BW)),
        _const_spec((D, BW)),
        _pick_spec(layer, (NGROUP, SGU_CHUNK, SGU_CHUNK)), _const_spec((SGU_CHUNK, BW)),
        _const_spec((3, BW)), _const_spec((4, BW, D)), _const_spec((4, D, D)),
        _const_spec((4, 1, D)), _const_spec((D, D)),
    ]
    return pl.pallas_call(
        functools.partial(_mix_kernel, tm=tm, row_len=row_len, k1_major=k1_major),
        grid=(n // tm,),
        in_specs=in_specs,
        out_specs=tile(D),
        out_shape=jax.ShapeDtypeStruct((n, D), F32),
        scratch_shapes=[pltpu.VMEM((tm, BW), BF)],
        compiler_params=_params(1),
        name="mix",
    )(*pre_args, x, h, of, ob, yf, gate, w["r"], w["go"], w["su"], w["sv"], w["cb"], w["cc"], w["cx"],
      w["sgu"], w["bsgu"], w["conv"], w["branch"], w["gate"], w["bgate"], w["out"])


_IN_EDGES = np.cumsum([0, HK, HK, BW, BW, 2 * RANK, BW, BW, BW, BW, BW, BW])
_IN_NAMES = ("q", "k", "v", "r", "a", "su", "sv", "fn", "cb", "cc", "cx")


def _in_columns(w_in_layer):
    return {name: w_in_layer[:, _IN_EDGES[j]:_IN_EDGES[j + 1]].astype(BF)
            for j, name in enumerate(_IN_NAMES)}


def _layer_weights(i, w_in_layer, w_gla_a2, b_gla_a2, g_gla_norm, w_sgu, b_sgu, w_conv, w_branch,
                   w_gate, b_gate, w_out):
    cols = _in_columns(w_in_layer)
    q, k, v, r, a, su, sv, fn, cb, cc, cx = [cols[name] for name in _IN_NAMES]
    a2 = w_gla_a2[i].astype(BF)
    a2_tok = jnp.zeros((2, LANES, HK), BF)
    a2_tok = a2_tok.at[0, :RANK].set(a2[0]).at[1, RANK:2 * RANK].set(a2[1])
    cc_, sc_ = _dft_cos_sin(GC)
    return {
        "q": q, "kT": k.T, "v": v, "r": r,
        "a": jnp.pad(a, ((0, 0), (0, LANES - 2 * RANK))), "aT": a.T,
        "a2": a2_tok, "a2T": jnp.transpose(a2_tok[:, :2 * RANK, :], (0, 2, 1)),
        "ba2": b_gla_a2[i][:, None, :], "ba2T": b_gla_a2[i][:, :, None],
        "go": g_gla_norm[i].reshape(1, BW),
        "su": su, "sv": sv, "fn": fn, "cb": cb, "cc": cc, "cx": cx,
        "ccs": _bf16_const(np.concatenate([cc_, -sc_], axis=1)),
        "layer": i,
        "sgu": w_sgu.astype(BF),
        "merge_casts": (_cast_job(w_gate.reshape(DEPTH, 4 * D, D), (i,), 4 * D, D),
                        _cast_job(w_branch.reshape(DEPTH, 4 * BW, D), (i,), 4 * BW, D),
                        _cast_job(w_out, (i,), D, D)),
        "bsgu": jnp.repeat(jnp.transpose(b_sgu[i]), GC, axis=1),
        "conv": w_conv[i],
        "bgate": b_gate[i][:, None, :],
    }


def kernel(x, c, ctx, c_ctx, w_ada, b_ada, g_norm, w_ff1, w_ff3, w_ff2, w_in, w_gla_a2, b_gla_a2,
           g_gla_norm, w_sgu, b_sgu, w_conv, w_branch, w_gate, b_gate, w_out, g_final):
    batch, seq, _ = x.shape
    ctx_len = ctx.shape[1]
    assert seq % TM_LAT == 0 and TM_LAT % GRID_W == 0 and seq == FFT_N1 * FFT_N2
    assert seq % TM_FFN == 0
    assert ctx_len % GLA_C == 0
    tpb = seq // TM_LAT

    cvec = jnp.concatenate([c, c_ctx[None, :]], axis=0)
    cb = jnp.broadcast_to(cvec[:, :, None], (3, D, LANES))
    mods = _adaln(cb, w_ada, b_ada).reshape(DEPTH, 3, N_MOD, 1, D)

    xs = x.reshape(batch * seq, D)
    cs = ctx.reshape(batch * ctx_len, D)
    lat = dict(tm=TM_LAT, tiles_per_batch=tpb, per_batch=True)
    lat_ffn = dict(tm=TM_FFN, tiles_per_batch=seq // TM_FFN, per_batch=True)
    cx_ = dict(tm=batch * ctx_len, tiles_per_batch=1, per_batch=False)
    s_zero = jnp.zeros((batch, 2, 2, 2 * DK, 2 * DV), F32)

    def ff_casts(layer, half):
        lead = (layer, half)
        return (_cast_job(w_ff1, lead, D, D_FF), _cast_job(w_ff3, lead, D, D_FF),
                _cast_job(w_ff2, lead, D_FF, D))

    ff = [w_ff1[0, 0].astype(BF), w_ff3[0, 0].astype(BF), w_ff2[0, 0].astype(BF)]

    for i in range(DEPTH):
        last = i == DEPTH - 1
        ml = [mods[i, 0:2, j] for j in range(N_MOD)]
        mc = [mods[i, 2:3, j] for j in range(N_MOD)]
        gn = g_norm[i][:, None, :]
        w = _layer_weights(i, w_in[i], w_gla_a2, b_gla_a2, g_gla_norm, w_sgu, b_sgu, w_conv,
                           w_branch, w_gate, b_gate, w_out)

        xs, hl, *gla_l, merge_w = _ffn(xs, gn[0], ml[0], ml[1], ml[2], *ff,
                                       prep=(gn[1], ml[3], ml[4], w, False),
                                       casts=w["merge_casts"], **lat_ffn)
        w["gate"] = merge_w[0].reshape(4, D, D)
        w["branch"] = merge_w[1].reshape(4, BW, D)
        w["out"] = merge_w[2]
        cs, hc, *gla_c, wrc, wic = _ffn(cs, gn[0], mc[0], mc[1], mc[2], *ff,
                                        prep=(gn[1], mc[3], mc[4], w, True), **cx_)

        ofc, obc, s_ctx = _gla(*gla_c, s_zero, w, tm=ctx_len, tiles_per_batch=1)
        ofl, obl, _, ff = _gla(*gla_l, s_ctx, w, tm=TM_LAT, tiles_per_batch=tpb,
                               casts=ff_casts(i, 1))
        yf = _fnet_latent(xs.reshape(batch, seq, D), gn[1], ml[3], ml[4], w, batch)
        xs = _mix(xs, hl, ofl, obl, yf, ml[5], w, row_len=GRID_W, **lat)
        if last:
            xs = _ffn(xs, gn[2], ml[6], ml[7], ml[8], *ff, g_final=g_final[None, :], **lat_ffn)
        else:
            yfc = _fft_direct(wrc, wic, batch, ctx_len)
            cs = _mix(cs, hc, ofc, obc, yfc, mc[5], w, row_len=ctx_len, **cx_)
            cs = _ffn(cs, gn[2], mc[6], mc[7], mc[8], *ff, **cx_)
            xs, ff = _ffn(xs, gn[2], ml[6], ml[7], ml[8], *ff, casts=ff_casts(i + 1, 0), **lat_ffn)
    return xs.reshape(batch, seq, D)
```

```python
import functools
import math

import numpy as np
import jax
import jax.numpy as jnp
from jax import lax
from jax.experimental import pallas as pl
from jax.experimental.pallas import tpu as pltpu

D = 1024
DEPTH = 2
GRID_W = 64
N_MOD = 9
D_FF = 2816
BW = 512
HEADS = 4
DV = 128
DK = 64
HK = HEADS * DK
RANK = 16
GLA_NORMALIZER = 16.0
SGU_CHUNK = 128
GC = 128
NGROUP = 4
EPS = 1e-6
LANES = 128

TM_LAT = 512
TM_FFN = 1024
FF_CHUNK = 256
GLA_C = 128
GLA_SUB = 32
GLA_NSUB = GLA_C // GLA_SUB
FFT_N1 = 64
FFT_N2 = 128
FFT_SUB = 8
FFT_K1_BLOCK = 8
ADA_TN = 2304
VMEM_LIMIT = 56 * 1024 * 1024
FNET_VMEM_LIMIT = 60 * 1024 * 1024

BF = jnp.bfloat16
F32 = jnp.float32

_NT = (((1,), (1,)), ((), ()))


def _dot(a, b):
    return jnp.dot(a, b, preferred_element_type=F32)


def _dot_nt(a, b):
    return lax.dot_general(a, b, _NT, preferred_element_type=F32)


def _sigmoid(x):
    return 1.0 / (1.0 + jnp.exp(-x))


def _silu(x):
    return x * _sigmoid(x)


def _log_sigmoid(x):
    return jnp.minimum(x, 0.0) - jnp.log(1.0 + jnp.exp(-jnp.abs(x)))


def _bf16_const(a):
    return jnp.asarray(a, F32).astype(BF)


def _const_spec(shape):
    nd = len(shape)
    return pl.BlockSpec(shape, lambda *_: (0,) * nd, pipeline_mode=pl.Buffered(1))


def _pick_spec(lead, shape):
    nd = len(shape)
    return pl.BlockSpec((None,) * len(lead) + tuple(shape), lambda *_: tuple(lead) + (0,) * nd,
                        pipeline_mode=pl.Buffered(1))


def _cast_job(src, lead, rows, cols):
    return (src, tuple(lead), rows, cols)


def _with_casts(kernel_fn, n_in, n_out, jobs, n_steps):
    bf16_rows = 16
    chunks = [max(k for k in range(1, n_steps + 1) if rows % (k * bf16_rows) == 0)
              for _, _, rows, _ in jobs]
    in_specs, args, out_specs, out_shape = [], [], [], []
    for (src, lead, rows, cols), ch in zip(jobs, chunks):
        r = rows // ch
        in_specs.append(pl.BlockSpec(
            (None,) * len(lead) + (r, cols),
            lambda i, lead=lead, ch=ch: lead + (jnp.minimum(i, ch - 1), 0)))
        out_specs.append(pl.BlockSpec((r, cols), lambda i, ch=ch: (jnp.minimum(i, ch - 1), 0)))
        args.append(src)
        out_shape.append(jax.ShapeDtypeStruct((rows, cols), BF))
    k = len(jobs)

    def kernel(*refs):
        ins, cast_in = refs[:n_in], refs[n_in:n_in + k]
        outs = refs[n_in + k:n_in + k + n_out]
        cast_out = refs[n_in + k + n_out:n_in + 2 * k + n_out]
        kernel_fn(*ins, *outs, *refs[n_in + 2 * k + n_out:])
        step = pl.program_id(0)
        for src_ref, dst_ref, ch in zip(cast_in, cast_out, chunks):
            if ch == n_steps:
                dst_ref[...] = src_ref[...].astype(BF)
            else:
                @pl.when(step < ch)
                def _(src_ref=src_ref, dst_ref=dst_ref):
                    dst_ref[...] = src_ref[...].astype(BF)

    return kernel, in_specs, args, out_specs, out_shape


def _params(n_grid, vmem_limit=None):
    return pltpu.CompilerParams(dimension_semantics=("arbitrary",) * n_grid,
                                vmem_limit_bytes=vmem_limit or VMEM_LIMIT)


def _norm_mod(x, g, shift, scale):
    hn = x * lax.rsqrt(jnp.mean(x * x, axis=-1, keepdims=True) + EPS) * g
    return hn * (1.0 + scale) + shift


def _adaln_kernel(cb_ref, w_ref, b_ref, o_ref):
    for r in range(3):
        cv = cb_ref[r]
        s = _silu(cv)
        for j in range(ADA_TN // LANES):
            w = w_ref[:, j * LANES:(j + 1) * LANES]
            o_ref[r:r + 1, j * LANES:(j + 1) * LANES] = (
                jnp.sum(w * s, axis=0, keepdims=True) + b_ref[:, j * LANES:(j + 1) * LANES])


def _adaln(cb, w_ada, b_ada):
    nmod = N_MOD * D
    return pl.pallas_call(
        _adaln_kernel,
        grid=(DEPTH, nmod // ADA_TN),
        in_specs=[
            pl.BlockSpec((3, D, LANES), lambda l, j: (0, 0, 0)),
            pl.BlockSpec((None, D, ADA_TN), lambda l, j: (l, 0, j)),
            pl.BlockSpec((None, 1, ADA_TN), lambda l, j: (l, 0, j)),
        ],
        out_specs=pl.BlockSpec((None, 3, ADA_TN), lambda l, j: (l, 0, j)),
        out_shape=jax.ShapeDtypeStruct((DEPTH, 3, nmod), F32),
        compiler_params=_params(2),
        name="adaln",
    )(cb, w_ada, b_ada.reshape(DEPTH, 1, nmod))


def _ffn_kernel(*refs, tail):
    s_ref, g_ref, sh_ref, sc_ref, gt_ref, w1_ref, w3_ref, w2_ref = refs[:8]
    rest = refs[8:]
    s = s_ref[...]
    h = _norm_mod(s, g_ref[...], sh_ref[...], sc_ref[...]).astype(BF)
    acc = jnp.zeros(s.shape, F32)
    for j in range(D_FF // FF_CHUNK):
        cols = slice(j * FF_CHUNK, (j + 1) * FF_CHUNK)
        a = _dot(h, w1_ref[:, cols])
        b = _dot(h, w3_ref[:, cols])
        u = (_silu(a) * b).astype(BF)
        acc = acc + _dot(u, w2_ref[cols, :])
    out = s + 0.5 * gt_ref[...] * acc
    if tail is None:
        rest[0][...] = out
    elif tail == "final":
        gf_ref, o_ref = rest
        o_ref[...] = out * lax.rsqrt(jnp.mean(out * out, axis=-1, keepdims=True) + EPS) * gf_ref[...]
    else:
        (g2_ref, sh2_ref, sc2_ref, wq_ref, wkT_ref, wv_ref, wa_ref, waT_ref) = rest[:8]
        rest = rest[8:]
        if tail == "prep_fnet":
            wfn_ref, ccs_ref = rest[:2]
            rest = rest[2:]
        o_ref, h_ref, q_ref, kT_ref, v_ref, a_ref, aT_ref = rest[:7]
        o_ref[...] = out
        h2 = _norm_mod(out, g2_ref[...], sh2_ref[...], sc2_ref[...]).astype(BF)
        h_ref[...] = h2
        q_ref[...] = (_dot(h2, wq_ref[...]) * (DK ** -0.5)).astype(BF)
        kT_ref[...] = _dot_nt(wkT_ref[...], h2).astype(BF)
        v_ref[...] = _dot(h2, wv_ref[...]).astype(BF)
        a_ref[...] = _dot(h2, wa_ref[...]).astype(BF)
        aT_ref[...] = _dot_nt(waT_ref[...], h2).astype(BF)
        if tail == "prep_fnet":
            wr_ref, wi_ref = rest[7:]
            fn = _dot(h2, wfn_ref[...]).astype(BF)
            for g in range(NGROUP):
                pq = _dot(fn[:, g * GC:(g + 1) * GC], ccs_ref[...])
                wr_ref[:, g * GC:(g + 1) * GC] = pq[:, :GC]
                wi_ref[:, g * GC:(g + 1) * GC] = pq[:, GC:]


def _mod_spec(tiles_per_batch, per_batch):
    if per_batch:
        return pl.BlockSpec((None, 1, D), lambda i: (i // tiles_per_batch, 0, 0))
    return pl.BlockSpec((None, 1, D), lambda i: (0, 0, 0))


def _ffn(s, g, shift, scale, gate, w1, w3, w2, *, tm, tiles_per_batch, per_batch,
         g_final=None, prep=None, casts=()):
    n = s.shape[0]
    ms = _mod_spec(tiles_per_batch, per_batch)
    rows = lambda width: pl.BlockSpec((tm, width), lambda i: (i, 0))
    cols = lambda height: pl.BlockSpec((height, tm), lambda i: (0, i))
    in_specs = [
        rows(D), _const_spec((1, D)), ms, ms, ms,
        _const_spec((D, D_FF)), _const_spec((D, D_FF)), _const_spec((D_FF, D)),
    ]
    args = [s, g, shift, scale, gate, w1, w3, w2]
    out_specs = [rows(D)]
    out_shape = [jax.ShapeDtypeStruct((n, D), F32)]
    tail = None
    if g_final is not None:
        tail = "final"
        in_specs.append(_const_spec((1, D)))
        args.append(g_final)
    elif prep is not None:
        g2, shift2, scale2, w, with_fnet = prep
        tail = "prep_fnet" if with_fnet else "prep"
        in_specs += [_const_spec((1, D)), ms, ms,
                     _const_spec((D, HK)), _const_spec((HK, D)), _const_spec((D, BW)),
                     _const_spec((D, LANES)), _const_spec((2 * RANK, D))]
        args += [g2, shift2, scale2, w["q"], w["kT"], w["v"], w["a"], w["aT"]]
        out_specs += [rows(D), rows(HK), cols(HK), rows(BW), rows(LANES), cols(2 * RANK)]
        out_shape += [
            jax.ShapeDtypeStruct((n, D), BF), jax.ShapeDtypeStruct((n, HK), BF),
            jax.ShapeDtypeStruct((HK, n), BF), jax.ShapeDtypeStruct((n, BW), BF),
            jax.ShapeDtypeStruct((n, LANES), BF), jax.ShapeDtypeStruct((2 * RANK, n), BF),
        ]
        if with_fnet:
            in_specs += [_const_spec((D, BW)), _const_spec((GC, 2 * GC))]
            args += [w["fn"], w["ccs"]]
            out_specs += [rows(BW), rows(BW)]
            out_shape += [jax.ShapeDtypeStruct((n, BW), F32)] * 2
    body = functools.partial(_ffn_kernel, tail=tail)
    n_main = len(out_shape)
    if casts:
        body, c_in, c_args, c_out, c_shape = _with_casts(body, len(args), n_main, casts, n // tm)
        in_specs, args = in_specs + c_in, args + c_args
        out_specs, out_shape = out_specs + c_out, out_shape + c_shape
    res = pl.pallas_call(
        body,
        grid=(n // tm,),
        in_specs=in_specs,
        out_specs=out_specs,
        out_shape=out_shape,
        compiler_params=_params(1),
        name="ffn" if tail is None else "ffn_" + tail,
    )(*args)
    main = list(res[:n_main]) if prep is not None else [res[0]]
    if casts:
        main.append(list(res[n_main:]))
    return main if len(main) > 1 else main[0]


def _bd_mask():
    r = lax.broadcasted_iota(jnp.int32, (2 * DK, 2 * DV), 0) // DK
    c = lax.broadcasted_iota(jnp.int32, (2 * DK, 2 * DV), 1) // DV
    return r == c


def _gla_consts():
    c = GLA_C
    t = np.arange(c)
    sub = t // GLA_SUB
    lc = np.stack([t[None, :] <= t[:, None], t[None, :] >= t[:, None]]).astype(np.float32)
    tt, jj = t[:, None], t[None, :]
    sub_end = (sub * GLA_SUB + GLA_SUB - 1)[None, :]
    sub_start = (sub * GLA_SUB)[None, :]
    mk_f = (tt > jj) & (tt <= sub_end)
    mk_b = (tt < jj) & (tt >= sub_start)
    me_f = tt > jj
    me_b = tt < jj
    ones = np.ones((c, c), bool)
    mcat = np.stack([np.concatenate([mk_f, me_f, ones], axis=1),
                     np.concatenate([mk_b, me_b, ones], axis=1)]).astype(np.float32)
    return jnp.asarray(lc, BF), jnp.asarray(mcat, BF)


def _gla_direction(dr, q_ref, kT_ref, v_ref, a_ref, aT_ref, wa2_ref, wa2T_ref, ba2_ref, ba2T_ref,
                   lc_ref, mcat_ref, state_ref, o_ref, nchunk):
    c = GLA_C
    inv = 1.0 / GLA_NORMALIZER
    g = (_log_sigmoid(_dot(a_ref[...], wa2_ref[dr]) + ba2_ref[dr]) * inv).astype(BF)
    gT = (_log_sigmoid(_dot(wa2T_ref[dr], aT_ref[...]) + ba2T_ref[dr]) * inv).astype(BF)
    col_sub = lax.broadcasted_iota(jnp.int32, (2 * DK, c), 1) // GLA_SUB
    ii = lax.broadcasted_iota(jnp.int32, (c, c), 0)
    jj = lax.broadcasted_iota(jnp.int32, (c, c), 1)
    tri = (jj <= ii) if dr == 0 else (jj >= ii)
    low_half = lax.broadcasted_iota(jnp.int32, (c, LANES), 1) < DK
    bd = _bd_mask()
    zero_v = jnp.zeros((c, DV), BF)
    state = [state_ref[dr, 0], state_ref[dr, 1]]
    order = range(nchunk) if dr == 0 else range(nchunk - 1, -1, -1)
    sums = []
    for ci in order:
        rows = slice(ci * c, (ci + 1) * c)
        sums.append((_dot(lc_ref[dr], g[rows]),
                     _dot(gT[:, rows], mcat_ref[dr])))
    yield
    operands = []
    for ci, (b, fm) in zip(order, sums):
        rows = slice(ci * c, (ci + 1) * c)
        kc = kT_ref[:, rows].astype(F32)
        ksub = (kc * jnp.exp(fm[:, 0:c])).astype(BF)
        kend = (kc * jnp.exp(fm[:, c:2 * c])).astype(BF)
        dec = jnp.exp(fm[:, 2 * c:3 * c])
        qc = q_ref[rows, :].astype(F32)
        qb = (qc * jnp.exp(b)).astype(BF)
        for p in range(2):
            lanes = slice(p * LANES, (p + 1) * LANES)
            qp = qc[:, lanes]
            bp = b[:, lanes]
            kp = ksub[lanes]
            qblocks = []
            kblocks = []
            for j in range(GLA_NSUB):
                if dr == 0:
                    ref_row, lo, hi = j * GLA_SUB + GLA_SUB - 1, j * GLA_SUB, c
                else:
                    ref_row, lo, hi = j * GLA_SUB, 0, (j + 1) * GLA_SUB
                live = qp[lo:hi] * jnp.exp(bp[lo:hi] - bp[ref_row:ref_row + 1, :])
                parts = []
                if lo > 0:
                    parts.append(jnp.zeros((lo, LANES), F32))
                parts.append(live)
                if hi < c:
                    parts.append(jnp.zeros((c - hi, LANES), F32))
                qblocks.append(jnp.concatenate(parts, axis=0) if len(parts) > 1 else live)
                kblocks.append(jnp.where(col_sub == j, kp, jnp.zeros_like(kp)))
            hats = []
            for hd in range(2):
                qh = []
                for m in range(GLA_NSUB // 2):
                    even, odd = qblocks[2 * m], qblocks[2 * m + 1]
                    if hd == 0:
                        qh.append(jnp.where(low_half, even, pltpu.roll(odd, DK, 1)))
                    else:
                        qh.append(jnp.where(low_half, pltpu.roll(even, DK, 1), odd))
                hats.append((jnp.concatenate(qh, axis=1).astype(BF),
                             jnp.concatenate([kj[hd * DK:(hd + 1) * DK] for kj in kblocks], axis=0)))
            dp = dec[lanes]
            operands.append((ci, p, hats, qb[:, lanes], kend[lanes],
                             jnp.concatenate([dp, dp], axis=1)))
    yield
    scores = []
    for ci, p, hats, qbp, kendp, decay in operands:
        heads = [jnp.where(tri, _dot(qhat, khat), 0.0).astype(BF) for qhat, khat in hats]
        scores.append(jnp.concatenate(heads, axis=1))
    yield
    pending = []
    for (ci, p, hats, qbp, kendp, decay), att in zip(operands, scores):
        vp = v_ref[ci * c:(ci + 1) * c, p * 2 * DV:(p + 1) * 2 * DV]
        vbd = jnp.concatenate([
            jnp.concatenate([vp[:, :DV], zero_v], axis=1),
            jnp.concatenate([zero_v, vp[:, DV:]], axis=1)], axis=0)
        pending.append((ci, p, _dot(att, vbd), qbp, decay, jnp.where(bd, _dot(kendp, vp), 0.0)))
    yield
    for ci, p, intra, qbp, decay, upd in pending:
        o_p = intra + _dot(qbp, state[p].astype(BF))
        o_ref[ci * c:(ci + 1) * c, p * 2 * DV:(p + 1) * 2 * DV] = o_p.astype(BF)
        state[p] = decay * state[p] + upd
    state_ref[dr, 0] = state[0]
    state_ref[dr, 1] = state[1]


def _gla_kernel(qf_ref, kTf_ref, vf_ref, af_ref, aTf_ref, qb_ref, kTb_ref, vb_ref, ab_ref, aTb_ref,
                s0_ref, wa2_ref, wa2T_ref, ba2_ref, ba2T_ref, lc_ref, mcat_ref,
                of_ref, ob_ref, sfin_ref, state_ref, *, tm, tiles_per_batch):
    @pl.when(pl.program_id(0) % tiles_per_batch == 0)
    def _():
        state_ref[...] = s0_ref[...]

    shared = (wa2_ref, wa2T_ref, ba2_ref, ba2T_ref, lc_ref, mcat_ref, state_ref)
    scans = [
        _gla_direction(0, qf_ref, kTf_ref, vf_ref, af_ref, aTf_ref, *shared, of_ref, tm // GLA_C),
        _gla_direction(1, qb_ref, kTb_ref, vb_ref, ab_ref, aTb_ref, *shared, ob_ref, tm // GLA_C),
    ]
    live = list(scans)
    while live:
        live = [scan for scan in live if next(scan, "done") != "done"]
    sfin_ref[...] = state_ref[...]


def _gla(q, kT, v, a, aT, s0, w, *, tm, tiles_per_batch, casts=()):
    n = q.shape[0]
    nt = n // tm
    batch = nt // tiles_per_batch
    lc, mcat = _gla_consts()
    tpb = tiles_per_batch
    fwd = lambda i: i
    bwd = lambda i: (i // tpb) * tpb + (tpb - 1 - i % tpb)

    def tile_specs(idx):
        return [
            pl.BlockSpec((tm, HK), lambda i: (idx(i), 0)),
            pl.BlockSpec((HK, tm), lambda i: (0, idx(i))),
            pl.BlockSpec((tm, BW), lambda i: (idx(i), 0)),
            pl.BlockSpec((tm, LANES), lambda i: (idx(i), 0)),
            pl.BlockSpec((2 * RANK, tm), lambda i: (0, idx(i))),
        ]

    state_block = (None, 2, 2, 2 * DK, 2 * DV)
    in_specs = tile_specs(fwd) + tile_specs(bwd) + [
        pl.BlockSpec(state_block, lambda i: (i // tpb, 0, 0, 0, 0)),
        _const_spec((2, LANES, HK)), _const_spec((2, HK, 2 * RANK)),
        _const_spec((2, 1, HK)), _const_spec((2, HK, 1)),
        _const_spec((2, GLA_C, GLA_C)), _const_spec((2, GLA_C, 3 * GLA_C)),
    ]
    out_specs = [
        pl.BlockSpec((tm, BW), lambda i: (fwd(i), 0)),
        pl.BlockSpec((tm, BW), lambda i: (bwd(i), 0)),
        pl.BlockSpec(state_block, lambda i: (i // tpb, 0, 0, 0, 0)),
    ]
    out_shape = [
        jax.ShapeDtypeStruct((n, BW), BF), jax.ShapeDtypeStruct((n, BW), BF),
        jax.ShapeDtypeStruct((batch, 2, 2, 2 * DK, 2 * DV), F32),
    ]
    tiles = (q, kT, v, a, aT)
    args = [*tiles, *tiles, s0, w["a2"], w["a2T"], w["ba2"], w["ba2T"], lc, mcat]
    body = functools.partial(_gla_kernel, tm=tm, tiles_per_batch=tpb)
    if casts:
        body, c_in, c_args, c_out, c_shape = _with_casts(body, len(args), 3, casts, nt)
        in_specs, args = in_specs + c_in, args + c_args
        out_specs, out_shape = out_specs + c_out, out_shape + c_shape
    res = pl.pallas_call(
        body,
        grid=(nt,),
        in_specs=in_specs,
        out_specs=out_specs,
        out_shape=out_shape,
        scratch_shapes=[pltpu.VMEM((2, 2, 2 * DK, 2 * DV), F32)],
        compiler_params=_params(1),
        name="gla",
    )(*args)
    return (*res[:3], list(res[3:])) if casts else res


def _dft_cos_sin(n):
    k = np.arange(n)
    ang = 2.0 * np.pi * ((k[:, None] * k[None, :]) % n) / n
    return np.cos(ang), np.sin(ang)


def _fnet_kernel(x_ref, g_ref, sh_ref, sc_ref, wfn_ref, ccs_ref, f1_ref, tc_ref, ts_ref, f2_ref,
                 o_ref, zr_scr, zi_scr):
    j = pl.program_id(1)
    n_stage1 = FFT_N2 // FFT_SUB

    @pl.when(j < n_stage1)
    def _():
        rows = FFT_N1 * FFT_SUB
        x = x_ref[...].reshape(rows, D)
        h = _norm_mod(x, g_ref[...], sh_ref[...], sc_ref[...]).astype(BF)
        fn = _dot(h, wfn_ref[...]).astype(BF)
        re, im = [], []
        for g in range(NGROUP):
            pq = _dot(fn[:, g * GC:(g + 1) * GC], ccs_ref[...])
            re.append(pq[:, :GC].astype(BF))
            im.append(pq[:, GC:].astype(BF))
        w = jnp.concatenate([jnp.concatenate(re, axis=1), jnp.concatenate(im, axis=1)], axis=0)
        y = _dot(f1_ref[...], w)
        yr, yi = y[:rows], y[rows:]
        tc = jnp.concatenate([tc_ref[...]] * (BW // LANES), axis=1)
        ts = jnp.concatenate([ts_ref[...]] * (BW // LANES), axis=1)
        group = pl.ds(pl.multiple_of(j * FFT_SUB, FFT_SUB), FFT_SUB)
        zr_scr[:, group, :] = (yr * tc + yi * ts).reshape(FFT_N1, FFT_SUB, BW)
        zi_scr[:, group, :] = (yi * tc - yr * ts).reshape(FFT_N1, FFT_SUB, BW)

    @pl.when(j >= n_stage1)
    def _():
        base = (j - n_stage1) * FFT_K1_BLOCK
        for k in range(FFT_K1_BLOCK):
            z = jnp.concatenate([zr_scr[base + k].astype(BF), zi_scr[base + k].astype(BF)], axis=0)
            o_ref[k] = _dot(f2_ref[...], z)


def _fft2_kernel(zr_ref, zi_ref, f_ref, o_ref):
    for k in range(zr_ref.shape[0]):
        z = jnp.concatenate([zr_ref[k].astype(BF), zi_ref[k].astype(BF)], axis=0)
        o_ref[k] = _dot(f_ref[...], z)


def _fnet_latent(x, g, shift, scale, w, batch):
    n1, n2, sub = FFT_N1, FFT_N2, FFT_SUB
    rows = n1 * sub
    c1, s1 = _dft_cos_sin(n1)
    eye = np.eye(sub)
    f1 = np.block([[np.kron(c1, eye), np.kron(s1, eye)],
                   [np.kron(-s1, eye), np.kron(c1, eye)]]) / math.sqrt(n1)
    k1 = np.arange(n1)[None, :, None]
    m2 = np.arange(n2).reshape(n2 // sub, 1, sub)
    ang = (2.0 * np.pi * (k1 * m2) / (n1 * n2)).reshape(n2 // sub, rows, 1)
    tc = np.broadcast_to(np.cos(ang), (n2 // sub, rows, LANES)).astype(np.float32)
    ts = np.broadcast_to(np.sin(ang), (n2 // sub, rows, LANES)).astype(np.float32)
    c2, s2 = _dft_cos_sin(n2)
    f2 = np.concatenate([c2, s2], axis=1) / math.sqrt(n2 * GC)
    s1_steps, s2_steps = n2 // sub, n1 // FFT_K1_BLOCK
    group = lambda j: jnp.minimum(j, s1_steps - 1)
    mod = pl.BlockSpec((None, 1, D), lambda b, j: (b, 0, 0))
    tw = pl.BlockSpec((None, rows, LANES), lambda b, j: (group(j), 0, 0))
    return pl.pallas_call(
        _fnet_kernel,
        grid=(batch, s1_steps + s2_steps),
        in_specs=[pl.BlockSpec((None, n1, sub, D), lambda b, j: (b, 0, group(j), 0)),
                  _const_spec((1, D)), mod, mod, _const_spec((D, BW)), _const_spec((GC, 2 * GC)),
                  _const_spec((2 * rows, 2 * rows)), tw, tw, _const_spec((n2, 2 * n2))],
        out_specs=pl.BlockSpec((None, FFT_K1_BLOCK, n2, BW),
                               lambda b, j: (b, jnp.maximum(j - s1_steps, 0), 0, 0)),
        out_shape=jax.ShapeDtypeStruct((batch, n1, n2, BW), F32),
        scratch_shapes=[pltpu.VMEM((n1, n2, BW), F32), pltpu.VMEM((n1, n2, BW), F32)],
        compiler_params=_params(2, FNET_VMEM_LIMIT),
        name="fnet",
    )(x.reshape(batch, n1, n2, D), g, shift, scale, w["fn"], w["ccs"], _bf16_const(f1),
      jnp.asarray(tc), jnp.asarray(ts), _bf16_const(f2))


def _fft_direct(wr, wi, batch, seq):
    c2, s2 = _dft_cos_sin(seq)
    f2 = np.concatenate([c2, s2], axis=1) / math.sqrt(seq * GC)
    zblk = pl.BlockSpec((1, seq, BW), lambda b: (b, 0, 0))
    y = pl.pallas_call(
        _fft2_kernel,
        grid=(batch,),
        in_specs=[zblk, zblk, _const_spec((seq, 2 * seq))],
        out_specs=zblk,
        out_shape=jax.ShapeDtypeStruct((batch, seq, BW), F32),
        compiler_params=_params(1),
        name="fft_direct",
    )(wr.reshape(batch, seq, BW), wi.reshape(batch, seq, BW), _bf16_const(f2))
    return y.reshape(batch * seq, BW)


def _mix_kernel(*refs, tm, row_len, k1_major):
    (x_ref, h_ref, of_ref, ob_ref, yf_ref, gt_ref, wr_ref, go_ref, wsu_ref, wsv_ref,
     wcb_ref, wcc_ref, wcx_ref, wsgu_ref, bsgu_ref, wconv_ref, wbr_ref, wgate_ref,
     bgate_ref, wout_ref, o_ref, br_ref) = refs
    h = h_ref[...]
    if k1_major:
        yf = jnp.concatenate([yf_ref[:, k, :] for k in range(tm // FFT_N1)], axis=0).astype(BF)
    else:
        yf = yf_ref[...].astype(BF)

    r = _dot(h, wr_ref[...])
    su = _dot(h, wsu_ref[...])
    sv = _dot(h, wsv_ref[...])
    zc = _dot(h, wcc_ref[...]) * _dot(h, wcx_ref[...])
    cb = _dot(h, wcb_ref[...])

    gla = []
    for hd in range(HEADS):
        lanes = slice(hd * DV, (hd + 1) * DV)
        oh = of_ref[:, lanes].astype(F32) + ob_ref[:, lanes].astype(F32)
        on = oh * lax.rsqrt(jnp.mean(oh * oh, axis=-1, keepdims=True) + EPS) * go_ref[:, lanes]
        gla.append((on * _silu(r[:, lanes])).astype(BF))
    for g in range(NGROUP):
        lanes = slice(g * GC, (g + 1) * GC)
        vg = sv[:, lanes]
        vc = vg - jnp.mean(vg, axis=-1, keepdims=True)
        z = (vc * lax.rsqrt(jnp.mean(vc * vc, axis=-1, keepdims=True) + EPS)).astype(BF)
        for n in range(tm // SGU_CHUNK):
            rows = slice(n * SGU_CHUNK, (n + 1) * SGU_CHUNK)
            s = _dot(wsgu_ref[g], z[rows]) + bsgu_ref[:, lanes]
            br_ref[rows, lanes] = (su[rows, lanes] * s).astype(BF)
    pos = lax.broadcasted_iota(jnp.int32, (tm, BW), 0) % row_len
    left = jnp.where(pos == 0, 0.0, pltpu.roll(zc, 1, 0))
    right = jnp.where(pos == row_len - 1, 0.0, pltpu.roll(zc, tm - 1, 0))
    y = left * wconv_ref[0:1, :] + zc * wconv_ref[1:2, :] + right * wconv_ref[2:3, :]
    branches = (jnp.concatenate(gla, axis=1), br_ref[...], yf, (cb * y).astype(BF))

    m = None
    for k, branch in enumerate(branches):
        gate = _sigmoid(_dot(h, wgate_ref[k]) + bgate_ref[k])
        term = gate * _dot(branch, wbr_ref[k])
        m = term if m is None else m + term
    o_ref[...] = x_ref[...] + gt_ref[...] * _dot(m.astype(BF), wout_ref[...])


def _mix(x, h, of, ob, yf, gate, w, *, tm, tiles_per_batch, per_batch, row_len):
    n = x.shape[0]
    layer = (w["layer"],)
    ms = _mod_spec(tiles_per_batch, per_batch)
    tile = lambda width: pl.BlockSpec((tm, width), lambda i: (i, 0))
    k1_major = yf.ndim == 4
    if k1_major:
        k2_tile = tm // FFT_N1
        assert tm % FFT_N1 == 0 and k2_tile % 8 == 0
        tpb = tiles_per_batch
        yf_spec = pl.BlockSpec((None, FFT_N1, k2_tile, BW), lambda i: (i // tpb, 0, i % tpb, 0))
    else:
        yf_spec = tile(BW)
    in_specs = [
        tile(D), tile(D), tile(BW), tile(BW), yf_spec, ms,
        _const_spec((D, BW)), _const_spec((1, BW)),
        _const_spec((D, BW)), _const_spec((D, BW)), _const_spec((D, BW)), _const_spec((D, BW)),
        _const_spec((D, BW)),
        _pick_spec(layer, (NGROUP, SGU_CHUNK, SGU_CHUNK)), _const_spec((SGU_CHUNK, BW)),
        _const_spec((3, BW)), _const_spec((4, BW, D)), _const_spec((4, D, D)),
        _const_spec((4, 1, D)), _const_spec((D, D)),
    ]
    return pl.pallas_call(
        functools.partial(_mix_kernel, tm=tm, row_len=row_len, k1_major=k1_major),
        grid=(n // tm,),
        in_specs=in_specs,
        out_specs=tile(D),
        out_shape=jax.ShapeDtypeStruct((n, D), F32),
        scratch_shapes=[pltpu.VMEM((tm, BW), BF)],
        compiler_params=_params(1),
        name="mix",
    )(x, h, of, ob, yf, gate, w["r"], w["go"], w["su"], w["sv"], w["cb"], w["cc"], w["cx"],
      w["sgu"], w["bsgu"], w["conv"], w["branch"], w["gate"], w["bgate"], w["out"])


_IN_EDGES = np.cumsum([0, HK, HK, BW, BW, 2 * RANK, BW, BW, BW, BW, BW, BW])
_IN_NAMES = ("q", "k", "v", "r", "a", "su", "sv", "fn", "cb", "cc", "cx")


def _in_columns(w_in_layer):
    return {name: w_in_layer[:, _IN_EDGES[j]:_IN_EDGES[j + 1]].astype(BF)
            for j, name in enumerate(_IN_NAMES)}


def _layer_weights(i, w_in_layer, w_gla_a2, b_gla_a2, g_gla_norm, w_sgu, b_sgu, w_conv, w_branch,
                   w_gate, b_gate, w_out):
    cols = _in_columns(w_in_layer)
    q, k, v, r, a, su, sv, fn, cb, cc, cx = [cols[name] for name in _IN_NAMES]
    a2 = w_gla_a2[i].astype(BF)
    a2_tok = jnp.zeros((2, LANES, HK), BF)
    a2_tok = a2_tok.at[0, :RANK].set(a2[0]).at[1, RANK:2 * RANK].set(a2[1])
    cc_, sc_ = _dft_cos_sin(GC)
    return {
        "q": q, "kT": k.T, "v": v, "r": r,
        "a": jnp.pad(a, ((0, 0), (0, LANES - 2 * RANK))), "aT": a.T,
        "a2": a2_tok, "a2T": jnp.transpose(a2_tok[:, :2 * RANK, :], (0, 2, 1)),
        "ba2": b_gla_a2[i][:, None, :], "ba2T": b_gla_a2[i][:, :, None],
        "go": g_gla_norm[i].reshape(1, BW),
        "su": su, "sv": sv, "fn": fn, "cb": cb, "cc": cc, "cx": cx,
        "ccs": _bf16_const(np.concatenate([cc_, -sc_], axis=1)),
        "layer": i,
        "sgu": w_sgu.astype(BF),
        "merge_casts": (_cast_job(w_gate.reshape(DEPTH, 4 * D, D), (i,), 4 * D, D),
                        _cast_job(w_branch.reshape(DEPTH, 4 * BW, D), (i,), 4 * BW, D),
                        _cast_job(w_out, (i,), D, D)),
        "bsgu": jnp.repeat(jnp.transpose(b_sgu[i]), GC, axis=1),
        "conv": w_conv[i],
        "bgate": b_gate[i][:, None, :],
    }


def kernel(x, c, ctx, c_ctx, w_ada, b_ada, g_norm, w_ff1, w_ff3, w_ff2, w_in, w_gla_a2, b_gla_a2,
           g_gla_norm, w_sgu, b_sgu, w_conv, w_branch, w_gate, b_gate, w_out, g_final):
    batch, seq, _ = x.shape
    ctx_len = ctx.shape[1]
    assert seq % TM_LAT == 0 and TM_LAT % GRID_W == 0 and seq == FFT_N1 * FFT_N2
    assert seq % TM_FFN == 0
    assert ctx_len % GLA_C == 0
    tpb = seq // TM_LAT

    cvec = jnp.concatenate([c, c_ctx[None, :]], axis=0)
    cb = jnp.broadcast_to(cvec[:, :, None], (3, D, LANES))
    mods = _adaln(cb, w_ada, b_ada).reshape(DEPTH, 3, N_MOD, 1, D)

    xs = x.reshape(batch * seq, D)
    cs = ctx.reshape(batch * ctx_len, D)
    lat = dict(tm=TM_LAT, tiles_per_batch=tpb, per_batch=True)
    lat_ffn = dict(tm=TM_FFN, tiles_per_batch=seq // TM_FFN, per_batch=True)
    cx_ = dict(tm=batch * ctx_len, tiles_per_batch=1, per_batch=False)
    s_zero = jnp.zeros((batch, 2, 2, 2 * DK, 2 * DV), F32)

    def ff_casts(layer, half):
        lead = (layer, half)
        return (_cast_job(w_ff1, lead, D, D_FF), _cast_job(w_ff3, lead, D, D_FF),
                _cast_job(w_ff2, lead, D_FF, D))

    ff = [w_ff1[0, 0].astype(BF), w_ff3[0, 0].astype(BF), w_ff2[0, 0].astype(BF)]

    for i in range(DEPTH):
        last = i == DEPTH - 1
        ml = [mods[i, 0:2, j] for j in range(N_MOD)]
        mc = [mods[i, 2:3, j] for j in range(N_MOD)]
        gn = g_norm[i][:, None, :]
        w = _layer_weights(i, w_in[i], w_gla_a2, b_gla_a2, g_gla_norm, w_sgu, b_sgu, w_conv,
                           w_branch, w_gate, b_gate, w_out)

        xs, hl, *gla_l, merge_w = _ffn(xs, gn[0], ml[0], ml[1], ml[2], *ff,
                                       prep=(gn[1], ml[3], ml[4], w, False),
                                       casts=w["merge_casts"], **lat_ffn)
        w["gate"] = merge_w[0].reshape(4, D, D)
        w["branch"] = merge_w[1].reshape(4, BW, D)
        w["out"] = merge_w[2]
        cs, hc, *gla_c, wrc, wic = _ffn(cs, gn[0], mc[0], mc[1], mc[2], *ff,
                                        prep=(gn[1], mc[3], mc[4], w, True), **cx_)

        ofc, obc, s_ctx = _gla(*gla_c, s_zero, w, tm=ctx_len, tiles_per_batch=1)
        ofl, obl, _, ff = _gla(*gla_l, s_ctx, w, tm=TM_LAT, tiles_per_batch=tpb,
                               casts=ff_casts(i, 1))
        yf = _fnet_latent(xs.reshape(batch, seq, D), gn[1], ml[3], ml[4], w, batch)
        xs = _mix(xs, hl, ofl, obl, yf, ml[5], w, row_len=GRID_W, **lat)
        if last:
            xs = _ffn(xs, gn[2], ml[6], ml[7], ml[8], *ff, g_final=g_final[None, :], **lat_ffn)
        else:
            yfc = _fft_direct(wrc, wic, batch, ctx_len)
            cs = _mix(cs, hc, ofc, obc, yfc, mc[5], w, row_len=ctx_len, **cx_)
            cs = _ffn(cs, gn[2], mc[6], mc[7], mc[8], *ff, **cx_)
            xs, ff = _ffn(xs, gn[2], ml[6], ml[7], ml[8], *ff, casts=ff_casts(i + 1, 0), **lat_ffn)
    return xs.reshape(batch, seq, D)
```

```python
import functools
import math

import numpy as np
import jax
import jax.numpy as jnp
from jax import lax
from jax.experimental import pallas as pl
from jax.experimental.pallas import tpu as pltpu

D = 1024
DEPTH = 2
GRID_W = 64
N_MOD = 9
D_FF = 2816
BW = 512
HEADS = 4
DV = 128
DK = 64
HK = HEADS * DK
RANK = 16
GLA_NORMALIZER = 16.0
SGU_CHUNK = 128
GC = 128
NGROUP = 4
EPS = 1e-6
LANES = 128

TM_LAT = 512
TM_FFN = 1024
FF_CHUNK = 256
GLA_C = 128
GLA_SUB = 32
GLA_NSUB = GLA_C // GLA_SUB
FFT_N1 = 64
FFT_N2 = 128
FFT_SUB = 8
FFT_K1_BLOCK = 8
ADA_TN = 2304
VMEM_LIMIT = 56 * 1024 * 1024
FNET_VMEM_LIMIT = 60 * 1024 * 1024

BF = jnp.bfloat16
F32 = jnp.float32

_NT = (((1,), (1,)), ((), ()))


def _dot(a, b):
    return jnp.dot(a, b, preferred_element_type=F32)


def _dot_nt(a, b):
    return lax.dot_general(a, b, _NT, preferred_element_type=F32)


def _sigmoid(x):
    return 0.5 * jnp.tanh(0.5 * x) + 0.5


def _silu(x):
    return x * _sigmoid(x)


def _log_sigmoid(x):
    return jnp.minimum(x, 0.0) - jnp.log(1.0 + jnp.exp(-jnp.abs(x)))


def _bf16_const(a):
    return jnp.asarray(a, F32).astype(BF)


def _const_spec(shape):
    nd = len(shape)
    return pl.BlockSpec(shape, lambda *_: (0,) * nd, pipeline_mode=pl.Buffered(1))


def _pick_spec(lead, shape):
    nd = len(shape)
    return pl.BlockSpec((None,) * len(lead) + tuple(shape), lambda *_: tuple(lead) + (0,) * nd,
                        pipeline_mode=pl.Buffered(1))


def _cast_job(src, lead, rows, cols):
    return (src, tuple(lead), rows, cols)


def _with_casts(kernel_fn, n_in, n_out, jobs, n_steps):
    bf16_rows = 16
    chunks = [max(k for k in range(1, n_steps + 1) if rows % (k * bf16_rows) == 0)
              for _, _, rows, _ in jobs]
    in_specs, args, out_specs, out_shape = [], [], [], []
    for (src, lead, rows, cols), ch in zip(jobs, chunks):
        r = rows // ch
        in_specs.append(pl.BlockSpec(
            (None,) * len(lead) + (r, cols),
            lambda i, lead=lead, ch=ch: lead + (jnp.minimum(i, ch - 1), 0)))
        out_specs.append(pl.BlockSpec((r, cols), lambda i, ch=ch: (jnp.minimum(i, ch - 1), 0)))
        args.append(src)
        out_shape.append(jax.ShapeDtypeStruct((rows, cols), BF))
    k = len(jobs)

    def kernel(*refs):
        ins, cast_in = refs[:n_in], refs[n_in:n_in + k]
        outs = refs[n_in + k:n_in + k + n_out]
        cast_out = refs[n_in + k + n_out:n_in + 2 * k + n_out]
        kernel_fn(*ins, *outs, *refs[n_in + 2 * k + n_out:])
        step = pl.program_id(0)
        for src_ref, dst_ref, ch in zip(cast_in, cast_out, chunks):
            if ch == n_steps:
                dst_ref[...] = src_ref[...].astype(BF)
            else:
                @pl.when(step < ch)
                def _(src_ref=src_ref, dst_ref=dst_ref):
                    dst_ref[...] = src_ref[...].astype(BF)

    return kernel, in_specs, args, out_specs, out_shape


def _params(n_grid, vmem_limit=None):
    return pltpu.CompilerParams(dimension_semantics=("arbitrary",) * n_grid,
                                vmem_limit_bytes=vmem_limit or VMEM_LIMIT)


def _norm_mod(x, g, shift, scale):
    hn = x * lax.rsqrt(jnp.mean(x * x, axis=-1, keepdims=True) + EPS) * g
    return hn * (1.0 + scale) + shift


def _adaln_kernel(cb_ref, w_ref, b_ref, o_ref):
    for r in range(3):
        cv = cb_ref[r]
        s = _silu(cv)
        for j in range(ADA_TN // LANES):
            w = w_ref[:, j * LANES:(j + 1) * LANES]
            o_ref[r:r + 1, j * LANES:(j + 1) * LANES] = (
                jnp.sum(w * s, axis=0, keepdims=True) + b_ref[:, j * LANES:(j + 1) * LANES])


def _adaln(cb, w_ada, b_ada):
    nmod = N_MOD * D
    return pl.pallas_call(
        _adaln_kernel,
        grid=(DEPTH, nmod // ADA_TN),
        in_specs=[
            pl.BlockSpec((3, D, LANES), lambda l, j: (0, 0, 0)),
            pl.BlockSpec((None, D, ADA_TN), lambda l, j: (l, 0, j)),
            pl.BlockSpec((None, 1, ADA_TN), lambda l, j: (l, 0, j)),
        ],
        out_specs=pl.BlockSpec((None, 3, ADA_TN), lambda l, j: (l, 0, j)),
        out_shape=jax.ShapeDtypeStruct((DEPTH, 3, nmod), F32),
        compiler_params=_params(2),
        name="adaln",
    )(cb, w_ada, b_ada.reshape(DEPTH, 1, nmod))


def _ffn_kernel(*refs, tail):
    s_ref, g_ref, sh_ref, sc_ref, gt_ref, w1_ref, w3_ref, w2_ref = refs[:8]
    rest = refs[8:]
    s = s_ref[...]
    h = _norm_mod(s, g_ref[...], sh_ref[...], sc_ref[...]).astype(BF)
    acc = jnp.zeros(s.shape, F32)
    for j in range(D_FF // FF_CHUNK):
        cols = slice(j * FF_CHUNK, (j + 1) * FF_CHUNK)
        a = _dot(h, w1_ref[:, cols])
        b = _dot(h, w3_ref[:, cols])
        u = (_silu(a) * b).astype(BF)
        acc = acc + _dot(u, w2_ref[cols, :])
    out = s + 0.5 * gt_ref[...] * acc
    if tail is None:
        rest[0][...] = out
    elif tail == "final":
        gf_ref, o_ref = rest
        o_ref[...] = out * lax.rsqrt(jnp.mean(out * out, axis=-1, keepdims=True) + EPS) * gf_ref[...]
    else:
        (g2_ref, sh2_ref, sc2_ref, wq_ref, wkT_ref, wv_ref, wa_ref, waT_ref) = rest[:8]
        rest = rest[8:]
        if tail == "prep_fnet":
            wfn_ref, ccs_ref = rest[:2]
            rest = rest[2:]
        o_ref, h_ref, q_ref, kT_ref, v_ref, a_ref, aT_ref = rest[:7]
        o_ref[...] = out
        h2 = _norm_mod(out, g2_ref[...], sh2_ref[...], sc2_ref[...]).astype(BF)
        h_ref[...] = h2
        q_ref[...] = (_dot(h2, wq_ref[...]) * (DK ** -0.5)).astype(BF)
        kT_ref[...] = _dot_nt(wkT_ref[...], h2).astype(BF)
        v_ref[...] = _dot(h2, wv_ref[...]).astype(BF)
        a_ref[...] = _dot(h2, wa_ref[...]).astype(BF)
        aT_ref[...] = _dot_nt(waT_ref[...], h2).astype(BF)
        if tail == "prep_fnet":
            wr_ref, wi_ref = rest[7:]
            fn = _dot(h2, wfn_ref[...]).astype(BF)
            for g in range(NGROUP):
                pq = _dot(fn[:, g * GC:(g + 1) * GC], ccs_ref[...])
                wr_ref[:, g * GC:(g + 1) * GC] = pq[:, :GC]
                wi_ref[:, g * GC:(g + 1) * GC] = pq[:, GC:]


def _mod_spec(tiles_per_batch, per_batch):
    if per_batch:
        return pl.BlockSpec((None, 1, D), lambda i: (i // tiles_per_batch, 0, 0))
    return pl.BlockSpec((None, 1, D), lambda i: (0, 0, 0))


def _ffn(s, g, shift, scale, gate, w1, w3, w2, *, tm, tiles_per_batch, per_batch,
         g_final=None, prep=None, casts=()):
    n = s.shape[0]
    ms = _mod_spec(tiles_per_batch, per_batch)
    rows = lambda width: pl.BlockSpec((tm, width), lambda i: (i, 0))
    cols = lambda height: pl.BlockSpec((height, tm), lambda i: (0, i))
    in_specs = [
        rows(D), _const_spec((1, D)), ms, ms, ms,
        _const_spec((D, D_FF)), _const_spec((D, D_FF)), _const_spec((D_FF, D)),
    ]
    args = [s, g, shift, scale, gate, w1, w3, w2]
    out_specs = [rows(D)]
    out_shape = [jax.ShapeDtypeStruct((n, D), F32)]
    tail = None
    if g_final is not None:
        tail = "final"
        in_specs.append(_const_spec((1, D)))
        args.append(g_final)
    elif prep is not None:
        g2, shift2, scale2, w, with_fnet = prep
        tail = "prep_fnet" if with_fnet else "prep"
        in_specs += [_const_spec((1, D)), ms, ms,
                     _const_spec((D, HK)), _const_spec((HK, D)), _const_spec((D, BW)),
                     _const_spec((D, LANES)), _const_spec((2 * RANK, D))]
        args += [g2, shift2, scale2, w["q"], w["kT"], w["v"], w["a"], w["aT"]]
        out_specs += [rows(D), rows(HK), cols(HK), rows(BW), rows(LANES), cols(2 * RANK)]
        out_shape += [
            jax.ShapeDtypeStruct((n, D), BF), jax.ShapeDtypeStruct((n, HK), BF),
            jax.ShapeDtypeStruct((HK, n), BF), jax.ShapeDtypeStruct((n, BW), BF),
            jax.ShapeDtypeStruct((n, LANES), BF), jax.ShapeDtypeStruct((2 * RANK, n), BF),
        ]
        if with_fnet:
            in_specs += [_const_spec((D, BW)), _const_spec((GC, 2 * GC))]
            args += [w["fn"], w["ccs"]]
            out_specs += [rows(BW), rows(BW)]
            out_shape += [jax.ShapeDtypeStruct((n, BW), F32)] * 2
    body = functools.partial(_ffn_kernel, tail=tail)
    n_main = len(out_shape)
    if casts:
        body, c_in, c_args, c_out, c_shape = _with_casts(body, len(args), n_main, casts, n // tm)
        in_specs, args = in_specs + c_in, args + c_args
        out_specs, out_shape = out_specs + c_out, out_shape + c_shape
    res = pl.pallas_call(
        body,
        grid=(n // tm,),
        in_specs=in_specs,
        out_specs=out_specs,
        out_shape=out_shape,
        compiler_params=_params(1),
        name="ffn" if tail is None else "ffn_" + tail,
    )(*args)
    main = list(res[:n_main]) if prep is not None else [res[0]]
    if casts:
        main.append(list(res[n_main:]))
    return main if len(main) > 1 else main[0]


def _bd_mask():
    r = lax.broadcasted_iota(jnp.int32, (2 * DK, 2 * DV), 0) // DK
    c = lax.broadcasted_iota(jnp.int32, (2 * DK, 2 * DV), 1) // DV
    return r == c


def _gla_consts():
    c = GLA_C
    t = np.arange(c)
    sub = t // GLA_SUB
    lc = np.stack([t[None, :] <= t[:, None], t[None, :] >= t[:, None]]).astype(np.float32)
    tt, jj = t[:, None], t[None, :]
    sub_end = (sub * GLA_SUB + GLA_SUB - 1)[None, :]
    sub_start = (sub * GLA_SUB)[None, :]
    mk_f = (tt > jj) & (tt <= sub_end)
    mk_b = (tt < jj) & (tt >= sub_start)
    me_f = tt > jj
    me_b = tt < jj
    ones = np.ones((c, c), bool)
    mcat = np.stack([np.concatenate([mk_f, me_f, ones], axis=1),
                     np.concatenate([mk_b, me_b, ones], axis=1)]).astype(np.float32)
    return jnp.asarray(lc, BF), jnp.asarray(mcat, BF)


def _gla_direction(dr, q_ref, kT_ref, v_ref, a_ref, aT_ref, wa2_ref, wa2T_ref, ba2_ref, ba2T_ref,
                   lc_ref, mcat_ref, state_ref, o_ref, nchunk):
    c = GLA_C
    inv = 1.0 / GLA_NORMALIZER
    g = (_log_sigmoid(_dot(a_ref[...], wa2_ref[dr]) + ba2_ref[dr]) * inv).astype(BF)
    gT = (_log_sigmoid(_dot(wa2T_ref[dr], aT_ref[...]) + ba2T_ref[dr]) * inv).astype(BF)
    col_sub = lax.broadcasted_iota(jnp.int32, (2 * DK, c), 1) // GLA_SUB
    ii = lax.broadcasted_iota(jnp.int32, (c, c), 0)
    jj = lax.broadcasted_iota(jnp.int32, (c, c), 1)
    tri = (jj <= ii) if dr == 0 else (jj >= ii)
    low_half = lax.broadcasted_iota(jnp.int32, (c, LANES), 1) < DK
    bd = _bd_mask()
    zero_v = jnp.zeros((c, DV), BF)
    state = [state_ref[dr, 0], state_ref[dr, 1]]
    order = range(nchunk) if dr == 0 else range(nchunk - 1, -1, -1)
    sums = []
    for ci in order:
        rows = slice(ci * c, (ci + 1) * c)
        sums.append((_dot(lc_ref[dr], g[rows]),
                     _dot(gT[:, rows], mcat_ref[dr])))
    yield
    operands = []
    for ci, (b, fm) in zip(order, sums):
        rows = slice(ci * c, (ci + 1) * c)
        kc = kT_ref[:, rows].astype(F32)
        ksub = (kc * jnp.exp(fm[:, 0:c])).astype(BF)
        kend = (kc * jnp.exp(fm[:, c:2 * c])).astype(BF)
        dec = jnp.exp(fm[:, 2 * c:3 * c])
        qc = q_ref[rows, :].astype(F32)
        qb = (qc * jnp.exp(b)).astype(BF)
        for p in range(2):
            lanes = slice(p * LANES, (p + 1) * LANES)
            qp = qc[:, lanes]
            bp = b[:, lanes]
            kp = ksub[lanes]
            qblocks = []
            kblocks = []
            for j in range(GLA_NSUB):
                if dr == 0:
                    ref_row, lo, hi = j * GLA_SUB + GLA_SUB - 1, j * GLA_SUB, c
                else:
                    ref_row, lo, hi = j * GLA_SUB, 0, (j + 1) * GLA_SUB
                live = qp[lo:hi] * jnp.exp(bp[lo:hi] - bp[ref_row:ref_row + 1, :])
                parts = []
                if lo > 0:
                    parts.append(jnp.zeros((lo, LANES), F32))
                parts.append(live)
                if hi < c:
                    parts.append(jnp.zeros((c - hi, LANES), F32))
                qblocks.append(jnp.concatenate(parts, axis=0) if len(parts) > 1 else live)
                kblocks.append(jnp.where(col_sub == j, kp, jnp.zeros_like(kp)))
            hats = []
            for hd in range(2):
                qh = []
                for m in range(GLA_NSUB // 2):
                    even, odd = qblocks[2 * m], qblocks[2 * m + 1]
                    if hd == 0:
                        qh.append(jnp.where(low_half, even, pltpu.roll(odd, DK, 1)))
                    else:
                        qh.append(jnp.where(low_half, pltpu.roll(even, DK, 1), odd))
                hats.append((jnp.concatenate(qh, axis=1).astype(BF),
                             jnp.concatenate([kj[hd * DK:(hd + 1) * DK] for kj in kblocks], axis=0)))
            dp = dec[lanes]
            operands.append((ci, p, hats, qb[:, lanes], kend[lanes],
                             jnp.concatenate([dp, dp], axis=1)))
    yield
    scores = []
    for ci, p, hats, qbp, kendp, decay in operands:
        heads = [jnp.where(tri, _dot(qhat, khat), 0.0).astype(BF) for qhat, khat in hats]
        scores.append(jnp.concatenate(heads, axis=1))
    yield
    pending = []
    for (ci, p, hats, qbp, kendp, decay), att in zip(operands, scores):
        vp = v_ref[ci * c:(ci + 1) * c, p * 2 * DV:(p + 1) * 2 * DV]
        vbd = jnp.concatenate([
            jnp.concatenate([vp[:, :DV], zero_v], axis=1),
            jnp.concatenate([zero_v, vp[:, DV:]], axis=1)], axis=0)
        pending.append((ci, p, _dot(att, vbd), qbp, decay, jnp.where(bd, _dot(kendp, vp), 0.0)))
    yield
    for ci, p, intra, qbp, decay, upd in pending:
        o_p = intra + _dot(qbp, state[p].astype(BF))
        o_ref[ci * c:(ci + 1) * c, p * 2 * DV:(p + 1) * 2 * DV] = o_p.astype(BF)
        state[p] = decay * state[p] + upd
    state_ref[dr, 0] = state[0]
    state_ref[dr, 1] = state[1]


def _gla_kernel(qf_ref, kTf_ref, vf_ref, af_ref, aTf_ref, qb_ref, kTb_ref, vb_ref, ab_ref, aTb_ref,
                s0_ref, wa2_ref, wa2T_ref, ba2_ref, ba2T_ref, lc_ref, mcat_ref,
                of_ref, ob_ref, sfin_ref, state_ref, *, tm, tiles_per_batch):
    @pl.when(pl.program_id(0) % tiles_per_batch == 0)
    def _():
        state_ref[...] = s0_ref[...]

    shared = (wa2_ref, wa2T_ref, ba2_ref, ba2T_ref, lc_ref, mcat_ref, state_ref)
    scans = [
        _gla_direction(0, qf_ref, kTf_ref, vf_ref, af_ref, aTf_ref, *shared, of_ref, tm // GLA_C),
        _gla_direction(1, qb_ref, kTb_ref, vb_ref, ab_ref, aTb_ref, *shared, ob_ref, tm // GLA_C),
    ]
    live = list(scans)
    while live:
        live = [scan for scan in live if next(scan, "done") != "done"]
    sfin_ref[...] = state_ref[...]


def _gla(q, kT, v, a, aT, s0, w, *, tm, tiles_per_batch, casts=()):
    n = q.shape[0]
    nt = n // tm
    batch = nt // tiles_per_batch
    lc, mcat = _gla_consts()
    tpb = tiles_per_batch
    fwd = lambda i: i
    bwd = lambda i: (i // tpb) * tpb + (tpb - 1 - i % tpb)

    def tile_specs(idx):
        return [
            pl.BlockSpec((tm, HK), lambda i: (idx(i), 0)),
            pl.BlockSpec((HK, tm), lambda i: (0, idx(i))),
            pl.BlockSpec((tm, BW), lambda i: (idx(i), 0)),
            pl.BlockSpec((tm, LANES), lambda i: (idx(i), 0)),
            pl.BlockSpec((2 * RANK, tm), lambda i: (0, idx(i))),
        ]

    state_block = (None, 2, 2, 2 * DK, 2 * DV)
    in_specs = tile_specs(fwd) + tile_specs(bwd) + [
        pl.BlockSpec(state_block, lambda i: (i // tpb, 0, 0, 0, 0)),
        _const_spec((2, LANES, HK)), _const_spec((2, HK, 2 * RANK)),
        _const_spec((2, 1, HK)), _const_spec((2, HK, 1)),
        _const_spec((2, GLA_C, GLA_C)), _const_spec((2, GLA_C, 3 * GLA_C)),
    ]
    out_specs = [
        pl.BlockSpec((tm, BW), lambda i: (fwd(i), 0)),
        pl.BlockSpec((tm, BW), lambda i: (bwd(i), 0)),
        pl.BlockSpec(state_block, lambda i: (i // tpb, 0, 0, 0, 0)),
    ]
    out_shape = [
        jax.ShapeDtypeStruct((n, BW), BF), jax.ShapeDtypeStruct((n, BW), BF),
        jax.ShapeDtypeStruct((batch, 2, 2, 2 * DK, 2 * DV), F32),
    ]
    tiles = (q, kT, v, a, aT)
    args = [*tiles, *tiles, s0, w["a2"], w["a2T"], w["ba2"], w["ba2T"], lc, mcat]
    body = functools.partial(_gla_kernel, tm=tm, tiles_per_batch=tpb)
    if casts:
        body, c_in, c_args, c_out, c_shape = _with_casts(body, len(args), 3, casts, nt)
        in_specs, args = in_specs + c_in, args + c_args
        out_specs, out_shape = out_specs + c_out, out_shape + c_shape
    res = pl.pallas_call(
        body,
        grid=(nt,),
        in_specs=in_specs,
        out_specs=out_specs,
        out_shape=out_shape,
        scratch_shapes=[pltpu.VMEM((2, 2, 2 * DK, 2 * DV), F32)],
        compiler_params=_params(1),
        name="gla",
    )(*args)
    return (*res[:3], list(res[3:])) if casts else res


def _dft_cos_sin(n):
    k = np.arange(n)
    ang = 2.0 * np.pi * ((k[:, None] * k[None, :]) % n) / n
    return np.cos(ang), np.sin(ang)


def _fnet_kernel(x_ref, g_ref, sh_ref, sc_ref, wfn_ref, ccs_ref, f1_ref, tc_ref, ts_ref, f2_ref,
                 o_ref, zr_scr, zi_scr):
    j = pl.program_id(1)
    n_stage1 = FFT_N2 // FFT_SUB

    @pl.when(j < n_stage1)
    def _():
        rows = FFT_N1 * FFT_SUB
        x = x_ref[...].reshape(rows, D)
        h = _norm_mod(x, g_ref[...], sh_ref[...], sc_ref[...]).astype(BF)
        fn = _dot(h, wfn_ref[...]).astype(BF)
        re, im = [], []
        for g in range(NGROUP):
            pq = _dot(fn[:, g * GC:(g + 1) * GC], ccs_ref[...])
            re.append(pq[:, :GC].astype(BF))
            im.append(pq[:, GC:].astype(BF))
        w = jnp.concatenate([jnp.concatenate(re, axis=1), jnp.concatenate(im, axis=1)], axis=0)
        y = _dot(f1_ref[...], w)
        yr, yi = y[:rows], y[rows:]
        tc = jnp.concatenate([tc_ref[...]] * (BW // LANES), axis=1)
        ts = jnp.concatenate([ts_ref[...]] * (BW // LANES), axis=1)
        group = pl.ds(pl.multiple_of(j * FFT_SUB, FFT_SUB), FFT_SUB)
        zr_scr[:, group, :] = (yr * tc + yi * ts).reshape(FFT_N1, FFT_SUB, BW)
        zi_scr[:, group, :] = (yi * tc - yr * ts).reshape(FFT_N1, FFT_SUB, BW)

    @pl.when(j >= n_stage1)
    def _():
        base = (j - n_stage1) * FFT_K1_BLOCK
        for k in range(FFT_K1_BLOCK):
            z = jnp.concatenate([zr_scr[base + k].astype(BF), zi_scr[base + k].astype(BF)], axis=0)
            o_ref[k] = _dot(f2_ref[...], z)


def _fft2_kernel(zr_ref, zi_ref, f_ref, o_ref):
    for k in range(zr_ref.shape[0]):
        z = jnp.concatenate([zr_ref[k].astype(BF), zi_ref[k].astype(BF)], axis=0)
        o_ref[k] = _dot(f_ref[...], z)


def _fnet_latent(x, g, shift, scale, w, batch):
    n1, n2, sub = FFT_N1, FFT_N2, FFT_SUB
    rows = n1 * sub
    c1, s1 = _dft_cos_sin(n1)
    eye = np.eye(sub)
    f1 = np.block([[np.kron(c1, eye), np.kron(s1, eye)],
                   [np.kron(-s1, eye), np.kron(c1, eye)]]) / math.sqrt(n1)
    k1 = np.arange(n1)[None, :, None]
    m2 = np.arange(n2).reshape(n2 // sub, 1, sub)
    ang = (2.0 * np.pi * (k1 * m2) / (n1 * n2)).reshape(n2 // sub, rows, 1)
    tc = np.broadcast_to(np.cos(ang), (n2 // sub, rows, LANES)).astype(np.float32)
    ts = np.broadcast_to(np.sin(ang), (n2 // sub, rows, LANES)).astype(np.float32)
    c2, s2 = _dft_cos_sin(n2)
    f2 = np.concatenate([c2, s2], axis=1) / math.sqrt(n2 * GC)
    s1_steps, s2_steps = n2 // sub, n1 // FFT_K1_BLOCK
    group = lambda j: jnp.minimum(j, s1_steps - 1)
    mod = pl.BlockSpec((None, 1, D), lambda b, j: (b, 0, 0))
    tw = pl.BlockSpec((None, rows, LANES), lambda b, j: (group(j), 0, 0))
    return pl.pallas_call(
        _fnet_kernel,
        grid=(batch, s1_steps + s2_steps),
        in_specs=[pl.BlockSpec((None, n1, sub, D), lambda b, j: (b, 0, group(j), 0)),
                  _const_spec((1, D)), mod, mod, _const_spec((D, BW)), _const_spec((GC, 2 * GC)),
                  _const_spec((2 * rows, 2 * rows)), tw, tw, _const_spec((n2, 2 * n2))],
        out_specs=pl.BlockSpec((None, FFT_K1_BLOCK, n2, BW),
                               lambda b, j: (b, jnp.maximum(j - s1_steps, 0), 0, 0)),
        out_shape=jax.ShapeDtypeStruct((batch, n1, n2, BW), F32),
        scratch_shapes=[pltpu.VMEM((n1, n2, BW), F32), pltpu.VMEM((n1, n2, BW), F32)],
        compiler_params=_params(2, FNET_VMEM_LIMIT),
        name="fnet",
    )(x.reshape(batch, n1, n2, D), g, shift, scale, w["fn"], w["ccs"], _bf16_const(f1),
      jnp.asarray(tc), jnp.asarray(ts), _bf16_const(f2))


def _fft_direct(wr, wi, batch, seq):
    c2, s2 = _dft_cos_sin(seq)
    f2 = np.concatenate([c2, s2], axis=1) / math.sqrt(seq * GC)
    zblk = pl.BlockSpec((1, seq, BW), lambda b: (b, 0, 0))
    y = pl.pallas_call(
        _fft2_kernel,
        grid=(batch,),
        in_specs=[zblk, zblk, _const_spec((seq, 2 * seq))],
        out_specs=zblk,
        out_shape=jax.ShapeDtypeStruct((batch, seq, BW), F32),
        compiler_params=_params(1),
        name="fft_direct",
    )(wr.reshape(batch, seq, BW), wi.reshape(batch, seq, BW), _bf16_const(f2))
    return y.reshape(batch * seq, BW)


def _mix_kernel(*refs, tm, row_len, k1_major):
    (x_ref, h_ref, of_ref, ob_ref, yf_ref, gt_ref, wr_ref, go_ref, wsu_ref, wsv_ref,
     wcb_ref, wcc_ref, wcx_ref, wsgu_ref, bsgu_ref, wconv_ref, wbr_ref, wgate_ref,
     bgate_ref, wout_ref, o_ref, br_ref) = refs
    h = h_ref[...]
    if k1_major:
        yf = jnp.concatenate([yf_ref[:, k, :] for k in range(tm // FFT_N1)], axis=0).astype(BF)
    else:
        yf = yf_ref[...].astype(BF)

    r = _dot(h, wr_ref[...])
    su = _dot(h, wsu_ref[...])
    sv = _dot(h, wsv_ref[...])
    zc = _dot(h, wcc_ref[...]) * _dot(h, wcx_ref[...])
    cb = _dot(h, wcb_ref[...])

    gla = []
    for hd in range(HEADS):
        lanes = slice(hd * DV, (hd + 1) * DV)
        oh = of_ref[:, lanes].astype(F32) + ob_ref[:, lanes].astype(F32)
        on = oh * lax.rsqrt(jnp.mean(oh * oh, axis=-1, keepdims=True) + EPS) * go_ref[:, lanes]
        gla.append((on * _silu(r[:, lanes])).astype(BF))
    for g in range(NGROUP):
        lanes = slice(g * GC, (g + 1) * GC)
        vg = sv[:, lanes]
        vc = vg - jnp.mean(vg, axis=-1, keepdims=True)
        z = (vc * lax.rsqrt(jnp.mean(vc * vc, axis=-1, keepdims=True) + EPS)).astype(BF)
        for n in range(tm // SGU_CHUNK):
            rows = slice(n * SGU_CHUNK, (n + 1) * SGU_CHUNK)
            s = _dot(wsgu_ref[g], z[rows]) + bsgu_ref[:, lanes]
            br_ref[rows, lanes] = (su[rows, lanes] * s).astype(BF)
    pos = lax.broadcasted_iota(jnp.int32, (tm, BW), 0) % row_len
    left = jnp.where(pos == 0, 0.0, pltpu.roll(zc, 1, 0))
    right = jnp.where(pos == row_len - 1, 0.0, pltpu.roll(zc, tm - 1, 0))
    y = left * wconv_ref[0:1, :] + zc * wconv_ref[1:2, :] + right * wconv_ref[2:3, :]
    branches = (jnp.concatenate(gla, axis=1), br_ref[...], yf, (cb * y).astype(BF))

    m = None
    for k, branch in enumerate(branches):
        gate = _sigmoid(_dot(h, wgate_ref[k]) + bgate_ref[k])
        term = gate * _dot(branch, wbr_ref[k])
        m = term if m is None else m + term
    o_ref[...] = x_ref[...] + gt_ref[...] * _dot(m.astype(BF), wout_ref[...])


def _mix(x, h, of, ob, yf, gate, w, *, tm, tiles_per_batch, per_batch, row_len):
    n = x.shape[0]
    layer = (w["layer"],)
    ms = _mod_spec(tiles_per_batch, per_batch)
    tile = lambda width: pl.BlockSpec((tm, width), lambda i: (i, 0))
    k1_major = yf.ndim == 4
    if k1_major:
        k2_tile = tm // FFT_N1
        assert tm % FFT_N1 == 0 and k2_tile % 8 == 0
        tpb = tiles_per_batch
        yf_spec = pl.BlockSpec((None, FFT_N1, k2_tile, BW), lambda i: (i // tpb, 0, i % tpb, 0))
    else:
        yf_spec = tile(BW)
    in_specs = [
        tile(D), tile(D), tile(BW), tile(BW), yf_spec, ms,
        _const_spec((D, BW)), _const_spec((1, BW)),
        _const_spec((D, BW)), _const_spec((D, BW)), _const_spec((D, BW)), _const_spec((D, BW)),
        _const_spec((D, BW)),
        _pick_spec(layer, (NGROUP, SGU_CHUNK, SGU_CHUNK)), _const_spec((SGU_CHUNK, BW)),
        _const_spec((3, BW)), _const_spec((4, BW, D)), _const_spec((4, D, D)),
        _const_spec((4, 1, D)), _const_spec((D, D)),
    ]
    return pl.pallas_call(
        functools.partial(_mix_kernel, tm=tm, row_len=row_len, k1_major=k1_major),
        grid=(n // tm,),
        in_specs=in_specs,
        out_specs=tile(D),
        out_shape=jax.ShapeDtypeStruct((n, D), F32),
        scratch_shapes=[pltpu.VMEM((tm, BW), BF)],
        compiler_params=_params(1),
        name="mix",
    )(x, h, of, ob, yf, gate, w["r"], w["go"], w["su"], w["sv"], w["cb"], w["cc"], w["cx"],
      w["sgu"], w["bsgu"], w["conv"], w["branch"], w["gate"], w["bgate"], w["out"])


_IN_EDGES = np.cumsum([0, HK, HK, BW, BW, 2 * RANK, BW, BW, BW, BW, BW, BW])
_IN_NAMES = ("q", "k", "v", "r", "a", "su", "sv", "fn", "cb", "cc", "cx")


def _in_columns(w_in_layer):
    return {name: w_in_layer[:, _IN_EDGES[j]:_IN_EDGES[j + 1]].astype(BF)
            for j, name in enumerate(_IN_NAMES)}


def _layer_weights(i, w_in_layer, w_gla_a2, b_gla_a2, g_gla_norm, w_sgu, b_sgu, w_conv, w_branch,
                   w_gate, b_gate, w_out):
    cols = _in_columns(w_in_layer)
    q, k, v, r, a, su, sv, fn, cb, cc, cx = [cols[name] for name in _IN_NAMES]
    a2 = w_gla_a2[i].astype(BF)
    a2_tok = jnp.zeros((2, LANES, HK), BF)
    a2_tok = a2_tok.at[0, :RANK].set(a2[0]).at[1, RANK:2 * RANK].set(a2[1])
    cc_, sc_ = _dft_cos_sin(GC)
    return {
        "q": q, "kT": k.T, "v": v, "r": r,
        "a": jnp.pad(a, ((0, 0), (0, LANES - 2 * RANK))), "aT": a.T,
        "a2": a2_tok, "a2T": jnp.transpose(a2_tok[:, :2 * RANK, :], (0, 2, 1)),
        "ba2": b_gla_a2[i][:, None, :], "ba2T": b_gla_a2[i][:, :, None],
        "go": g_gla_norm[i].reshape(1, BW),
        "su": su, "sv": sv, "fn": fn, "cb": cb, "cc": cc, "cx": cx,
        "ccs": _bf16_const(np.concatenate([cc_, -sc_], axis=1)),
        "layer": i,
        "sgu": w_sgu.astype(BF),
        "merge_casts": (_cast_job(w_gate.reshape(DEPTH, 4 * D, D), (i,), 4 * D, D),
                        _cast_job(w_branch.reshape(DEPTH, 4 * BW, D), (i,), 4 * BW, D),
                        _cast_job(w_out, (i,), D, D)),
        "bsgu": jnp.repeat(jnp.transpose(b_sgu[i]), GC, axis=1),
        "conv": w_conv[i],
        "bgate": b_gate[i][:, None, :],
    }


def kernel(x, c, ctx, c_ctx, w_ada, b_ada, g_norm, w_ff1, w_ff3, w_ff2, w_in, w_gla_a2, b_gla_a2,
           g_gla_norm, w_sgu, b_sgu, w_conv, w_branch, w_gate, b_gate, w_out, g_final):
    batch, seq, _ = x.shape
    ctx_len = ctx.shape[1]
    assert seq % TM_LAT == 0 and TM_LAT % GRID_W == 0 and seq == FFT_N1 * FFT_N2
    assert seq % TM_FFN == 0
    assert ctx_len % GLA_C == 0
    tpb = seq // TM_LAT

    cvec = jnp.concatenate([c, c_ctx[None, :]], axis=0)
    cb = jnp.broadcast_to(cvec[:, :, None], (3, D, LANES))
    mods = _adaln(cb, w_ada, b_ada).reshape(DEPTH, 3, N_MOD, 1, D)

    xs = x.reshape(batch * seq, D)
    cs = ctx.reshape(batch * ctx_len, D)
    lat = dict(tm=TM_LAT, tiles_per_batch=tpb, per_batch=True)
    lat_ffn = dict(tm=TM_FFN, tiles_per_batch=seq // TM_FFN, per_batch=True)
    cx_ = dict(tm=batch * ctx_len, tiles_per_batch=1, per_batch=False)
    s_zero = jnp.zeros((batch, 2, 2, 2 * DK, 2 * DV), F32)

    def ff_casts(layer, half):
        lead = (layer, half)
        return (_cast_job(w_ff1, lead, D, D_FF), _cast_job(w_ff3, lead, D, D_FF),
                _cast_job(w_ff2, lead, D_FF, D))

    ff = [w_ff1[0, 0].astype(BF), w_ff3[0, 0].astype(BF), w_ff2[0, 0].astype(BF)]

    for i in range(DEPTH):
        last = i == DEPTH - 1
        ml = [mods[i, 0:2, j] for j in range(N_MOD)]
        mc = [mods[i, 2:3, j] for j in range(N_MOD)]
        gn = g_norm[i][:, None, :]
        w = _layer_weights(i, w_in[i], w_gla_a2, b_gla_a2, g_gla_norm, w_sgu, b_sgu, w_conv,
                           w_branch, w_gate, b_gate, w_out)

        xs, hl, *gla_l, merge_w = _ffn(xs, gn[0], ml[0], ml[1], ml[2], *ff,
                                       prep=(gn[1], ml[3], ml[4], w, False),
                                       casts=w["merge_casts"], **lat_ffn)
        w["gate"] = merge_w[0].reshape(4, D, D)
        w["branch"] = merge_w[1].reshape(4, BW, D)
        w["out"] = merge_w[2]
        cs, hc, *gla_c, wrc, wic = _ffn(cs, gn[0], mc[0], mc[1], mc[2], *ff,
                                        prep=(gn[1], mc[3], mc[4], w, True), **cx_)

        ofc, obc, s_ctx = _gla(*gla_c, s_zero, w, tm=ctx_len, tiles_per_batch=1)
        ofl, obl, _, ff = _gla(*gla_l, s_ctx, w, tm=TM_LAT, tiles_per_batch=tpb,
                               casts=ff_casts(i, 1))
        yf = _fnet_latent(xs.reshape(batch, seq, D), gn[1], ml[3], ml[4], w, batch)
        xs = _mix(xs, hl, ofl, obl, yf, ml[5], w, row_len=GRID_W, **lat)
        if last:
            xs = _ffn(xs, gn[2], ml[6], ml[7], ml[8], *ff, g_final=g_final[None, :], **lat_ffn)
        else:
            yfc = _fft_direct(wrc, wic, batch, ctx_len)
            cs = _mix(cs, hc, ofc, obc, yfc, mc[5], w, row_len=ctx_len, **cx_)
            cs = _ffn(cs, gn[2], mc[6], mc[7], mc[8], *ff, **cx_)
            xs, ff = _ffn(xs, gn[2], ml[6], ml[7], ml[8], *ff, casts=ff_casts(i + 1, 0), **lat_ffn)
    return xs.reshape(batch, seq, D)
```

```python
import functools
import math

import numpy as np
import jax
import jax.numpy as jnp
from jax import lax
from jax.experimental import pallas as pl
from jax.experimental.pallas import tpu as pltpu

D = 1024
DEPTH = 2
GRID_W = 64
N_MOD = 9
D_FF = 2816
BW = 512
HEADS = 4
DV = 128
DK = 64
HK = HEADS * DK
RANK = 16
GLA_NORMALIZER = 16.0
SGU_CHUNK = 128
GC = 128
NGROUP = 4
EPS = 1e-6
LANES = 128

TM_LAT = 512
TM_FFN = 1024
FF_CHUNK = 256
GLA_C = 128
GLA_SUB = 32
GLA_NSUB = GLA_C // GLA_SUB
FFT_N1 = 64
FFT_N2 = 128
FFT_SUB = 8
FFT_K1_BLOCK = 8
ADA_TN = 2304
VMEM_LIMIT = 56 * 1024 * 1024
FNET_VMEM_LIMIT = 60 * 1024 * 1024

BF = jnp.bfloat16
F32 = jnp.float32

_NT = (((1,), (1,)), ((), ()))


def _dot(a, b):
    return jnp.dot(a, b, preferred_element_type=F32)


def _dot_nt(a, b):
    return lax.dot_general(a, b, _NT, preferred_element_type=F32)


def _sigmoid(x):
    return 0.5 * jnp.tanh(0.5 * x) + 0.5


def _silu(x):
    return x * _sigmoid(x)


def _log_sigmoid(x):
    return jnp.minimum(x, 0.0) - jnp.log(1.0 + jnp.exp(-jnp.abs(x)))


def _bf16_const(a):
    return jnp.asarray(a, F32).astype(BF)


def _const_spec(shape):
    nd = len(shape)
    return pl.BlockSpec(shape, lambda *_: (0,) * nd, pipeline_mode=pl.Buffered(1))


def _pick_spec(lead, shape):
    nd = len(shape)
    return pl.BlockSpec((None,) * len(lead) + tuple(shape), lambda *_: tuple(lead) + (0,) * nd,
                        pipeline_mode=pl.Buffered(1))


def _cast_job(src, lead, rows, cols):
    return (src, tuple(lead), rows, cols)


def _with_casts(kernel_fn, n_in, n_out, jobs, n_steps):
    bf16_rows = 16
    chunks = [max(k for k in range(1, n_steps + 1) if rows % (k * bf16_rows) == 0)
              for _, _, rows, _ in jobs]
    in_specs, args, out_specs, out_shape = [], [], [], []
    for (src, lead, rows, cols), ch in zip(jobs, chunks):
        r = rows // ch
        in_specs.append(pl.BlockSpec(
            (None,) * len(lead) + (r, cols),
            lambda i, lead=lead, ch=ch: lead + (jnp.minimum(i, ch - 1), 0)))
        out_specs.append(pl.BlockSpec((r, cols), lambda i, ch=ch: (jnp.minimum(i, ch - 1), 0)))
        args.append(src)
        out_shape.append(jax.ShapeDtypeStruct((rows, cols), BF))
    k = len(jobs)

    def kernel(*refs):
        ins, cast_in = refs[:n_in], refs[n_in:n_in + k]
        outs = refs[n_in + k:n_in + k + n_out]
        cast_out = refs[n_in + k + n_out:n_in + 2 * k + n_out]
        kernel_fn(*ins, *outs, *refs[n_in + 2 * k + n_out:])
        step = pl.program_id(0)
        for src_ref, dst_ref, ch in zip(cast_in, cast_out, chunks):
            if ch == n_steps:
                dst_ref[...] = src_ref[...].astype(BF)
            else:
                @pl.when(step < ch)
                def _(src_ref=src_ref, dst_ref=dst_ref):
                    dst_ref[...] = src_ref[...].astype(BF)

    return kernel, in_specs, args, out_specs, out_shape


def _params(n_grid, vmem_limit=None):
    return pltpu.CompilerParams(dimension_semantics=("arbitrary",) * n_grid,
                                vmem_limit_bytes=vmem_limit or VMEM_LIMIT)


def _norm_mod(x, g, shift, scale):
    hn = x * lax.rsqrt(jnp.mean(x * x, axis=-1, keepdims=True) + EPS) * g
    return hn * (1.0 + scale) + shift


def _adaln_kernel(cb_ref, w_ref, b_ref, o_ref):
    for r in range(3):
        cv = cb_ref[r]
        s = _silu(cv)
        for j in range(ADA_TN // LANES):
            w = w_ref[:, j * LANES:(j + 1) * LANES]
            o_ref[r:r + 1, j * LANES:(j + 1) * LANES] = (
                jnp.sum(w * s, axis=0, keepdims=True) + b_ref[:, j * LANES:(j + 1) * LANES])


def _adaln(cb, w_ada, b_ada):
    nmod = N_MOD * D
    return pl.pallas_call(
        _adaln_kernel,
        grid=(DEPTH, nmod // ADA_TN),
        in_specs=[
            pl.BlockSpec((3, D, LANES), lambda l, j: (0, 0, 0)),
            pl.BlockSpec((None, D, ADA_TN), lambda l, j: (l, 0, j)),
            pl.BlockSpec((None, 1, ADA_TN), lambda l, j: (l, 0, j)),
        ],
        out_specs=pl.BlockSpec((None, 3, ADA_TN), lambda l, j: (l, 0, j)),
        out_shape=jax.ShapeDtypeStruct((DEPTH, 3, nmod), F32),
        compiler_params=_params(2),
        name="adaln",
    )(cb, w_ada, b_ada.reshape(DEPTH, 1, nmod))


def _ffn_kernel(*refs, tail):
    s_ref, g_ref, sh_ref, sc_ref, gt_ref, w1_ref, w3_ref, w2_ref = refs[:8]
    rest = refs[8:]
    s = s_ref[...]
    h = _norm_mod(s, g_ref[...], sh_ref[...], sc_ref[...]).astype(BF)
    acc = jnp.zeros(s.shape, F32)
    for j in range(D_FF // FF_CHUNK):
        cols = slice(j * FF_CHUNK, (j + 1) * FF_CHUNK)
        a = _dot(h, w1_ref[:, cols])
        b = _dot(h, w3_ref[:, cols])
        u = (_silu(a) * b).astype(BF)
        acc = acc + _dot(u, w2_ref[cols, :])
    out = s + 0.5 * gt_ref[...] * acc
    if tail is None:
        rest[0][...] = out
    elif tail == "final":
        gf_ref, o_ref = rest
        o_ref[...] = out * lax.rsqrt(jnp.mean(out * out, axis=-1, keepdims=True) + EPS) * gf_ref[...]
    else:
        (g2_ref, sh2_ref, sc2_ref, wq_ref, wkT_ref, wv_ref, wa_ref, waT_ref) = rest[:8]
        rest = rest[8:]
        if tail == "prep_fnet":
            wfn_ref, ccs_ref = rest[:2]
            rest = rest[2:]
        o_ref, h_ref, q_ref, kT_ref, v_ref, a_ref, aT_ref = rest[:7]
        o_ref[...] = out
        h2 = _norm_mod(out, g2_ref[...], sh2_ref[...], sc2_ref[...]).astype(BF)
        h_ref[...] = h2
        q_ref[...] = (_dot(h2, wq_ref[...]) * (DK ** -0.5)).astype(BF)
        kT_ref[...] = _dot_nt(wkT_ref[...], h2).astype(BF)
        v_ref[...] = _dot(h2, wv_ref[...]).astype(BF)
        a_ref[...] = _dot(h2, wa_ref[...]).astype(BF)
        aT_ref[...] = _dot_nt(waT_ref[...], h2).astype(BF)
        if tail == "prep_fnet":
            wr_ref, wi_ref = rest[7:]
            fn = _dot(h2, wfn_ref[...]).astype(BF)
            for g in range(NGROUP):
                pq = _dot(fn[:, g * GC:(g + 1) * GC], ccs_ref[...])
                wr_ref[:, g * GC:(g + 1) * GC] = pq[:, :GC]
                wi_ref[:, g * GC:(g + 1) * GC] = pq[:, GC:]


def _mod_spec(tiles_per_batch, per_batch):
    if per_batch:
        return pl.BlockSpec((None, 1, D), lambda i: (i // tiles_per_batch, 0, 0))
    return pl.BlockSpec((None, 1, D), lambda i: (0, 0, 0))


def _ffn(s, g, shift, scale, gate, w1, w3, w2, *, tm, tiles_per_batch, per_batch,
         g_final=None, prep=None, casts=()):
    n = s.shape[0]
    ms = _mod_spec(tiles_per_batch, per_batch)
    rows = lambda width: pl.BlockSpec((tm, width), lambda i: (i, 0))
    cols = lambda height: pl.BlockSpec((height, tm), lambda i: (0, i))
    in_specs = [
        rows(D), _const_spec((1, D)), ms, ms, ms,
        _const_spec((D, D_FF)), _const_spec((D, D_FF)), _const_spec((D_FF, D)),
    ]
    args = [s, g, shift, scale, gate, w1, w3, w2]
    out_specs = [rows(D)]
    out_shape = [jax.ShapeDtypeStruct((n, D), F32)]
    tail = None
    if g_final is not None:
        tail = "final"
        in_specs.append(_const_spec((1, D)))
        args.append(g_final)
    elif prep is not None:
        g2, shift2, scale2, w, with_fnet = prep
        tail = "prep_fnet" if with_fnet else "prep"
        in_specs += [_const_spec((1, D)), ms, ms,
                     _const_spec((D, HK)), _const_spec((HK, D)), _const_spec((D, BW)),
                     _const_spec((D, LANES)), _const_spec((2 * RANK, D))]
        args += [g2, shift2, scale2, w["q"], w["kT"], w["v"], w["a"], w["aT"]]
        out_specs += [rows(D), rows(HK), cols(HK), rows(BW), rows(LANES), cols(2 * RANK)]
        out_shape += [
            jax.ShapeDtypeStruct((n, D), BF), jax.ShapeDtypeStruct((n, HK), BF),
            jax.ShapeDtypeStruct((HK, n), BF), jax.ShapeDtypeStruct((n, BW), BF),
            jax.ShapeDtypeStruct((n, LANES), BF), jax.ShapeDtypeStruct((2 * RANK, n), BF),
        ]
        if with_fnet:
            in_specs += [_const_spec((D, BW)), _const_spec((GC, 2 * GC))]
            args += [w["fn"], w["ccs"]]
            out_specs += [rows(BW), rows(BW)]
            out_shape += [jax.ShapeDtypeStruct((n, BW), F32)] * 2
    body = functools.partial(_ffn_kernel, tail=tail)
    n_main = len(out_shape)
    if casts:
        body, c_in, c_args, c_out, c_shape = _with_casts(body, len(args), n_main, casts, n // tm)
        in_specs, args = in_specs + c_in, args + c_args
        out_specs, out_shape = out_specs + c_out, out_shape + c_shape
    res = pl.pallas_call(
        body,
        grid=(n // tm,),
        in_specs=in_specs,
        out_specs=out_specs,
        out_shape=out_shape,
        compiler_params=_params(1),
        name="ffn" if tail is None else "ffn_" + tail,
    )(*args)
    main = list(res[:n_main]) if prep is not None else [res[0]]
    if casts:
        main.append(list(res[n_main:]))
    return main if len(main) > 1 else main[0]


def _bd_mask():
    r = lax.broadcasted_iota(jnp.int32, (2 * DK, 2 * DV), 0) // DK
    c = lax.broadcasted_iota(jnp.int32, (2 * DK, 2 * DV), 1) // DV
    return r == c


def _gla_consts():
    c = GLA_C
    t = np.arange(c)
    sub = t // GLA_SUB
    lc = np.stack([t[None, :] <= t[:, None], t[None, :] >= t[:, None]]).astype(np.float32)
    tt, jj = t[:, None], t[None, :]
    sub_end = (sub * GLA_SUB + GLA_SUB - 1)[None, :]
    sub_start = (sub * GLA_SUB)[None, :]
    mk_f = (tt > jj) & (tt <= sub_end)
    mk_b = (tt < jj) & (tt >= sub_start)
    me_f = tt > jj
    me_b = tt < jj
    ones = np.ones((c, c), bool)
    mcat = np.stack([np.concatenate([mk_f, me_f, ones], axis=1),
                     np.concatenate([mk_b, me_b, ones], axis=1)]).astype(np.float32)
    return jnp.asarray(lc, BF), jnp.asarray(mcat, BF)


def _gla_direction(dr, q_ref, kT_ref, v_ref, a_ref, aT_ref, wa2_ref, wa2T_ref, ba2_ref, ba2T_ref,
                   lc_ref, mcat_ref, state_ref, o_ref, nchunk):
    c = GLA_C
    inv = 1.0 / GLA_NORMALIZER
    g = (_log_sigmoid(_dot(a_ref[...], wa2_ref[dr]) + ba2_ref[dr]) * inv).astype(BF)
    gT = (_log_sigmoid(_dot(wa2T_ref[dr], aT_ref[...]) + ba2T_ref[dr]) * inv).astype(BF)
    col_sub = lax.broadcasted_iota(jnp.int32, (2 * DK, c), 1) // GLA_SUB
    ii = lax.broadcasted_iota(jnp.int32, (c, c), 0)
    jj = lax.broadcasted_iota(jnp.int32, (c, c), 1)
    tri = (jj <= ii) if dr == 0 else (jj >= ii)
    low_half = lax.broadcasted_iota(jnp.int32, (c, LANES), 1) < DK
    bd = _bd_mask()
    zero_v = jnp.zeros((c, DV), BF)
    state = [state_ref[dr, 0], state_ref[dr, 1]]
    order = range(nchunk) if dr == 0 else range(nchunk - 1, -1, -1)
    sums = []
    for ci in order:
        rows = slice(ci * c, (ci + 1) * c)
        sums.append((_dot(lc_ref[dr], g[rows]),
                     _dot(gT[:, rows], mcat_ref[dr])))
    yield
    operands = []
    for ci, (b, fm) in zip(order, sums):
        rows = slice(ci * c, (ci + 1) * c)
        kc = kT_ref[:, rows].astype(F32)
        ksub = (kc * jnp.exp(fm[:, 0:c])).astype(BF)
        kend = (kc * jnp.exp(fm[:, c:2 * c])).astype(BF)
        dec = jnp.exp(fm[:, 2 * c:3 * c])
        qc = q_ref[rows, :].astype(F32)
        qb = (qc * jnp.exp(b)).astype(BF)
        for p in range(2):
            lanes = slice(p * LANES, (p + 1) * LANES)
            qp = qc[:, lanes]
            bp = b[:, lanes]
            kp = ksub[lanes]
            qblocks = []
            kblocks = []
            for j in range(GLA_NSUB):
                if dr == 0:
                    ref_row, lo, hi = j * GLA_SUB + GLA_SUB - 1, j * GLA_SUB, c
                else:
                    ref_row, lo, hi = j * GLA_SUB, 0, (j + 1) * GLA_SUB
                live = qp[lo:hi] * jnp.exp(bp[lo:hi] - bp[ref_row:ref_row + 1, :])
                parts = []
                if lo > 0:
                    parts.append(jnp.zeros((lo, LANES), F32))
                parts.append(live)
                if hi < c:
                    parts.append(jnp.zeros((c - hi, LANES), F32))
                qblocks.append(jnp.concatenate(parts, axis=0) if len(parts) > 1 else live)
                kblocks.append(jnp.where(col_sub == j, kp, jnp.zeros_like(kp)))
            hats = []
            for hd in range(2):
                qh = []
                for m in range(GLA_NSUB // 2):
                    even, odd = qblocks[2 * m], qblocks[2 * m + 1]
                    if hd == 0:
                        qh.append(jnp.where(low_half, even, pltpu.roll(odd, DK, 1)))
                    else:
                        qh.append(jnp.where(low_half, pltpu.roll(even, DK, 1), odd))
                hats.append((jnp.concatenate(qh, axis=1).astype(BF),
                             jnp.concatenate([kj[hd * DK:(hd + 1) * DK] for kj in kblocks], axis=0)))
            dp = dec[lanes]
            operands.append((ci, p, hats, qb[:, lanes], kend[lanes],
                             jnp.concatenate([dp, dp], axis=1)))
    yield
    scores = []
    for ci, p, hats, qbp, kendp, decay in operands:
        heads = [jnp.where(tri, _dot(qhat, khat), 0.0).astype(BF) for qhat, khat in hats]
        scores.append(jnp.concatenate(heads, axis=1))
    yield
    pending = []
    for (ci, p, hats, qbp, kendp, decay), att in zip(operands, scores):
        vp = v_ref[ci * c:(ci + 1) * c, p * 2 * DV:(p + 1) * 2 * DV]
        vbd = jnp.concatenate([
            jnp.concatenate([vp[:, :DV], zero_v], axis=1),
            jnp.concatenate([zero_v, vp[:, DV:]], axis=1)], axis=0)
        pending.append((ci, p, _dot(att, vbd), qbp, decay, jnp.where(bd, _dot(kendp, vp), 0.0)))
    yield
    for ci, p, intra, qbp, decay, upd in pending:
        o_p = intra + _dot(qbp, state[p].astype(BF))
        o_ref[ci * c:(ci + 1) * c, p * 2 * DV:(p + 1) * 2 * DV] = o_p.astype(BF)
        state[p] = decay * state[p] + upd
    state_ref[dr, 0] = state[0]
    state_ref[dr, 1] = state[1]


def _gla_kernel(qf_ref, kTf_ref, vf_ref, af_ref, aTf_ref, qb_ref, kTb_ref, vb_ref, ab_ref, aTb_ref,
                s0_ref, wa2_ref, wa2T_ref, ba2_ref, ba2T_ref, lc_ref, mcat_ref,
                of_ref, ob_ref, sfin_ref, state_ref, *, tm, tiles_per_batch):
    @pl.when(pl.program_id(0) % tiles_per_batch == 0)
    def _():
        state_ref[...] = s0_ref[...]

    shared = (wa2_ref, wa2T_ref, ba2_ref, ba2T_ref, lc_ref, mcat_ref, state_ref)
    scans = [
        _gla_direction(0, qf_ref, kTf_ref, vf_ref, af_ref, aTf_ref, *shared, of_ref, tm // GLA_C),
        _gla_direction(1, qb_ref, kTb_ref, vb_ref, ab_ref, aTb_ref, *shared, ob_ref, tm // GLA_C),
    ]
    live = list(scans)
    while live:
        live = [scan for scan in live if next(scan, "done") != "done"]
    sfin_ref[...] = state_ref[...]


def _gla(q, kT, v, a, aT, s0, w, *, tm, tiles_per_batch, casts=()):
    n = q.shape[0]
    nt = n // tm
    batch = nt // tiles_per_batch
    lc, mcat = _gla_consts()
    tpb = tiles_per_batch
    fwd = lambda i: i
    bwd = lambda i: (i // tpb) * tpb + (tpb - 1 - i % tpb)

    def tile_specs(idx):
        return [
            pl.BlockSpec((tm, HK), lambda i: (idx(i), 0)),
            pl.BlockSpec((HK, tm), lambda i: (0, idx(i))),
            pl.BlockSpec((tm, BW), lambda i: (idx(i), 0)),
            pl.BlockSpec((tm, LANES), lambda i: (idx(i), 0)),
            pl.BlockSpec((2 * RANK, tm), lambda i: (0, idx(i))),
        ]

    state_block = (None, 2, 2, 2 * DK, 2 * DV)
    in_specs = tile_specs(fwd) + tile_specs(bwd) + [
        pl.BlockSpec(state_block, lambda i: (i // tpb, 0, 0, 0, 0)),
        _const_spec((2, LANES, HK)), _const_spec((2, HK, 2 * RANK)),
        _const_spec((2, 1, HK)), _const_spec((2, HK, 1)),
        _const_spec((2, GLA_C, GLA_C)), _const_spec((2, GLA_C, 3 * GLA_C)),
    ]
    out_specs = [
        pl.BlockSpec((tm, BW), lambda i: (fwd(i), 0)),
        pl.BlockSpec((tm, BW), lambda i: (bwd(i), 0)),
        pl.BlockSpec(state_block, lambda i: (i // tpb, 0, 0, 0, 0)),
    ]
    out_shape = [
        jax.ShapeDtypeStruct((n, BW), BF), jax.ShapeDtypeStruct((n, BW), BF),
        jax.ShapeDtypeStruct((batch, 2, 2, 2 * DK, 2 * DV), F32),
    ]
    tiles = (q, kT, v, a, aT)
    args = [*tiles, *tiles, s0, w["a2"], w["a2T"], w["ba2"], w["ba2T"], lc, mcat]
    body = functools.partial(_gla_kernel, tm=tm, tiles_per_batch=tpb)
    if casts:
        body, c_in, c_args, c_out, c_shape = _with_casts(body, len(args), 3, casts, nt)
        in_specs, args = in_specs + c_in, args + c_args
        out_specs, out_shape = out_specs + c_out, out_shape + c_shape
    res = pl.pallas_call(
        body,
        grid=(nt,),
        in_specs=in_specs,
        out_specs=out_specs,
        out_shape=out_shape,
        scratch_shapes=[pltpu.VMEM((2, 2, 2 * DK, 2 * DV), F32)],
        compiler_params=_params(1),
        name="gla",
    )(*args)
    return (*res[:3], list(res[3:])) if casts else res


def _dft_cos_sin(n):
    k = np.arange(n)
    ang = 2.0 * np.pi * ((k[:, None] * k[None, :]) % n) / n
    return np.cos(ang), np.sin(ang)


def _fnet_kernel(x_ref, g_ref, sh_ref, sc_ref, wfn_ref, ccs_ref, f1_ref, tc_ref, ts_ref, f2_ref,
                 o_ref, zr_scr, zi_scr):
    j = pl.program_id(1)
    n_stage1 = FFT_N2 // FFT_SUB

    @pl.when(j < n_stage1)
    def _():
        rows = FFT_N1 * FFT_SUB
        x = x_ref[...].reshape(rows, D)
        h = _norm_mod(x, g_ref[...], sh_ref[...], sc_ref[...]).astype(BF)
        fn = _dot(h, wfn_ref[...]).astype(BF)
        re, im = [], []
        for g in range(NGROUP):
            pq = _dot(fn[:, g * GC:(g + 1) * GC], ccs_ref[...])
            re.append(pq[:, :GC].astype(BF))
            im.append(pq[:, GC:].astype(BF))
        w = jnp.concatenate([jnp.concatenate(re, axis=1), jnp.concatenate(im, axis=1)], axis=0)
        y = _dot(f1_ref[...], w)
        yr, yi = y[:rows], y[rows:]
        tc = jnp.concatenate([tc_ref[...]] * (BW // LANES), axis=1)
        ts = jnp.concatenate([ts_ref[...]] * (BW // LANES), axis=1)
        group = pl.ds(pl.multiple_of(j * FFT_SUB, FFT_SUB), FFT_SUB)
        zr_scr[:, group, :] = (yr * tc + yi * ts).reshape(FFT_N1, FFT_SUB, BW)
        zi_scr[:, group, :] = (yi * tc - yr * ts).reshape(FFT_N1, FFT_SUB, BW)

    @pl.when(j >= n_stage1)
    def _():
        base = (j - n_stage1) * FFT_K1_BLOCK
        for k in range(FFT_K1_BLOCK):
            z = jnp.concatenate([zr_scr[base + k].astype(BF), zi_scr[base + k].astype(BF)], axis=0)
            o_ref[k] = _dot(f2_ref[...], z)


def _fft2_kernel(zr_ref, zi_ref, f_ref, o_ref):
    for k in range(zr_ref.shape[0]):
        z = jnp.concatenate([zr_ref[k].astype(BF), zi_ref[k].astype(BF)], axis=0)
        o_ref[k] = _dot(f_ref[...], z)


def _fnet_latent(x, g, shift, scale, w, batch):
    n1, n2, sub = FFT_N1, FFT_N2, FFT_SUB
    rows = n1 * sub
    c1, s1 = _dft_cos_sin(n1)
    eye = np.eye(sub)
    f1 = np.block([[np.kron(c1, eye), np.kron(s1, eye)],
                   [np.kron(-s1, eye), np.kron(c1, eye)]]) / math.sqrt(n1)
    k1 = np.arange(n1)[None, :, None]
    m2 = np.arange(n2).reshape(n2 // sub, 1, sub)
    ang = (2.0 * np.pi * (k1 * m2) / (n1 * n2)).reshape(n2 // sub, rows, 1)
    tc = np.broadcast_to(np.cos(ang), (n2 // sub, rows, LANES)).astype(np.float32)
    ts = np.broadcast_to(np.sin(ang), (n2 // sub, rows, LANES)).astype(np.float32)
    c2, s2 = _dft_cos_sin(n2)
    f2 = np.concatenate([c2, s2], axis=1) / math.sqrt(n2 * GC)
    s1_steps, s2_steps = n2 // sub, n1 // FFT_K1_BLOCK
    group = lambda j: jnp.minimum(j, s1_steps - 1)
    mod = pl.BlockSpec((None, 1, D), lambda b, j: (b, 0, 0))
    tw = pl.BlockSpec((None, rows, LANES), lambda b, j: (group(j), 0, 0))
    return pl.pallas_call(
        _fnet_kernel,
        grid=(batch, s1_steps + s2_steps),
        in_specs=[pl.BlockSpec((None, n1, sub, D), lambda b, j: (b, 0, group(j), 0)),
                  _const_spec((1, D)), mod, mod, _const_spec((D, BW)), _const_spec((GC, 2 * GC)),
                  _const_spec((2 * rows, 2 * rows)), tw, tw, _const_spec((n2, 2 * n2))],
        out_specs=pl.BlockSpec((None, FFT_K1_BLOCK, n2, BW),
                               lambda b, j: (b, jnp.maximum(j - s1_steps, 0), 0, 0)),
        out_shape=jax.ShapeDtypeStruct((batch, n1, n2, BW), F32),
        scratch_shapes=[pltpu.VMEM((n1, n2, BW), F32), pltpu.VMEM((n1, n2, BW), F32)],
        compiler_params=_params(2, FNET_VMEM_LIMIT),
        name="fnet",
    )(x.reshape(batch, n1, n2, D), g, shift, scale, w["fn"], w["ccs"], _bf16_const(f1),
      jnp.asarray(tc), jnp.asarray(ts), _bf16_const(f2))


def _fft_direct(wr, wi, batch, seq):
    c2, s2 = _dft_cos_sin(seq)
    f2 = np.concatenate([c2, s2], axis=1) / math.sqrt(seq * GC)
    zblk = pl.BlockSpec((1, seq, BW), lambda b: (b, 0, 0))
    y = pl.pallas_call(
        _fft2_kernel,
        grid=(batch,),
        in_specs=[zblk, zblk, _const_spec((seq, 2 * seq))],
        out_specs=zblk,
        out_shape=jax.ShapeDtypeStruct((batch, seq, BW), F32),
        compiler_params=_params(1),
        name="fft_direct",
    )(wr.reshape(batch, seq, BW), wi.reshape(batch, seq, BW), _bf16_const(f2))
    return y.reshape(batch * seq, BW)


def _mix_kernel(*refs, tm, row_len, k1_major):
    (x_ref, h_ref, of_ref, ob_ref, yf_ref, gt_ref, wr_ref, go_ref, wsu_ref, wsv_ref,
     wcb_ref, wcc_ref, wcx_ref, wsgu_ref, bsgu_ref, wconv_ref, wbr_ref, wgate_ref,
     bgate_ref, wout_ref, o_ref, br_ref) = refs
    h = h_ref[...]
    if k1_major:
        yf = jnp.concatenate([yf_ref[:, k, :] for k in range(tm // FFT_N1)], axis=0).astype(BF)
    else:
        yf = yf_ref[...].astype(BF)

    r = _dot(h, wr_ref[...])
    su = _dot(h, wsu_ref[...])
    sv = _dot(h, wsv_ref[...])
    zc = _dot(h, wcc_ref[...]) * _dot(h, wcx_ref[...])
    cb = _dot(h, wcb_ref[...])

    gla = []
    for hd in range(HEADS):
        lanes = slice(hd * DV, (hd + 1) * DV)
        oh = of_ref[:, lanes].astype(F32) + ob_ref[:, lanes].astype(F32)
        on = oh * lax.rsqrt(jnp.mean(oh * oh, axis=-1, keepdims=True) + EPS) * go_ref[:, lanes]
        gla.append((on * _silu(r[:, lanes])).astype(BF))
    for g in range(NGROUP):
        lanes = slice(g * GC, (g + 1) * GC)
        vg = sv[:, lanes]
        vc = vg - jnp.mean(vg, axis=-1, keepdims=True)
        z = (vc * lax.rsqrt(jnp.mean(vc * vc, axis=-1, keepdims=True) + EPS)).astype(BF)
        for n in range(tm // SGU_CHUNK):
            rows = slice(n * SGU_CHUNK, (n + 1) * SGU_CHUNK)
            s = _dot(wsgu_ref[g], z[rows]) + bsgu_ref[:, lanes]
            br_ref[rows, lanes] = (su[rows, lanes] * s).astype(BF)
    pos = lax.broadcasted_iota(jnp.int32, (tm, BW), 0) % row_len
    left = jnp.where(pos == 0, 0.0, pltpu.roll(zc, 1, 0))
    right = jnp.where(pos == row_len - 1, 0.0, pltpu.roll(zc, tm - 1, 0))
    y = left * wconv_ref[0:1, :] + zc * wconv_ref[1:2, :] + right * wconv_ref[2:3, :]
    branches = (jnp.concatenate(gla, axis=1), br_ref[...], yf, (cb * y).astype(BF))

    m2 = None
    for k, branch in enumerate(branches):
        t = jnp.tanh(0.5 * (_dot(h, wgate_ref[k]) + bgate_ref[k]))
        proj = _dot(branch, wbr_ref[k])
        term = t * proj + proj
        m2 = term if m2 is None else m2 + term
    o_ref[...] = x_ref[...] + (0.5 * gt_ref[...]) * _dot(m2.astype(BF), wout_ref[...])


def _mix(x, h, of, ob, yf, gate, w, *, tm, tiles_per_batch, per_batch, row_len):
    n = x.shape[0]
    layer = (w["layer"],)
    ms = _mod_spec(tiles_per_batch, per_batch)
    tile = lambda width: pl.BlockSpec((tm, width), lambda i: (i, 0))
    k1_major = yf.ndim == 4
    if k1_major:
        k2_tile = tm // FFT_N1
        assert tm % FFT_N1 == 0 and k2_tile % 8 == 0
        tpb = tiles_per_batch
        yf_spec = pl.BlockSpec((None, FFT_N1, k2_tile, BW), lambda i: (i // tpb, 0, i % tpb, 0))
    else:
        yf_spec = tile(BW)
    in_specs = [
        tile(D), tile(D), tile(BW), tile(BW), yf_spec, ms,
        _const_spec((D, BW)), _const_spec((1, BW)),
        _const_spec((D, BW)), _const_spec((D, BW)), _const_spec((D, BW)), _const_spec((D, BW)),
        _const_spec((D, BW)),
        _pick_spec(layer, (NGROUP, SGU_CHUNK, SGU_CHUNK)), _const_spec((SGU_CHUNK, BW)),
        _const_spec((3, BW)), _const_spec((4, BW, D)), _const_spec((4, D, D)),
        _const_spec((4, 1, D)), _const_spec((D, D)),
    ]
    return pl.pallas_call(
        functools.partial(_mix_kernel, tm=tm, row_len=row_len, k1_major=k1_major),
        grid=(n // tm,),
        in_specs=in_specs,
        out_specs=tile(D),
        out_shape=jax.ShapeDtypeStruct((n, D), F32),
        scratch_shapes=[pltpu.VMEM((tm, BW), BF)],
        compiler_params=_params(1),
        name="mix",
    )(x, h, of, ob, yf, gate, w["r"], w["go"], w["su"], w["sv"], w["cb"], w["cc"], w["cx"],
      w["sgu"], w["bsgu"], w["conv"], w["branch"], w["gate"], w["bgate"], w["out"])


_IN_EDGES = np.cumsum([0, HK, HK, BW, BW, 2 * RANK, BW, BW, BW, BW, BW, BW])
_IN_NAMES = ("q", "k", "v", "r", "a", "su", "sv", "fn", "cb", "cc", "cx")


def _in_columns(w_in_layer):
    return {name: w_in_layer[:, _IN_EDGES[j]:_IN_EDGES[j + 1]].astype(BF)
            for j, name in enumerate(_IN_NAMES)}


def _layer_weights(i, w_in_layer, w_gla_a2, b_gla_a2, g_gla_norm, w_sgu, b_sgu, w_conv, w_branch,
                   w_gate, b_gate, w_out):
    cols = _in_columns(w_in_layer)
    q, k, v, r, a, su, sv, fn, cb, cc, cx = [cols[name] for name in _IN_NAMES]
    a2 = w_gla_a2[i].astype(BF)
    a2_tok = jnp.zeros((2, LANES, HK), BF)
    a2_tok = a2_tok.at[0, :RANK].set(a2[0]).at[1, RANK:2 * RANK].set(a2[1])
    cc_, sc_ = _dft_cos_sin(GC)
    return {
        "q": q, "kT": k.T, "v": v, "r": r,
        "a": jnp.pad(a, ((0, 0), (0, LANES - 2 * RANK))), "aT": a.T,
        "a2": a2_tok, "a2T": jnp.transpose(a2_tok[:, :2 * RANK, :], (0, 2, 1)),
        "ba2": b_gla_a2[i][:, None, :], "ba2T": b_gla_a2[i][:, :, None],
        "go": g_gla_norm[i].reshape(1, BW),
        "su": su, "sv": sv, "fn": fn, "cb": cb, "cc": cc, "cx": cx,
        "ccs": _bf16_const(np.concatenate([cc_, -sc_], axis=1)),
        "layer": i,
        "sgu": w_sgu.astype(BF),
        "merge_casts": (_cast_job(w_gate.reshape(DEPTH, 4 * D, D), (i,), 4 * D, D),
                        _cast_job(w_branch.reshape(DEPTH, 4 * BW, D), (i,), 4 * BW, D),
                        _cast_job(w_out, (i,), D, D)),
        "bsgu": jnp.repeat(jnp.transpose(b_sgu[i]), GC, axis=1),
        "conv": w_conv[i],
        "bgate": b_gate[i][:, None, :],
    }


def kernel(x, c, ctx, c_ctx, w_ada, b_ada, g_norm, w_ff1, w_ff3, w_ff2, w_in, w_gla_a2, b_gla_a2,
           g_gla_norm, w_sgu, b_sgu, w_conv, w_branch, w_gate, b_gate, w_out, g_final):
    batch, seq, _ = x.shape
    ctx_len = ctx.shape[1]
    assert seq % TM_LAT == 0 and TM_LAT % GRID_W == 0 and seq == FFT_N1 * FFT_N2
    assert seq % TM_FFN == 0
    assert ctx_len % GLA_C == 0
    tpb = seq // TM_LAT

    cvec = jnp.concatenate([c, c_ctx[None, :]], axis=0)
    cb = jnp.broadcast_to(cvec[:, :, None], (3, D, LANES))
    mods = _adaln(cb, w_ada, b_ada).reshape(DEPTH, 3, N_MOD, 1, D)

    xs = x.reshape(batch * seq, D)
    cs = ctx.reshape(batch * ctx_len, D)
    lat = dict(tm=TM_LAT, tiles_per_batch=tpb, per_batch=True)
    lat_ffn = dict(tm=TM_FFN, tiles_per_batch=seq // TM_FFN, per_batch=True)
    cx_ = dict(tm=batch * ctx_len, tiles_per_batch=1, per_batch=False)
    s_zero = jnp.zeros((batch, 2, 2, 2 * DK, 2 * DV), F32)

    def ff_casts(layer, half):
        lead = (layer, half)
        return (_cast_job(w_ff1, lead, D, D_FF), _cast_job(w_ff3, lead, D, D_FF),
                _cast_job(w_ff2, lead, D_FF, D))

    ff = [w_ff1[0, 0].astype(BF), w_ff3[0, 0].astype(BF), w_ff2[0, 0].astype(BF)]

    for i in range(DEPTH):
        last = i == DEPTH - 1
        ml = [mods[i, 0:2, j] for j in range(N_MOD)]
        mc = [mods[i, 2:3, j] for j in range(N_MOD)]
        gn = g_norm[i][:, None, :]
        w = _layer_weights(i, w_in[i], w_gla_a2, b_gla_a2, g_gla_norm, w_sgu, b_sgu, w_conv,
                           w_branch, w_gate, b_gate, w_out)

        xs, hl, *gla_l, merge_w = _ffn(xs, gn[0], ml[0], ml[1], ml[2], *ff,
                                       prep=(gn[1], ml[3], ml[4], w, False),
                                       casts=w["merge_casts"], **lat_ffn)
        w["gate"] = merge_w[0].reshape(4, D, D)
        w["branch"] = merge_w[1].reshape(4, BW, D)
        w["out"] = merge_w[2]
        cs, hc, *gla_c, wrc, wic = _ffn(cs, gn[0], mc[0], mc[1], mc[2], *ff,
                                        prep=(gn[1], mc[3], mc[4], w, True), **cx_)

        ofc, obc, s_ctx = _gla(*gla_c, s_zero, w, tm=ctx_len, tiles_per_batch=1)
        ofl, obl, _, ff = _gla(*gla_l, s_ctx, w, tm=TM_LAT, tiles_per_batch=tpb,
                               casts=ff_casts(i, 1))
        yf = _fnet_latent(xs.reshape(batch, seq, D), gn[1], ml[3], ml[4], w, batch)
        xs = _mix(xs, hl, ofl, obl, yf, ml[5], w, row_len=GRID_W, **lat)
        if last:
            xs = _ffn(xs, gn[2], ml[6], ml[7], ml[8], *ff, g_final=g_final[None, :], **lat_ffn)
        else:
            yfc = _fft_direct(wrc, wic, batch, ctx_len)
            cs = _mix(cs, hc, ofc, obc, yfc, mc[5], w, row_len=ctx_len, **cx_)
            cs = _ffn(cs, gn[2], mc[6], mc[7], mc[8], *ff, **cx_)
            xs, ff = _ffn(xs, gn[2], ml[6], ml[7], ml[8], *ff, casts=ff_casts(i + 1, 0), **lat_ffn)
    return xs.reshape(batch, seq, D)
```

```python
import functools
import math

import numpy as np
import jax
import jax.numpy as jnp
from jax import lax
from jax.experimental import pallas as pl
from jax.experimental.pallas import tpu as pltpu

D = 1024
DEPTH = 2
GRID_W = 64
N_MOD = 9
D_FF = 2816
BW = 512
HEADS = 4
DV = 128
DK = 64
HK = HEADS * DK
RANK = 16
GLA_NORMALIZER = 16.0
SGU_CHUNK = 128
GC = 128
NGROUP = 4
EPS = 1e-6
LANES = 128

TM_LAT = 512
TM_FFN = 1024
FF_CHUNK = 256
GLA_C = 128
GLA_SUB = 32
GLA_NSUB = GLA_C // GLA_SUB
FFT_N1 = 64
FFT_N2 = 128
FFT_SUB = 8
FFT_K1_BLOCK = 8
ADA_TN = 2304
VMEM_LIMIT = 56 * 1024 * 1024
FNET_VMEM_LIMIT = 60 * 1024 * 1024

BF = jnp.bfloat16
F32 = jnp.float32

_NT = (((1,), (1,)), ((), ()))


def _dot(a, b):
    return jnp.dot(a, b, preferred_element_type=F32)


def _dot_nt(a, b):
    return lax.dot_general(a, b, _NT, preferred_element_type=F32)


def _sigmoid(x):
    return 0.5 * jnp.tanh(0.5 * x) + 0.5


def _silu(x):
    return x * _sigmoid(x)


def _log_sigmoid(x):
    return jnp.minimum(x, 0.0) - jnp.log(1.0 + jnp.exp(-jnp.abs(x)))


def _bf16_const(a):
    return jnp.asarray(a, F32).astype(BF)


def _const_spec(shape):
    nd = len(shape)
    return pl.BlockSpec(shape, lambda *_: (0,) * nd, pipeline_mode=pl.Buffered(1))


def _pick_spec(lead, shape):
    nd = len(shape)
    return pl.BlockSpec((None,) * len(lead) + tuple(shape), lambda *_: tuple(lead) + (0,) * nd,
                        pipeline_mode=pl.Buffered(1))


def _cast_job(src, lead, rows, cols):
    return (src, tuple(lead), rows, cols)


def _with_casts(kernel_fn, n_in, n_out, jobs, n_steps):
    bf16_rows = 16
    chunks = [max(k for k in range(1, n_steps + 1) if rows % (k * bf16_rows) == 0)
              for _, _, rows, _ in jobs]
    in_specs, args, out_specs, out_shape = [], [], [], []
    for (src, lead, rows, cols), ch in zip(jobs, chunks):
        r = rows // ch
        in_specs.append(pl.BlockSpec(
            (None,) * len(lead) + (r, cols),
            lambda i, lead=lead, ch=ch: lead + (jnp.minimum(i, ch - 1), 0)))
        out_specs.append(pl.BlockSpec((r, cols), lambda i, ch=ch: (jnp.minimum(i, ch - 1), 0)))
        args.append(src)
        out_shape.append(jax.ShapeDtypeStruct((rows, cols), BF))
    k = len(jobs)

    def kernel(*refs):
        ins, cast_in = refs[:n_in], refs[n_in:n_in + k]
        outs = refs[n_in + k:n_in + k + n_out]
        cast_out = refs[n_in + k + n_out:n_in + 2 * k + n_out]
        kernel_fn(*ins, *outs, *refs[n_in + 2 * k + n_out:])
        step = pl.program_id(0)
        for src_ref, dst_ref, ch in zip(cast_in, cast_out, chunks):
            if ch == n_steps:
                dst_ref[...] = src_ref[...].astype(BF)
            else:
                @pl.when(step < ch)
                def _(src_ref=src_ref, dst_ref=dst_ref):
                    dst_ref[...] = src_ref[...].astype(BF)

    return kernel, in_specs, args, out_specs, out_shape


def _params(n_grid, vmem_limit=None):
    return pltpu.CompilerParams(dimension_semantics=("arbitrary",) * n_grid,
                                vmem_limit_bytes=vmem_limit or VMEM_LIMIT)


def _norm_mod(x, g, shift, scale):
    hn = x * lax.rsqrt(jnp.mean(x * x, axis=-1, keepdims=True) + EPS) * g
    return hn * (1.0 + scale) + shift


def _adaln_kernel(cb_ref, w_ref, b_ref, o_ref):
    for r in range(3):
        cv = cb_ref[r]
        s = _silu(cv)
        for j in range(ADA_TN // LANES):
            w = w_ref[:, j * LANES:(j + 1) * LANES]
            o_ref[r:r + 1, j * LANES:(j + 1) * LANES] = (
                jnp.sum(w * s, axis=0, keepdims=True) + b_ref[:, j * LANES:(j + 1) * LANES])


def _adaln(cb, w_ada, b_ada):
    nmod = N_MOD * D
    return pl.pallas_call(
        _adaln_kernel,
        grid=(DEPTH, nmod // ADA_TN),
        in_specs=[
            pl.BlockSpec((3, D, LANES), lambda l, j: (0, 0, 0)),
            pl.BlockSpec((None, D, ADA_TN), lambda l, j: (l, 0, j)),
            pl.BlockSpec((None, 1, ADA_TN), lambda l, j: (l, 0, j)),
        ],
        out_specs=pl.BlockSpec((None, 3, ADA_TN), lambda l, j: (l, 0, j)),
        out_shape=jax.ShapeDtypeStruct((DEPTH, 3, nmod), F32),
        compiler_params=_params(2),
        name="adaln",
    )(cb, w_ada, b_ada.reshape(DEPTH, 1, nmod))


def _ffn_kernel(*refs, tail):
    s_ref, g_ref, sh_ref, sc_ref, gt_ref, w1_ref, w3_ref, w2_ref = refs[:8]
    rest = refs[8:]
    s = s_ref[...]
    h = _norm_mod(s, g_ref[...], sh_ref[...], sc_ref[...]).astype(BF)
    acc = jnp.zeros(s.shape, F32)
    for j in range(D_FF // FF_CHUNK):
        cols = slice(j * FF_CHUNK, (j + 1) * FF_CHUNK)
        a = _dot(h, w1_ref[:, cols])
        b = _dot(h, w3_ref[:, cols])
        u = (_silu(a) * b).astype(BF)
        acc = acc + _dot(u, w2_ref[cols, :])
    out = s + 0.5 * gt_ref[...] * acc
    if tail is None:
        rest[0][...] = out
    elif tail == "final":
        gf_ref, o_ref = rest
        o_ref[...] = out * lax.rsqrt(jnp.mean(out * out, axis=-1, keepdims=True) + EPS) * gf_ref[...]
    else:
        (g2_ref, sh2_ref, sc2_ref, wq_ref, wkT_ref, wv_ref, wa_ref, waT_ref) = rest[:8]
        rest = rest[8:]
        if tail == "prep_fnet":
            wfn_ref, ccs_ref = rest[:2]
            rest = rest[2:]
        o_ref, h_ref, q_ref, kT_ref, v_ref, a_ref, aT_ref = rest[:7]
        o_ref[...] = out
        h2 = _norm_mod(out, g2_ref[...], sh2_ref[...], sc2_ref[...]).astype(BF)
        h_ref[...] = h2
        q_ref[...] = (_dot(h2, wq_ref[...]) * (DK ** -0.5)).astype(BF)
        kT_ref[...] = _dot_nt(wkT_ref[...], h2).astype(BF)
        v_ref[...] = _dot(h2, wv_ref[...]).astype(BF)
        a_ref[...] = _dot(h2, wa_ref[...]).astype(BF)
        aT_ref[...] = _dot_nt(waT_ref[...], h2).astype(BF)
        if tail == "prep_fnet":
            wr_ref, wi_ref = rest[7:]
            fn = _dot(h2, wfn_ref[...]).astype(BF)
            for g in range(NGROUP):
                pq = _dot(fn[:, g * GC:(g + 1) * GC], ccs_ref[...])
                wr_ref[:, g * GC:(g + 1) * GC] = pq[:, :GC]
                wi_ref[:, g * GC:(g + 1) * GC] = pq[:, GC:]


def _mod_spec(tiles_per_batch, per_batch):
    if per_batch:
        return pl.BlockSpec((None, 1, D), lambda i: (i // tiles_per_batch, 0, 0))
    return pl.BlockSpec((None, 1, D), lambda i: (0, 0, 0))


def _ffn(s, g, shift, scale, gate, w1, w3, w2, *, tm, tiles_per_batch, per_batch,
         g_final=None, prep=None, casts=()):
    n = s.shape[0]
    ms = _mod_spec(tiles_per_batch, per_batch)
    rows = lambda width: pl.BlockSpec((tm, width), lambda i: (i, 0))
    cols = lambda height: pl.BlockSpec((height, tm), lambda i: (0, i))
    in_specs = [
        rows(D), _const_spec((1, D)), ms, ms, ms,
        _const_spec((D, D_FF)), _const_spec((D, D_FF)), _const_spec((D_FF, D)),
    ]
    args = [s, g, shift, scale, gate, w1, w3, w2]
    out_specs = [rows(D)]
    out_shape = [jax.ShapeDtypeStruct((n, D), F32)]
    tail = None
    if g_final is not None:
        tail = "final"
        in_specs.append(_const_spec((1, D)))
        args.append(g_final)
    elif prep is not None:
        g2, shift2, scale2, w, with_fnet = prep
        tail = "prep_fnet" if with_fnet else "prep"
        in_specs += [_const_spec((1, D)), ms, ms,
                     _const_spec((D, HK)), _const_spec((HK, D)), _const_spec((D, BW)),
                     _const_spec((D, LANES)), _const_spec((2 * RANK, D))]
        args += [g2, shift2, scale2, w["q"], w["kT"], w["v"], w["a"], w["aT"]]
        out_specs += [rows(D), rows(HK), cols(HK), rows(BW), rows(LANES), cols(2 * RANK)]
        out_shape += [
            jax.ShapeDtypeStruct((n, D), BF), jax.ShapeDtypeStruct((n, HK), BF),
            jax.ShapeDtypeStruct((HK, n), BF), jax.ShapeDtypeStruct((n, BW), BF),
            jax.ShapeDtypeStruct((n, LANES), BF), jax.ShapeDtypeStruct((2 * RANK, n), BF),
        ]
        if with_fnet:
            in_specs += [_const_spec((D, BW)), _const_spec((GC, 2 * GC))]
            args += [w["fn"], w["ccs"]]
            out_specs += [rows(BW), rows(BW)]
            out_shape += [jax.ShapeDtypeStruct((n, BW), F32)] * 2
    body = functools.partial(_ffn_kernel, tail=tail)
    n_main = len(out_shape)
    if casts:
        body, c_in, c_args, c_out, c_shape = _with_casts(body, len(args), n_main, casts, n // tm)
        in_specs, args = in_specs + c_in, args + c_args
        out_specs, out_shape = out_specs + c_out, out_shape + c_shape
    res = pl.pallas_call(
        body,
        grid=(n // tm,),
        in_specs=in_specs,
        out_specs=out_specs,
        out_shape=out_shape,
        compiler_params=_params(1),
        name="ffn" if tail is None else "ffn_" + tail,
    )(*args)
    main = list(res[:n_main]) if prep is not None else [res[0]]
    if casts:
        main.append(list(res[n_main:]))
    return main if len(main) > 1 else main[0]


def _bd_mask():
    r = lax.broadcasted_iota(jnp.int32, (2 * DK, 2 * DV), 0) // DK
    c = lax.broadcasted_iota(jnp.int32, (2 * DK, 2 * DV), 1) // DV
    return r == c


def _gla_consts():
    c = GLA_C
    t = np.arange(c)
    sub = t // GLA_SUB
    lc = np.stack([t[None, :] <= t[:, None], t[None, :] >= t[:, None]]).astype(np.float32)
    tt, jj = t[:, None], t[None, :]
    sub_end = (sub * GLA_SUB + GLA_SUB - 1)[None, :]
    sub_start = (sub * GLA_SUB)[None, :]
    mk_f = (tt > jj) & (tt <= sub_end)
    mk_b = (tt < jj) & (tt >= sub_start)
    me_f = tt > jj
    me_b = tt < jj
    ones = np.ones((c, c), bool)
    mcat = np.stack([np.concatenate([mk_f, me_f, ones], axis=1),
                     np.concatenate([mk_b, me_b, ones], axis=1)]).astype(np.float32)
    return jnp.asarray(lc, BF), jnp.asarray(mcat, BF)


def _gla_direction(dr, q_ref, kT_ref, v_ref, a_ref, aT_ref, wa2_ref, wa2T_ref, ba2_ref, ba2T_ref,
                   lc_ref, mcat_ref, state_ref, o_ref, nchunk):
    c = GLA_C
    inv = 1.0 / GLA_NORMALIZER
    g = (_log_sigmoid(_dot(a_ref[...], wa2_ref[dr]) + ba2_ref[dr]) * inv).astype(BF)
    gT = (_log_sigmoid(_dot(wa2T_ref[dr], aT_ref[...]) + ba2T_ref[dr]) * inv).astype(BF)
    col_sub = lax.broadcasted_iota(jnp.int32, (2 * DK, c), 1) // GLA_SUB
    ii = lax.broadcasted_iota(jnp.int32, (c, c), 0)
    jj = lax.broadcasted_iota(jnp.int32, (c, c), 1)
    tri = (jj <= ii) if dr == 0 else (jj >= ii)
    low_half = lax.broadcasted_iota(jnp.int32, (c, LANES), 1) < DK
    bd = _bd_mask()
    zero_v = jnp.zeros((c, DV), BF)
    state = [state_ref[dr, 0], state_ref[dr, 1]]
    order = range(nchunk) if dr == 0 else range(nchunk - 1, -1, -1)
    sums = []
    for ci in order:
        rows = slice(ci * c, (ci + 1) * c)
        sums.append((_dot(lc_ref[dr], g[rows]),
                     _dot(gT[:, rows], mcat_ref[dr])))
    yield
    operands = []
    for ci, (b, fm) in zip(order, sums):
        rows = slice(ci * c, (ci + 1) * c)
        kc = kT_ref[:, rows].astype(F32)
        ksub = (kc * jnp.exp(fm[:, 0:c])).astype(BF)
        kend = (kc * jnp.exp(fm[:, c:2 * c])).astype(BF)
        dec = jnp.exp(fm[:, 2 * c:3 * c])
        qc = q_ref[rows, :].astype(F32)
        qb = (qc * jnp.exp(b)).astype(BF)
        for p in range(2):
            lanes = slice(p * LANES, (p + 1) * LANES)
            qp = qc[:, lanes]
            bp = b[:, lanes]
            kp = ksub[lanes]
            qblocks = []
            kblocks = []
            for j in range(GLA_NSUB):
                if dr == 0:
                    ref_row, lo, hi = j * GLA_SUB + GLA_SUB - 1, j * GLA_SUB, c
                else:
                    ref_row, lo, hi = j * GLA_SUB, 0, (j + 1) * GLA_SUB
                live = qp[lo:hi] * jnp.exp(bp[lo:hi] - bp[ref_row:ref_row + 1, :])
                parts = []
                if lo > 0:
                    parts.append(jnp.zeros((lo, LANES), F32))
                parts.append(live)
                if hi < c:
                    parts.append(jnp.zeros((c - hi, LANES), F32))
                qblocks.append(jnp.concatenate(parts, axis=0) if len(parts) > 1 else live)
                kblocks.append(jnp.where(col_sub == j, kp, jnp.zeros_like(kp)))
            hats = []
            for hd in range(2):
                qh = []
                for m in range(GLA_NSUB // 2):
                    even, odd = qblocks[2 * m], qblocks[2 * m + 1]
                    if hd == 0:
                        qh.append(jnp.where(low_half, even, pltpu.roll(odd, DK, 1)))
                    else:
                        qh.append(jnp.where(low_half, pltpu.roll(even, DK, 1), odd))
                hats.append((jnp.concatenate(qh, axis=1).astype(BF),
                             jnp.concatenate([kj[hd * DK:(hd + 1) * DK] for kj in kblocks], axis=0)))
            dp = dec[lanes]
            operands.append((ci, p, hats, qb[:, lanes], kend[lanes],
                             jnp.concatenate([dp, dp], axis=1)))
    yield
    scores = []
    for ci, p, hats, qbp, kendp, decay in operands:
        heads = [jnp.where(tri, _dot(qhat, khat), 0.0).astype(BF) for qhat, khat in hats]
        scores.append(jnp.concatenate(heads, axis=1))
    yield
    pending = []
    for (ci, p, hats, qbp, kendp, decay), att in zip(operands, scores):
        vp = v_ref[ci * c:(ci + 1) * c, p * 2 * DV:(p + 1) * 2 * DV]
        vbd = jnp.concatenate([
            jnp.concatenate([vp[:, :DV], zero_v], axis=1),
            jnp.concatenate([zero_v, vp[:, DV:]], axis=1)], axis=0)
        pending.append((ci, p, _dot(att, vbd), qbp, decay, jnp.where(bd, _dot(kendp, vp), 0.0)))
    yield
    for ci, p, intra, qbp, decay, upd in pending:
        o_p = intra + _dot(qbp, state[p].astype(BF))
        o_ref[ci * c:(ci + 1) * c, p * 2 * DV:(p + 1) * 2 * DV] = o_p.astype(BF)
        state[p] = decay * state[p] + upd
    state_ref[dr, 0] = state[0]
    state_ref[dr, 1] = state[1]


def _gla_kernel(qf_ref, kTf_ref, vf_ref, af_ref, aTf_ref, qb_ref, kTb_ref, vb_ref, ab_ref, aTb_ref,
                s0_ref, wa2_ref, wa2T_ref, ba2_ref, ba2T_ref, lc_ref, mcat_ref,
                of_ref, ob_ref, sfin_ref, state_ref, *, tm, tiles_per_batch):
    @pl.when(pl.program_id(0) % tiles_per_batch == 0)
    def _():
        state_ref[...] = s0_ref[...]

    shared = (wa2_ref, wa2T_ref, ba2_ref, ba2T_ref, lc_ref, mcat_ref, state_ref)
    scans = [
        _gla_direction(0, qf_ref, kTf_ref, vf_ref, af_ref, aTf_ref, *shared, of_ref, tm // GLA_C),
        _gla_direction(1, qb_ref, kTb_ref, vb_ref, ab_ref, aTb_ref, *shared, ob_ref, tm // GLA_C),
    ]
    live = list(scans)
    while live:
        live = [scan for scan in live if next(scan, "done") != "done"]
    sfin_ref[...] = state_ref[...]


def _gla(q, kT, v, a, aT, s0, w, *, tm, tiles_per_batch, casts=()):
    n = q.shape[0]
    nt = n // tm
    batch = nt // tiles_per_batch
    lc, mcat = _gla_consts()
    tpb = tiles_per_batch
    fwd = lambda i: i
    bwd = lambda i: (i // tpb) * tpb + (tpb - 1 - i % tpb)

    def tile_specs(idx):
        return [
            pl.BlockSpec((tm, HK), lambda i: (idx(i), 0)),
            pl.BlockSpec((HK, tm), lambda i: (0, idx(i))),
            pl.BlockSpec((tm, BW), lambda i: (idx(i), 0)),
            pl.BlockSpec((tm, LANES), lambda i: (idx(i), 0)),
            pl.BlockSpec((2 * RANK, tm), lambda i: (0, idx(i))),
        ]

    state_block = (None, 2, 2, 2 * DK, 2 * DV)
    in_specs = tile_specs(fwd) + tile_specs(bwd) + [
        pl.BlockSpec(state_block, lambda i: (i // tpb, 0, 0, 0, 0)),
        _const_spec((2, LANES, HK)), _const_spec((2, HK, 2 * RANK)),
        _const_spec((2, 1, HK)), _const_spec((2, HK, 1)),
        _const_spec((2, GLA_C, GLA_C)), _const_spec((2, GLA_C, 3 * GLA_C)),
    ]
    out_specs = [
        pl.BlockSpec((tm, BW), lambda i: (fwd(i), 0)),
        pl.BlockSpec((tm, BW), lambda i: (bwd(i), 0)),
        pl.BlockSpec(state_block, lambda i: (i // tpb, 0, 0, 0, 0)),
    ]
    out_shape = [
        jax.ShapeDtypeStruct((n, BW), BF), jax.ShapeDtypeStruct((n, BW), BF),
        jax.ShapeDtypeStruct((batch, 2, 2, 2 * DK, 2 * DV), F32),
    ]
    tiles = (q, kT, v, a, aT)
    args = [*tiles, *tiles, s0, w["a2"], w["a2T"], w["ba2"], w["ba2T"], lc, mcat]
    body = functools.partial(_gla_kernel, tm=tm, tiles_per_batch=tpb)
    if casts:
        body, c_in, c_args, c_out, c_shape = _with_casts(body, len(args), 3, casts, nt)
        in_specs, args = in_specs + c_in, args + c_args
        out_specs, out_shape = out_specs + c_out, out_shape + c_shape
    res = pl.pallas_call(
        body,
        grid=(nt,),
        in_specs=in_specs,
        out_specs=out_specs,
        out_shape=out_shape,
        scratch_shapes=[pltpu.VMEM((2, 2, 2 * DK, 2 * DV), F32)],
        compiler_params=_params(1),
        name="gla",
    )(*args)
    return (*res[:3], list(res[3:])) if casts else res


def _dft_cos_sin(n):
    k = np.arange(n)
    ang = 2.0 * np.pi * ((k[:, None] * k[None, :]) % n) / n
    return np.cos(ang), np.sin(ang)


def _fnet_kernel(x_ref, g_ref, sh_ref, sc_ref, wfn_ref, ccs_ref, f1_ref, tc_ref, ts_ref, f2_ref,
                 o_ref, zr_scr, zi_scr):
    j = pl.program_id(1)
    n_stage1 = FFT_N2 // FFT_SUB

    @pl.when(j < n_stage1)
    def _():
        rows = FFT_N1 * FFT_SUB
        x = x_ref[...].reshape(rows, D)
        h = _norm_mod(x, g_ref[...], sh_ref[...], sc_ref[...]).astype(BF)
        fn = _dot(h, wfn_ref[...]).astype(BF)
        re, im = [], []
        for g in range(NGROUP):
            pq = _dot(fn[:, g * GC:(g + 1) * GC], ccs_ref[...])
            re.append(pq[:, :GC].astype(BF))
            im.append(pq[:, GC:].astype(BF))
        w = jnp.concatenate([jnp.concatenate(re, axis=1), jnp.concatenate(im, axis=1)], axis=0)
        y = _dot(f1_ref[...], w)
        yr, yi = y[:rows], y[rows:]
        tc = jnp.concatenate([tc_ref[...]] * (BW // LANES), axis=1)
        ts = jnp.concatenate([ts_ref[...]] * (BW // LANES), axis=1)
        group = pl.ds(pl.multiple_of(j * FFT_SUB, FFT_SUB), FFT_SUB)
        zr_scr[:, group, :] = (yr * tc + yi * ts).reshape(FFT_N1, FFT_SUB, BW)
        zi_scr[:, group, :] = (yi * tc - yr * ts).reshape(FFT_N1, FFT_SUB, BW)

    @pl.when(j >= n_stage1)
    def _():
        base = (j - n_stage1) * FFT_K1_BLOCK
        for k in range(FFT_K1_BLOCK):
            z = jnp.concatenate([zr_scr[base + k].astype(BF), zi_scr[base + k].astype(BF)], axis=0)
            o_ref[k] = _dot(f2_ref[...], z)


def _fft2_kernel(zr_ref, zi_ref, f_ref, o_ref):
    for k in range(zr_ref.shape[0]):
        z = jnp.concatenate([zr_ref[k].astype(BF), zi_ref[k].astype(BF)], axis=0)
        o_ref[k] = _dot(f_ref[...], z)


def _fnet_latent(x, g, shift, scale, w, batch):
    n1, n2, sub = FFT_N1, FFT_N2, FFT_SUB
    rows = n1 * sub
    c1, s1 = _dft_cos_sin(n1)
    eye = np.eye(sub)
    f1 = np.block([[np.kron(c1, eye), np.kron(s1, eye)],
                   [np.kron(-s1, eye), np.kron(c1, eye)]]) / math.sqrt(n1)
    k1 = np.arange(n1)[None, :, None]
    m2 = np.arange(n2).reshape(n2 // sub, 1, sub)
    ang = (2.0 * np.pi * (k1 * m2) / (n1 * n2)).reshape(n2 // sub, rows, 1)
    tc = np.broadcast_to(np.cos(ang), (n2 // sub, rows, LANES)).astype(np.float32)
    ts = np.broadcast_to(np.sin(ang), (n2 // sub, rows, LANES)).astype(np.float32)
    c2, s2 = _dft_cos_sin(n2)
    f2 = np.concatenate([c2, s2], axis=1) / math.sqrt(n2 * GC)
    s1_steps, s2_steps = n2 // sub, n1 // FFT_K1_BLOCK
    group = lambda j: jnp.minimum(j, s1_steps - 1)
    mod = pl.BlockSpec((None, 1, D), lambda b, j: (b, 0, 0))
    tw = pl.BlockSpec((None, rows, LANES), lambda b, j: (group(j), 0, 0))
    return pl.pallas_call(
        _fnet_kernel,
        grid=(batch, s1_steps + s2_steps),
        in_specs=[pl.BlockSpec((None, n1, sub, D), lambda b, j: (b, 0, group(j), 0)),
                  _const_spec((1, D)), mod, mod, _const_spec((D, BW)), _const_spec((GC, 2 * GC)),
                  _const_spec((2 * rows, 2 * rows)), tw, tw, _const_spec((n2, 2 * n2))],
        out_specs=pl.BlockSpec((None, FFT_K1_BLOCK, n2, BW),
                               lambda b, j: (b, jnp.maximum(j - s1_steps, 0), 0, 0)),
        out_shape=jax.ShapeDtypeStruct((batch, n1, n2, BW), F32),
        scratch_shapes=[pltpu.VMEM((n1, n2, BW), F32), pltpu.VMEM((n1, n2, BW), F32)],
        compiler_params=_params(2, FNET_VMEM_LIMIT),
        name="fnet",
    )(x.reshape(batch, n1, n2, D), g, shift, scale, w["fn"], w["ccs"], _bf16_const(f1),
      jnp.asarray(tc), jnp.asarray(ts), _bf16_const(f2))


def _fft_direct(wr, wi, batch, seq):
    c2, s2 = _dft_cos_sin(seq)
    f2 = np.concatenate([c2, s2], axis=1) / math.sqrt(seq * GC)
    zblk = pl.BlockSpec((1, seq, BW), lambda b: (b, 0, 0))
    y = pl.pallas_call(
        _fft2_kernel,
        grid=(batch,),
        in_specs=[zblk, zblk, _const_spec((seq, 2 * seq))],
        out_specs=zblk,
        out_shape=jax.ShapeDtypeStruct((batch, seq, BW), F32),
        compiler_params=_params(1),
        name="fft_direct",
    )(wr.reshape(batch, seq, BW), wi.reshape(batch, seq, BW), _bf16_const(f2))
    return y.reshape(batch * seq, BW)


def _mix_kernel(*refs, tm, row_len, k1_major):
    (x_ref, h_ref, of_ref, ob_ref, yf_ref, gt_ref, wr_ref, go_ref, wsu_ref, wsv_ref,
     wcb_ref, wcc_ref, wcx_ref, wsgu_ref, bsgu_ref, wconv_ref, wbr_ref, wgate_ref,
     bgate_ref, wout_ref, o_ref, br_ref) = refs
    h = h_ref[...]
    if k1_major:
        yf = jnp.concatenate([yf_ref[:, k, :] for k in range(tm // FFT_N1)], axis=0).astype(BF)
    else:
        yf = yf_ref[...].astype(BF)

    r = _dot(h, wr_ref[...])
    su = _dot(h, wsu_ref[...])
    sv = _dot(h, wsv_ref[...])
    zc = _dot(h, wcc_ref[...]) * _dot(h, wcx_ref[...])
    cb = _dot(h, wcb_ref[...])

    gla = []
    for hd in range(HEADS):
        lanes = slice(hd * DV, (hd + 1) * DV)
        oh = of_ref[:, lanes].astype(F32) + ob_ref[:, lanes].astype(F32)
        on = oh * lax.rsqrt(jnp.mean(oh * oh, axis=-1, keepdims=True) + EPS) * go_ref[:, lanes]
        gla.append((on * _silu(r[:, lanes])).astype(BF))
    for g in range(NGROUP):
        lanes = slice(g * GC, (g + 1) * GC)
        vg = sv[:, lanes]
        vc = vg - jnp.mean(vg, axis=-1, keepdims=True)
        z = (vc * lax.rsqrt(jnp.mean(vc * vc, axis=-1, keepdims=True) + EPS)).astype(BF)
        for n in range(tm // SGU_CHUNK):
            rows = slice(n * SGU_CHUNK, (n + 1) * SGU_CHUNK)
            s = _dot(wsgu_ref[g], z[rows]) + bsgu_ref[:, lanes]
            br_ref[rows, lanes] = (su[rows, lanes] * s).astype(BF)
    pos = lax.broadcasted_iota(jnp.int32, (tm, BW), 0) % row_len
    left = jnp.where(pos == 0, 0.0, pltpu.roll(zc, 1, 0))
    right = jnp.where(pos == row_len - 1, 0.0, pltpu.roll(zc, tm - 1, 0))
    y = left * wconv_ref[0:1, :] + zc * wconv_ref[1:2, :] + right * wconv_ref[2:3, :]
    branches = (jnp.concatenate(gla, axis=1), br_ref[...], yf, (cb * y).astype(BF))

    halves = []
    for c0 in range(0, D, D // 2):
        cols = slice(c0, c0 + D // 2)
        m2 = None
        for k, branch in enumerate(branches):
            t = jnp.tanh(0.5 * (_dot(h, wgate_ref[k, :, cols]) + bgate_ref[k, :, cols]))
            proj = _dot(branch, wbr_ref[k, :, cols])
            term = t * proj + proj
            m2 = term if m2 is None else m2 + term
        halves.append(m2.astype(BF))
    merged = jnp.concatenate(halves, axis=1)
    o_ref[...] = x_ref[...] + (0.5 * gt_ref[...]) * _dot(merged, wout_ref[...])


def _mix(x, h, of, ob, yf, gate, w, *, tm, tiles_per_batch, per_batch, row_len):
    n = x.shape[0]
    layer = (w["layer"],)
    ms = _mod_spec(tiles_per_batch, per_batch)
    tile = lambda width: pl.BlockSpec((tm, width), lambda i: (i, 0))
    k1_major = yf.ndim == 4
    if k1_major:
        k2_tile = tm // FFT_N1
        assert tm % FFT_N1 == 0 and k2_tile % 8 == 0
        tpb = tiles_per_batch
        yf_spec = pl.BlockSpec((None, FFT_N1, k2_tile, BW), lambda i: (i // tpb, 0, i % tpb, 0))
    else:
        yf_spec = tile(BW)
    in_specs = [
        tile(D), tile(D), tile(BW), tile(BW), yf_spec, ms,
        _const_spec((D, BW)), _const_spec((1, BW)),
        _const_spec((D, BW)), _const_spec((D, BW)), _const_spec((D, BW)), _const_spec((D, BW)),
        _const_spec((D, BW)),
        _pick_spec(layer, (NGROUP, SGU_CHUNK, SGU_CHUNK)), _const_spec((SGU_CHUNK, BW)),
        _const_spec((3, BW)), _const_spec((4, BW, D)), _const_spec((4, D, D)),
        _const_spec((4, 1, D)), _const_spec((D, D)),
    ]
    return pl.pallas_call(
        functools.partial(_mix_kernel, tm=tm, row_len=row_len, k1_major=k1_major),
        grid=(n // tm,),
        in_specs=in_specs,
        out_specs=tile(D),
        out_shape=jax.ShapeDtypeStruct((n, D), F32),
        scratch_shapes=[pltpu.VMEM((tm, BW), BF)],
        compiler_params=_params(1),
        name="mix",
    )(x, h, of, ob, yf, gate, w["r"], w["go"], w["su"], w["sv"], w["cb"], w["cc"], w["cx"],
      w["sgu"], w["bsgu"], w["conv"], w["branch"], w["gate"], w["bgate"], w["out"])


_IN_EDGES = np.cumsum([0, HK, HK, BW, BW, 2 * RANK, BW, BW, BW, BW, BW, BW])
_IN_NAMES = ("q", "k", "v", "r", "a", "su", "sv", "fn", "cb", "cc", "cx")


def _in_columns(w_in_layer):
    return {name: w_in_layer[:, _IN_EDGES[j]:_IN_EDGES[j + 1]].astype(BF)
            for j, name in enumerate(_IN_NAMES)}


def _layer_weights(i, w_in_layer, w_gla_a2, b_gla_a2, g_gla_norm, w_sgu, b_sgu, w_conv, w_branch,
                   w_gate, b_gate, w_out):
    cols = _in_columns(w_in_layer)
    q, k, v, r, a, su, sv, fn, cb, cc, cx = [cols[name] for name in _IN_NAMES]
    a2 = w_gla_a2[i].astype(BF)
    a2_tok = jnp.zeros((2, LANES, HK), BF)
    a2_tok = a2_tok.at[0, :RANK].set(a2[0]).at[1, RANK:2 * RANK].set(a2[1])
    cc_, sc_ = _dft_cos_sin(GC)
    return {
        "q": q, "kT": k.T, "v": v, "r": r,
        "a": jnp.pad(a, ((0, 0), (0, LANES - 2 * RANK))), "aT": a.T,
        "a2": a2_tok, "a2T": jnp.transpose(a2_tok[:, :2 * RANK, :], (0, 2, 1)),
        "ba2": b_gla_a2[i][:, None, :], "ba2T": b_gla_a2[i][:, :, None],
        "go": g_gla_norm[i].reshape(1, BW),
        "su": su, "sv": sv, "fn": fn, "cb": cb, "cc": cc, "cx": cx,
        "ccs": _bf16_const(np.concatenate([cc_, -sc_], axis=1)),
        "layer": i,
        "sgu": w_sgu.astype(BF),
        "merge_casts": (_cast_job(w_gate.reshape(DEPTH, 4 * D, D), (i,), 4 * D, D),
                        _cast_job(w_branch.reshape(DEPTH, 4 * BW, D), (i,), 4 * BW, D),
                        _cast_job(w_out, (i,), D, D)),
        "bsgu": jnp.repeat(jnp.transpose(b_sgu[i]), GC, axis=1),
        "conv": w_conv[i],
        "bgate": b_gate[i][:, None, :],
    }


def kernel(x, c, ctx, c_ctx, w_ada, b_ada, g_norm, w_ff1, w_ff3, w_ff2, w_in, w_gla_a2, b_gla_a2,
           g_gla_norm, w_sgu, b_sgu, w_conv, w_branch, w_gate, b_gate, w_out, g_final):
    batch, seq, _ = x.shape
    ctx_len = ctx.shape[1]
    assert seq % TM_LAT == 0 and TM_LAT % GRID_W == 0 and seq == FFT_N1 * FFT_N2
    assert seq % TM_FFN == 0
    assert ctx_len % GLA_C == 0
    tpb = seq // TM_LAT

    cvec = jnp.concatenate([c, c_ctx[None, :]], axis=0)
    cb = jnp.broadcast_to(cvec[:, :, None], (3, D, LANES))
    mods = _adaln(cb, w_ada, b_ada).reshape(DEPTH, 3, N_MOD, 1, D)

    xs = x.reshape(batch * seq, D)
    cs = ctx.reshape(batch * ctx_len, D)
    lat = dict(tm=TM_LAT, tiles_per_batch=tpb, per_batch=True)
    lat_ffn = dict(tm=TM_FFN, tiles_per_batch=seq // TM_FFN, per_batch=True)
    cx_ = dict(tm=batch * ctx_len, tiles_per_batch=1, per_batch=False)
    s_zero = jnp.zeros((batch, 2, 2, 2 * DK, 2 * DV), F32)

    def ff_casts(layer, half):
        lead = (layer, half)
        return (_cast_job(w_ff1, lead, D, D_FF), _cast_job(w_ff3, lead, D, D_FF),
                _cast_job(w_ff2, lead, D_FF, D))

    ff = [w_ff1[0, 0].astype(BF), w_ff3[0, 0].astype(BF), w_ff2[0, 0].astype(BF)]

    for i in range(DEPTH):
        last = i == DEPTH - 1
        ml = [mods[i, 0:2, j] for j in range(N_MOD)]
        mc = [mods[i, 2:3, j] for j in range(N_MOD)]
        gn = g_norm[i][:, None, :]
        w = _layer_weights(i, w_in[i], w_gla_a2, b_gla_a2, g_gla_norm, w_sgu, b_sgu, w_conv,
                           w_branch, w_gate, b_gate, w_out)

        xs, hl, *gla_l, merge_w = _ffn(xs, gn[0], ml[0], ml[1], ml[2], *ff,
                                       prep=(gn[1], ml[3], ml[4], w, False),
                                       casts=w["merge_casts"], **lat_ffn)
        w["gate"] = merge_w[0].reshape(4, D, D)
        w["branch"] = merge_w[1].reshape(4, BW, D)
        w["out"] = merge_w[2]
        cs, hc, *gla_c, wrc, wic = _ffn(cs, gn[0], mc[0], mc[1], mc[2], *ff,
                                        prep=(gn[1], mc[3], mc[4], w, True), **cx_)

        ofc, obc, s_ctx = _gla(*gla_c, s_zero, w, tm=ctx_len, tiles_per_batch=1)
        ofl, obl, _, ff = _gla(*gla_l, s_ctx, w, tm=TM_LAT, tiles_per_batch=tpb,
                               casts=ff_casts(i, 1))
        yf = _fnet_latent(xs.reshape(batch, seq, D), gn[1], ml[3], ml[4], w, batch)
        xs = _mix(xs, hl, ofl, obl, yf, ml[5], w, row_len=GRID_W, **lat)
        if last:
            xs = _ffn(xs, gn[2], ml[6], ml[7], ml[8], *ff, g_final=g_final[None, :], **lat_ffn)
        else:
            yfc = _fft_direct(wrc, wic, batch, ctx_len)
            cs = _mix(cs, hc, ofc, obc, yfc, mc[5], w, row_len=ctx_len, **cx_)
            cs = _ffn(cs, gn[2], mc[6], mc[7], mc[8], *ff, **cx_)
            xs, ff = _ffn(xs, gn[2], ml[6], ml[7], ml[8], *ff, casts=ff_casts(i + 1, 0), **lat_ffn)
    return xs.reshape(batch, seq, D)
```
